```python
import jax
import jax.numpy as jnp
from jax import lax
import numpy as np

D_MODEL = 1024
BATCH = 1
SEQ = 16384
DEPTH = 2
DEC_BATCH = 32
DEC_SEQ = 1
PAST_LEN = 16384
PAGE_SIZE = 128

RET_HEADS = 8
RET_DK = 64
RET_DV = 64
RET_QK = RET_HEADS * RET_DK
RET_V = RET_HEADS * RET_DV
RET_CHUNK = 128
ATT_HEADS = 8
ATT_HD = 64
ATT_W = ATT_HEADS * ATT_HD
MOBA_BLOCK = 256
MOBA_TOPK = 3
Q_BLOCK = 128
N_GROUPS = 4
EXP_PER_GROUP = 4
N_EXPERTS = N_GROUPS * EXP_PER_GROUP
TOPK_IN_GROUP = 2
D_EXPERT = 512
IN_SPLIT = (RET_QK, 2 * RET_QK, 2 * RET_QK + RET_V, 2 * RET_QK + 2 * RET_V,
            2 * RET_QK + 2 * RET_V + ATT_W, 2 * RET_QK + 2 * RET_V + 2 * ATT_W,
            2 * RET_QK + 2 * RET_V + 3 * ATT_W, 2 * RET_QK + 2 * RET_V + 3 * ATT_W + D_MODEL)
N_IN = 2 * RET_QK + 2 * RET_V + 3 * ATT_W + 2 * D_MODEL
EPS = 1e-6
POOL_NUM = 5
POOL_DEN = 4

kernel_name = 'retention_moba_hmoe_decoder_step'


def rms_norm(x, g):
    x32 = x.astype(jnp.float32)
    y = x32 * lax.rsqrt(jnp.mean(x32 * x32, axis=-1, keepdims=True) + EPS)
    return (y * g.astype(jnp.float32)).astype(x.dtype)


def ada_mod(c, w_ada, b_ada):
    m = (jax.nn.silu(c) @ w_ada + b_ada)[:, None, :]
    return jnp.split(m, 6, axis=-1)


def modulate(x, g, shift, scale):
    return rms_norm(x, g) * (1 + scale) + shift


def alibi_slopes():
    return 2.0 ** (-8.0 * jnp.arange(1, ATT_HEADS + 1, dtype=jnp.float32) / ATT_HEADS)


def retention_log_decay():
    return jnp.log1p(-(2.0 ** (-5.0 - jnp.arange(RET_HEADS, dtype=jnp.float32))))


def retention_chunk(state, q, k, v, log_gamma):
    c = q.shape[1]
    idx = jnp.arange(c, dtype=jnp.float32)
    diff = idx[:, None] - idx[None, :]
    decay = jnp.where(diff >= 0, jnp.exp(log_gamma[:, None, None] * jnp.maximum(diff, 0.0)), 0.0)
    scores = jnp.einsum('bnhd,bmhd->bhnm', q, k) * decay[None]
    inner = jnp.einsum('bhnm,bmhe->bnhe', scores, v)
    xi = jnp.exp(log_gamma[None, :] * (idx[:, None] + 1.0))
    cross = jnp.einsum('bnhd,bhde->bnhe', q, state) * xi[None, :, :, None]
    zeta = jnp.exp(log_gamma[None, :] * (c - 1.0 - idx[:, None]))
    new_state = (jnp.exp(log_gamma * c)[None, :, None, None] * state
                 + jnp.einsum('bmhd,bmhe->bhde', k * zeta[None, :, :, None], v))
    return new_state, inner + cross


def retention_prompt(q, k, v, log_gamma):
    b, s, h, dk = q.shape
    dv = v.shape[-1]
    nc = s // RET_CHUNK

    def chunks(a):
        return a.reshape(b, nc, RET_CHUNK, h, a.shape[-1]).transpose(1, 0, 2, 3, 4)

    def step(state, xs):
        return retention_chunk(state, xs[0], xs[1], xs[2], log_gamma)

    state0 = jnp.zeros((b, h, dk, dv), jnp.float32)
    final, o = lax.scan(step, state0, (chunks(q), chunks(k), chunks(v)))
    return o.transpose(1, 0, 2, 3, 4).reshape(b, s, h, dv), final


def head_norm(o):
    mu = jnp.mean(o, axis=-1, keepdims=True)
    var = jnp.mean(jnp.square(o - mu), axis=-1, keepdims=True)
    return (o - mu) * lax.rsqrt(var + EPS)


def pad_blocks(a):
    t = a.shape[1]
    tp = -(-t // MOBA_BLOCK) * MOBA_BLOCK
    return jnp.pad(a, ((0, 0), (0, tp - t), (0, 0), (0, 0)))


def block_means(k_pad):
    b, t, h, d = k_pad.shape
    kb = k_pad.astype(jnp.float32).reshape(b, t // MOBA_BLOCK, MOBA_BLOCK, h, d)
    return kb.mean(axis=2).transpose(0, 2, 1, 3)


def moba_queries(q, pos, k_bh, v_bh, means, slopes):
    b, h, t, d = k_bh.shape
    nb = t // MOBA_BLOCK
    n_sel = min(MOBA_TOPK, nb)
    qh = q.transpose(0, 2, 1, 3).astype(jnp.float32)
    own = pos // MOBA_BLOCK
    gate = jnp.einsum('bhqd,bhnd->bhqn', qh, means)
    gate = jnp.where(jnp.arange(nb)[None, :] < own[:, None], gate, -jnp.inf)
    _, sel = lax.top_k(gate, n_sel)
    valid = jnp.arange(n_sel)[None, :] < own[:, None]
    sel_pos = sel[..., None] * MOBA_BLOCK + jnp.arange(MOBA_BLOCK)
    bi = jnp.arange(b)[:, None, None, None, None]
    hi = jnp.arange(h)[None, :, None, None, None]
    k_sel = k_bh[bi, hi, sel_pos].astype(jnp.float32)
    v_sel = v_bh[bi, hi, sel_pos].astype(jnp.float32)
    dist_sel = (pos[:, None, None] - sel_pos).astype(jnp.float32)
    s_sel = jnp.einsum('bhqd,bhqncd->bhqnc', qh, k_sel) - slopes[None, :, None, None, None] * dist_sel
    s_sel = jnp.where(valid[None, None, :, :, None], s_sel, -jnp.inf).reshape(b, h, -1, n_sel * MOBA_BLOCK)
    own_pos = own[:, None] * MOBA_BLOCK + jnp.arange(MOBA_BLOCK)
    k_own = k_bh[:, :, own_pos].astype(jnp.float32)
    v_own = v_bh[:, :, own_pos].astype(jnp.float32)
    dist_own = (pos[:, None] - own_pos).astype(jnp.float32)
    s_own = jnp.einsum('bhqd,bhqcd->bhqc', qh, k_own) - slopes[None, :, None, None] * dist_own
    s_own = jnp.where((own_pos <= pos[:, None])[None, None], s_own, -jnp.inf)
    p = jax.nn.softmax(jnp.concatenate([s_sel, s_own], axis=-1), axis=-1)
    p_sel = p[..., :n_sel * MOBA_BLOCK].reshape(b, h, -1, n_sel, MOBA_BLOCK)
    p_own = p[..., n_sel * MOBA_BLOCK:]
    o = (jnp.einsum('bhqnc,bhqncd->bhqd', p_sel, v_sel)
         + jnp.einsum('bhqc,bhqcd->bhqd', p_own, v_own))
    return o.transpose(0, 2, 1, 3)


def moba_prompt(q, k, v, slopes):
    b, s, h, d = q.shape
    kp = pad_blocks(k)
    vp = pad_blocks(v)
    means = block_means(kp)
    k_bh = kp.transpose(0, 2, 1, 3)
    v_bh = vp.transpose(0, 2, 1, 3)
    nq = s // Q_BLOCK
    qs = q.reshape(b, nq, Q_BLOCK, h, d).transpose(1, 0, 2, 3, 4)
    ps = jnp.arange(s, dtype=jnp.int32).reshape(nq, Q_BLOCK)
    o = lax.map(lambda xs: moba_queries(xs[0], xs[1], k_bh, v_bh, means, slopes), (qs, ps))
    return o.transpose(1, 0, 2, 3, 4).reshape(b, s, h, d)


def moba_sample(q, k_new, v_new, cache_k_l, cache_v_l, page_table, slopes):
    bd, n_pages = page_table.shape
    past_len = n_pages * PAGE_SIZE
    h, d = cache_k_l.shape[-2], cache_k_l.shape[-1]
    k_past = cache_k_l[page_table].reshape(bd, past_len, h, d)
    v_past = cache_v_l[page_table].reshape(bd, past_len, h, d)
    kp = pad_blocks(jnp.concatenate([k_past, k_new.astype(k_past.dtype)], axis=1))
    vp = pad_blocks(jnp.concatenate([v_past, v_new.astype(v_past.dtype)], axis=1))
    means = block_means(kp)
    pos = past_len + jnp.arange(q.shape[1], dtype=jnp.int32)
    return moba_queries(q, pos, kp.transpose(0, 2, 1, 3), vp.transpose(0, 2, 1, 3), means, slopes)


def mixer_project(x, g_mix, shift, scale, w_in):
    b, s, _ = x.shape
    h = modulate(x, g_mix, shift, scale)
    rq, rk, rv, rg, aq, ak, av, gr, ga = jnp.split(h @ w_in, IN_SPLIT, axis=-1)
    rq = rq.reshape(b, s, RET_HEADS, RET_DK).astype(jnp.float32)
    rk = rk.reshape(b, s, RET_HEADS, RET_DK).astype(jnp.float32) * RET_DK ** -0.5
    rv = rv.reshape(b, s, RET_HEADS, RET_DV).astype(jnp.float32)
    aq = aq.reshape(b, s, ATT_HEADS, ATT_HD) * ATT_HD ** -0.5
    ak = ak.reshape(b, s, ATT_HEADS, ATT_HD)
    av = av.reshape(b, s, ATT_HEADS, ATT_HD)
    return rq, rk, rv, rg, aq, ak, av, gr, ga


def mixer_merge(x, gate, o_ret, rg, o_att, gr, ga, w_ret_o, w_att_o, w_out):
    b, s, _ = x.shape
    ret = (jax.nn.silu(rg) * head_norm(o_ret).reshape(b, s, RET_V).astype(x.dtype)) @ w_ret_o
    att = o_att.reshape(b, s, ATT_W).astype(x.dtype) @ w_att_o
    merged = jax.nn.sigmoid(gr) * ret + jax.nn.sigmoid(ga) * att
    return x + gate * (merged @ w_out)


def hier_route(h, w_rg, b_rg, w_re, b_re):
    b, s, _ = h.shape
    p_group = jax.nn.softmax((h @ w_rg + b_rg).astype(jnp.float32), axis=-1)
    p_top, g_idx = lax.top_k(p_group, 1)
    lg_e = (h @ w_re + b_re).astype(jnp.float32).reshape(b, s, N_GROUPS, EXP_PER_GROUP)
    lg_in = jnp.einsum('bsge,bsg->bse', lg_e, jax.nn.one_hot(g_idx[..., 0], N_GROUPS, dtype=jnp.float32))
    e_val, e_idx = lax.top_k(lg_in, TOPK_IN_GROUP)
    w = jax.nn.softmax(e_val, axis=-1) * p_top
    flat = g_idx * EXP_PER_GROUP + e_idx
    return jnp.sum(jax.nn.one_hot(flat, N_EXPERTS, dtype=jnp.float32) * w[..., None], axis=-2)


def moe_sublayer(x, g_ffn, shift, scale, gate, w_rg, b_rg, w_re, b_re, w_gate_e, w_up_e, w_down_e):
    h = modulate(x, g_ffn, shift, scale)
    wts = hier_route(h, w_rg, b_rg, w_re, b_re).astype(h.dtype)
    hg = jnp.einsum('bsd,edf->bsef', h, w_gate_e)
    hu = jnp.einsum('bsd,edf->bsef', h, w_up_e)
    act = jax.nn.silu(hg) * hu * wts[..., None]
    return x + gate * jnp.einsum('bsef,efd->bsd', act, w_down_e)


def setup_inputs(seed: int = 0) -> dict:
    key = jax.random.key(seed)
    ks = jax.random.split(key, 24)
    n_pages = PAST_LEN // PAGE_SIZE
    n_pool = (DEC_BATCH * n_pages * POOL_NUM + POOL_DEN - 1) // POOL_DEN

    def nrm(k, shape, s):
        return jax.random.normal(k, shape, jnp.float32) * s

    perm = jax.random.permutation(ks[5], n_pool)
    page_table = perm[:DEC_BATCH * n_pages].reshape(DEC_BATCH, n_pages).astype(jnp.int32)
    return {
        'x_prompt': nrm(ks[0], (BATCH, SEQ, D_MODEL), 1.0),
        'x_sample': nrm(ks[1], (DEC_BATCH, DEC_SEQ, D_MODEL), 1.0),
        'cache_k': nrm(ks[2], (DEPTH, n_pool, PAGE_SIZE, ATT_HEADS, ATT_HD), 1.0),
        'cache_v': nrm(ks[3], (DEPTH, n_pool, PAGE_SIZE, ATT_HEADS, ATT_HD), 1.0),
        'state_ret': nrm(ks[4], (DEPTH, DEC_BATCH, RET_HEADS, RET_DK, RET_DV), 1.0),
        'page_table': page_table,
        'c_prompt': nrm(ks[6], (BATCH, D_MODEL), 1.0),
        'c_sample': nrm(ks[7], (DEC_BATCH, D_MODEL), 1.0),
        'w_ada': nrm(ks[8], (DEPTH, D_MODEL, 6 * D_MODEL), 0.5 * D_MODEL ** -0.5),
        'b_ada': nrm(ks[9], (DEPTH, 6 * D_MODEL), 0.01),
        'g_mix': 1.0 + nrm(ks[10], (DEPTH, D_MODEL), 0.05),
        'w_in': nrm(ks[11], (DEPTH, D_MODEL, N_IN), D_MODEL ** -0.5),
        'w_ret_o': nrm(ks[12], (DEPTH, RET_V, D_MODEL), RET_V ** -0.5),
        'w_att_o': nrm(ks[13], (DEPTH, ATT_W, D_MODEL), ATT_W ** -0.5),
        'w_out': nrm(ks[14], (DEPTH, D_MODEL, D_MODEL), D_MODEL ** -0.5),
        'g_ffn': 1.0 + nrm(ks[15], (DEPTH, D_MODEL), 0.05),
        'w_rg': nrm(ks[16], (DEPTH, D_MODEL, N_GROUPS), D_MODEL ** -0.5),
        'b_rg': nrm(ks[17], (DEPTH, N_GROUPS), 0.01),
        'w_re': nrm(ks[18], (DEPTH, D_MODEL, N_EXPERTS), D_MODEL ** -0.5),
        'b_re': nrm(ks[19], (DEPTH, N_EXPERTS), 0.01),
        'w_gate_e': nrm(ks[20], (DEPTH, N_EXPERTS, D_MODEL, D_EXPERT), D_MODEL ** -0.5),
        'w_up_e': nrm(ks[21], (DEPTH, N_EXPERTS, D_MODEL, D_EXPERT), D_MODEL ** -0.5),
        'w_down_e': nrm(ks[22], (DEPTH, N_EXPERTS, D_EXPERT, D_MODEL), D_EXPERT ** -0.5),
        'g_final': 1.0 + nrm(ks[23], (D_MODEL,), 0.05),
    }


def reference(x_prompt, x_sample, cache_k, cache_v, state_ret, page_table, c_prompt, c_sample,
              w_ada, b_ada, g_mix, w_in, w_ret_o, w_att_o, w_out, g_ffn, w_rg, b_rg, w_re, b_re,
              w_gate_e, w_up_e, w_down_e, g_final):
    slopes = alibi_slopes()
    log_gamma = retention_log_decay()
    xp = x_prompt
    xs = x_sample
    kp_l, vp_l, rp_l, ks_l, vs_l, rs_l = [], [], [], [], [], []
    for l in range(DEPTH):
        mp = ada_mod(c_prompt, w_ada[l], b_ada[l])
        ms = ada_mod(c_sample, w_ada[l], b_ada[l])
        rq, rk, rv, rg, aq, ak, av, gr, ga = mixer_project(xp, g_mix[l], mp[0], mp[1], w_in[l])
        o_ret, st = retention_prompt(rq, rk, rv, log_gamma)
        o_att = moba_prompt(aq, ak, av, slopes)
        xp = mixer_merge(xp, mp[2], o_ret, rg, o_att, gr, ga, w_ret_o[l], w_att_o[l], w_out[l])
        xp = moe_sublayer(xp, g_ffn[l], mp[3], mp[4], mp[5], w_rg[l], b_rg[l], w_re[l], b_re[l],
                          w_gate_e[l], w_up_e[l], w_down_e[l])
        kp_l.append(ak)
        vp_l.append(av)
        rp_l.append(st.astype(x_prompt.dtype))
        rq, rk, rv, rg, aq, ak, av, gr, ga = mixer_project(xs, g_mix[l], ms[0], ms[1], w_in[l])
        st, o_ret = retention_chunk(state_ret[l].astype(jnp.float32), rq, rk, rv, log_gamma)
        o_att = moba_sample(aq, ak, av, cache_k[l], cache_v[l], page_table, slopes)
        xs = mixer_merge(xs, ms[2], o_ret, rg, o_att, gr, ga, w_ret_o[l], w_att_o[l], w_out[l])
        xs = moe_sublayer(xs, g_ffn[l], ms[3], ms[4], ms[5], w_rg[l], b_rg[l], w_re[l], b_re[l],
                          w_gate_e[l], w_up_e[l], w_down_e[l])
        ks_l.append(ak)
        vs_l.append(av)
        rs_l.append(st.astype(state_ret.dtype))
    y_prompt = rms_norm(xp, g_final)
    y_sample = rms_norm(xs, g_final)
    k_prompt = jnp.stack(kp_l)
    v_prompt = jnp.stack(vp_l)
    ret_prompt = jnp.stack(rp_l)
    k_sample = jnp.stack(ks_l)
    v_sample = jnp.stack(vs_l)
    ret_sample = jnp.stack(rs_l)
    return (y_prompt, y_sample, k_prompt, v_prompt, ret_prompt, k_sample, v_sample, ret_sample)
```

```python
import functools

import jax
import jax.numpy as jnp
from jax import lax
from jax.experimental import pallas as pl
from jax.experimental.pallas import tpu as pltpu

EPS = 1e-6
RET_HEADS = 8
RET_DK = 64
RET_DV = 64
ATT_HEADS = 8
ATT_HD = 64
HEAD_W = RET_HEADS * RET_DK
MOBA_BLOCK = 256
MOBA_TOPK = 3
PAGE_SIZE = 128
PAGES_PER_BLOCK = MOBA_BLOCK // PAGE_SIZE
N_GROUPS = 4
EXP_PER_GROUP = 4
N_EXPERTS = N_GROUPS * EXP_PER_GROUP
SUBLANES = 8
LANES = 128
PAIRS = HEAD_W // LANES
NEG = -1e30
PAGES_PER_STEP = 16
VMEM_LIMIT = 56 * 1024 * 1024

F32 = jnp.float32
BF16 = jnp.bfloat16
HI = lax.Precision.HIGHEST
NT = (((1,), (1,)), ((), ()))
TN = (((0,), (0,)), ((), ()))


def _dot(a, b):
    return jnp.dot(a, b, preferred_element_type=F32)


def _dot_nt(a, b):
    return lax.dot_general(a, b, NT, preferred_element_type=F32)


def _dot_hi(a, b):
    return jnp.dot(a, b, preferred_element_type=F32, precision=HI)


def _dot_nt_hi(a, b):
    return lax.dot_general(a, b, NT, preferred_element_type=F32, precision=HI)


def _mm(a, w_ref, exact):
    if exact:
        return _dot_hi(a.astype(F32), w_ref[...])
    return _dot(a.astype(BF16), w_ref[...])


def _params(*sem):
    return pltpu.CompilerParams(dimension_semantics=sem, vmem_limit_bytes=VMEM_LIMIT)


def _full(shape):
    n = len(shape)
    return pl.BlockSpec(shape, lambda *_: (0,) * n)


def _modulate(x, g, shift, scale):
    ms = jnp.mean(x * x, axis=-1, keepdims=True)
    y = x * lax.rsqrt(ms + EPS) * g
    return y * (1.0 + scale) + shift


def _silu(x):
    return x * jax.nn.sigmoid(x)


def _head_lane_mask(shape, a):
    lane = lax.broadcasted_iota(jnp.int32, shape, len(shape) - 1)
    return (lane // RET_DK) == a


def _top3_rows(g, n_rows):
    row = lax.broadcasted_iota(jnp.int32, g.shape, 0)
    sel = jnp.zeros(g.shape, F32)
    idxs = []
    for _ in range(MOBA_TOPK):
        cmax = jnp.max(g, axis=0, keepdims=True)
        idx = jnp.min(jnp.where(g == cmax, row, n_rows), axis=0, keepdims=True)
        pick = jnp.logical_and(row == idx, cmax > -jnp.inf)
        sel = jnp.where(pick, 1.0, sel)
        g = jnp.where(pick, -jnp.inf, g)
        idxs.append(idx)
    return sel, idxs


def _ada_kernel(c_ref, w_ref, b_ref, o_ref):
    c = c_ref[...]
    o_ref[...] = _dot_hi(_silu(c), w_ref[...]) + b_ref[...]


def _ada(c_all, w_ada, b_ada):
    depth, d, d6 = w_ada.shape
    r = c_all.shape[0]
    return pl.pallas_call(
        _ada_kernel,
        grid=(depth, d6 // d),
        in_specs=[
            pl.BlockSpec((r, d), lambda l, j: (0, 0)),
            pl.BlockSpec((None, d, d), lambda l, j: (l, 0, j)),
            pl.BlockSpec((None, 1, d), lambda l, j: (l, 0, j)),
        ],
        out_specs=pl.BlockSpec((None, r, d), lambda l, j: (l, 0, j)),
        out_shape=jax.ShapeDtypeStruct((depth, r, d6), F32),
        compiler_params=_params("parallel", "parallel"),
        name="ada_mod",
    )(c_all, w_ada, b_ada.reshape(depth, 1, d6))


def _proj_prompt_kernel(x_ref, g_ref, sh_ref, sc_ref, w_ref, wkvt_ref,
                        rq_ref, rk_ref, rv_ref, rg_ref, aq_ref, akb_ref, akt_ref, avt_ref, avtb_ref,
                        gr_ref, ga_ref, sel_ref, means_ref):
    i = pl.program_id(0)
    nb = means_ref.shape[0]
    tm = x_ref.shape[0]
    w = HEAD_W

    @pl.when(i == 0)
    def _():
        means_ref[...] = jnp.zeros_like(means_ref)

    hb = _modulate(x_ref[...], g_ref[...], sh_ref[...], sc_ref[...]).astype(BF16)

    def col(c, n=1):
        return _dot(hb, w_ref[:, c * w:(c + n) * w])

    rq_ref[...] = col(0).astype(BF16)
    rk_ref[...] = (col(1) * RET_DK ** -0.5).astype(BF16)
    rv_ref[...] = col(2).astype(BF16)
    rg_ref[...] = col(3)
    q = col(4) * ATT_HD ** -0.5
    aq_ref[...] = q.astype(BF16)
    k = col(5)
    akb_ref[...] = k.astype(BF16)
    gr_ref[...] = col(7, 2)
    ga_ref[...] = col(9, 2)
    kvt = _dot_nt(wkvt_ref[...], hb)
    akt_ref[...] = kvt[:w]
    avt_ref[...] = kvt[w:]
    avtb_ref[...] = kvt[w:].astype(BF16)

    valid = lax.broadcasted_iota(jnp.int32, (nb, tm), 0) < i
    for p in range(PAIRS):
        mp = means_ref[:, p * LANES:(p + 1) * LANES]
        qp = q[:, p * LANES:(p + 1) * LANES]
        for a in range(2):
            qm = jnp.where(_head_lane_mask(qp.shape, a), qp, 0.0)
            gate = jnp.where(valid, _dot_nt_hi(mp, qm), -jnp.inf)
            sel, _ = _top3_rows(gate, nb)
            sel_ref[2 * p + a] = sel

    means_ref[pl.ds(i, 1), :] = jnp.mean(k, axis=0, keepdims=True)


def _proj_prompt(x, g, shift, scale, w_in_b, w_kvt_b):
    s, d = x.shape
    tm = MOBA_BLOCK
    assert s % tm == 0
    nb = s // tm
    w = HEAD_W
    n_in = w_in_b.shape[1]
    row = lambda width: pl.BlockSpec((tm, width), lambda i: (i, 0))
    colblk = pl.BlockSpec((w, tm), lambda i: (0, i))
    vec = pl.BlockSpec((1, d), lambda i: (0, 0))
    out_shape = (
        jax.ShapeDtypeStruct((s, w), BF16),
        jax.ShapeDtypeStruct((s, w), BF16),
        jax.ShapeDtypeStruct((s, w), BF16),
        jax.ShapeDtypeStruct((s, w), F32),
        jax.ShapeDtypeStruct((s, w), BF16),
        jax.ShapeDtypeStruct((nb, tm, w), BF16),
        jax.ShapeDtypeStruct((w, s), F32),
        jax.ShapeDtypeStruct((w, s), F32),
        jax.ShapeDtypeStruct((nb, w, tm), BF16),
        jax.ShapeDtypeStruct((s, d), F32),
        jax.ShapeDtypeStruct((s, d), F32),
        jax.ShapeDtypeStruct((ATT_HEADS, nb, s), F32),
    )
    out_specs = (
        row(w), row(w), row(w), row(w), row(w),
        pl.BlockSpec((None, tm, w), lambda i: (i, 0, 0)),
        colblk, colblk,
        pl.BlockSpec((None, w, tm), lambda i: (i, 0, 0)),
        row(d), row(d),
        pl.BlockSpec((ATT_HEADS, nb, tm), lambda i: (0, 0, i)),
    )
    return pl.pallas_call(
        _proj_prompt_kernel,
        grid=(nb,),
        in_specs=[row(d), vec, vec, vec,
                  pl.BlockSpec((d, n_in), lambda i: (0, 0), pipeline_mode=pl.Buffered(1)),
                  pl.BlockSpec((2 * w, d), lambda i: (0, 0), pipeline_mode=pl.Buffered(1))],
        out_specs=out_specs,
        out_shape=out_shape,
        scratch_shapes=[pltpu.VMEM((nb, w), F32)],
        compiler_params=_params("arbitrary"),
        name="proj_prompt",
    )(x, g, shift, scale, w_in_b, w_kvt_b)


def _proj_sample_kernel(x_ref, g_ref, sh_ref, sc_ref, w_ref, o_ref):
    h = _modulate(x_ref[...], g_ref[...], sh_ref[...], sc_ref[...])
    o_ref[...] = _dot_hi(h, w_ref[...])


def _proj_sample(x, g, shift, scale, w_in, layer):
    b, d = x.shape
    n_in = w_in.shape[2]
    tn = HEAD_W
    return pl.pallas_call(
        _proj_sample_kernel,
        grid=(n_in // tn,),
        in_specs=[_full((b, d)), _full((1, d)), _full((b, d)), _full((b, d)),
                  pl.BlockSpec((None, d, tn), lambda j: (layer, 0, j))],
        out_specs=pl.BlockSpec((b, tn), lambda j: (0, j)),
        out_shape=jax.ShapeDtypeStruct((b, n_in), F32),
        compiler_params=_params("parallel"),
        name="proj_sample",
    )(x, g, shift, scale, w_in)


def _retention_kernel(q_ref, k_ref, v_ref, rg_ref, decay_ref, xi_ref, zeta_ref, gc_ref, avg_ref,
                      o_ref, st_ref):
    i = pl.program_id(0)

    @pl.when(i == 0)
    def _():
        st_ref[...] = jnp.zeros_like(st_ref)

    c = q_ref.shape[0]
    row = lax.broadcasted_iota(jnp.int32, (LANES, LANES), 0)
    lane = lax.broadcasted_iota(jnp.int32, (LANES, LANES), 1)
    same_head = (row // RET_DK) == (lane // RET_DV)
    avg = avg_ref[...]
    for p in range(PAIRS):
        sl = slice(p * LANES, (p + 1) * LANES)
        q = q_ref[:, sl]
        k = k_ref[:, sl]
        v = v_ref[:, sl]
        state = st_ref[p]
        cross = _dot(q, state.astype(BF16)) * xi_ref[:, sl]
        inner = []
        for a in range(2):
            qm = jnp.where(_head_lane_mask(q.shape, a), q, jnp.zeros_like(q))
            scores = _dot_nt(qm, k) * decay_ref[2 * p + a]
            inner.append(_dot(scores.astype(BF16), v))
        o = jnp.where(_head_lane_mask((c, LANES), 0), inner[0], inner[1]) + cross
        kz = (k.astype(F32) * zeta_ref[:, sl]).astype(BF16)
        upd = lax.dot_general(kz, v, TN, preferred_element_type=F32)
        st_ref[p] = gc_ref[:, sl] * state + jnp.where(same_head, upd, 0.0)
        mu = _dot_hi(o, avg)
        dlt = o - mu
        var = _dot_hi(dlt * dlt, avg)
        hn = dlt * lax.rsqrt(var + EPS)
        o_ref[:, sl] = (_silu(rg_ref[:, sl]) * hn).astype(BF16)


def _ret_tables(c):
    h = jnp.arange(RET_HEADS, dtype=F32)
    log_gamma = jnp.log1p(-(2.0 ** (-5.0 - h)))
    idx = jnp.arange(c, dtype=F32)
    diff = idx[:, None] - idx[None, :]
    decay = jnp.where(diff >= 0, jnp.exp(log_gamma[:, None, None] * jnp.maximum(diff, 0.0)), 0.0)
    xi = jnp.exp(log_gamma[None, :] * (idx[:, None] + 1.0))
    zeta = jnp.exp(log_gamma[None, :] * (c - 1.0 - idx[:, None]))
    gc = jnp.exp(log_gamma * c)[None, :]
    rep = lambda t: jnp.repeat(t, RET_DV, axis=1)
    return decay, rep(xi), rep(zeta), rep(gc)


def _head_avg_matrix():
    r = jnp.arange(LANES)
    return jnp.where((r[:, None] // RET_DV) == (r[None, :] // RET_DV), 1.0 / RET_DV, 0.0).astype(F32)


def _retention_prompt(rq, rk, rv, rg, chunk):
    s, w = rq.shape
    assert s % chunk == 0
    decay, xi, zeta, gc = _ret_tables(chunk)
    row = pl.BlockSpec((chunk, w), lambda i: (i, 0))
    o, st = pl.pallas_call(
        _retention_kernel,
        grid=(s // chunk,),
        in_specs=[row, row, row, row,
                  _full((RET_HEADS, chunk, chunk)), _full((chunk, w)), _full((chunk, w)), _full((1, w)),
                  _full((LANES, LANES))],
        out_specs=(row, _full((PAIRS, LANES, LANES))),
        out_shape=(jax.ShapeDtypeStruct((s, w), BF16), jax.ShapeDtypeStruct((PAIRS, LANES, LANES), F32)),
        compiler_params=_params("arbitrary"),
        name="retention_prompt",
    )(rq, rk, rv, rg, decay, xi, zeta, gc, _head_avg_matrix())
    st = st.reshape(PAIRS, 2, RET_DK, 2, RET_DV)
    st = jnp.stack([st[:, 0, :, 0, :], st[:, 1, :, 1, :]], axis=1).reshape(RET_HEADS, RET_DK, RET_DV)
    return o, st


def _moba_prompt_kernel(q_ref, k_ref, vt_ref, sel_ref, bias_ref, o_ref, *, slopes):
    p = pl.program_id(0)
    i = pl.program_id(1)
    tq = q_ref.shape[0]
    tk = k_ref.shape[1]
    q = q_ref[...]
    kk = lax.broadcasted_iota(jnp.int32, (tk, tq), 0)
    qq = lax.broadcasted_iota(jnp.int32, (tk, tq), 1)
    causal = kk <= qq
    k_own = k_ref[i]
    vt_own = vt_ref[i]

    outs = []
    for a in range(2):
        qm = jnp.where(_head_lane_mask(q.shape, a), q, jnp.zeros_like(q))
        bias = bias_ref[a]
        vsl = slice(a * ATT_HD, (a + 1) * ATT_HD)
        slope = jnp.float32(slopes[a])
        for pp in range(1, PAIRS):
            slope = jnp.where(p == pp, jnp.float32(slopes[2 * pp + a]), slope)

        s = jnp.where(causal, _dot_nt(k_own, qm) + bias, NEG)
        m0 = jnp.max(s, axis=0, keepdims=True)
        pr = jnp.exp(s - m0)
        l0 = jnp.sum(pr, axis=0, keepdims=True)
        acc0 = _dot(vt_own[vsl, :], pr.astype(BF16))

        def body(n, carry, qm=qm, bias=bias, vsl=vsl, slope=slope, a=a):
            m, l, acc = carry
            off = slope * (MOBA_BLOCK * (n - i)).astype(F32)
            picked = sel_ref[a, pl.ds(n, 1), :] > 0.5
            s = jnp.where(picked, _dot_nt(k_ref[n], qm) + bias + off, NEG)
            m_new = jnp.maximum(m, jnp.max(s, axis=0, keepdims=True))
            alpha = jnp.exp(m - m_new)
            pr = jnp.exp(s - m_new)
            l = alpha * l + jnp.sum(pr, axis=0, keepdims=True)
            acc = alpha * acc + _dot(vt_ref[n][vsl, :], pr.astype(BF16))
            return m_new, l, acc

        _, l, acc = lax.fori_loop(0, i, body, (m0, l0, acc0))
        outs.append(acc / l)

    o_ref[...] = jnp.concatenate(outs, axis=0).T.astype(BF16)


def _moba_prompt(aq, akb, avtb, sel):
    s, w = aq.shape
    nb, tk, _ = akb.shape
    tq = MOBA_BLOCK
    h = jnp.arange(1, ATT_HEADS + 1, dtype=F32)
    slopes_arr = 2.0 ** (-8.0 * h / ATT_HEADS)
    kpos = jnp.arange(tk, dtype=F32)[:, None]
    qpos = jnp.arange(tq, dtype=F32)[None, :]
    bias = -slopes_arr[:, None, None] * (qpos - kpos)[None]
    slopes = tuple(2.0 ** (-8.0 * (j + 1) / ATT_HEADS) for j in range(ATT_HEADS))
    return pl.pallas_call(
        functools.partial(_moba_prompt_kernel, slopes=slopes),
        grid=(PAIRS, s // tq),
        in_specs=[
            pl.BlockSpec((tq, LANES), lambda p, i: (i, p)),
            pl.BlockSpec((nb, tk, LANES), lambda p, i: (0, 0, p)),
            pl.BlockSpec((nb, LANES, tk), lambda p, i: (0, p, 0)),
            pl.BlockSpec((2, nb, tq), lambda p, i: (p, 0, i)),
            pl.BlockSpec((2, tk, tq), lambda p, i: (p, 0, 0)),
        ],
        out_specs=pl.BlockSpec((tq, LANES), lambda p, i: (i, p)),
        out_shape=jax.ShapeDtypeStruct((s, w), BF16),
        compiler_params=_params("parallel", "arbitrary"),
        name="moba_prompt",
    )(aq, akb, avtb, sel, bias)


def _merge_router_kernel(x_ref, ret_ref, att_ref, gr_ref, ga_ref, g1_ref, wro_ref, wao_ref, wo_ref,
                         gf_ref, sh_ref, sc_ref, wr_ref, br_ref,
                         x1_ref, h2_ref, wts_ref, wt_scr, *, exact):
    ret = _mm(ret_ref[...], wro_ref, exact)
    att = _mm(att_ref[...], wao_ref, exact)
    merged = jax.nn.sigmoid(gr_ref[...]) * ret + jax.nn.sigmoid(ga_ref[...]) * att
    x1 = x_ref[...] + g1_ref[...] * _mm(merged, wo_ref, exact)
    x1_ref[...] = x1
    h2 = _modulate(x1, gf_ref[...], sh_ref[...], sc_ref[...])
    h2_ref[...] = h2.astype(h2_ref.dtype)

    logits = _dot_nt_hi(wr_ref[...], h2) + br_ref[...]
    lg = [logits[g:g + 1, :] for g in range(N_GROUPS)]
    le = [logits[N_GROUPS + e:N_GROUPS + e + 1, :] for e in range(N_EXPERTS)]
    mg = functools.reduce(jnp.maximum, lg)
    eg = [jnp.exp(v - mg) for v in lg]
    den = functools.reduce(jnp.add, eg)
    pg = [v / den for v in eg]
    p_top = functools.reduce(jnp.maximum, pg)
    taken = jnp.zeros(p_top.shape, jnp.bool_)
    onehot = []
    for g in range(N_GROUPS):
        hit = jnp.logical_and(pg[g] == p_top, jnp.logical_not(taken))
        onehot.append(hit)
        taken = jnp.logical_or(taken, hit)
    vin = []
    for j in range(EXP_PER_GROUP):
        v = jnp.zeros_like(p_top)
        for g in range(N_GROUPS):
            v = jnp.where(onehot[g], le[g * EXP_PER_GROUP + j], v)
        vin.append(v)
    rank = []
    for j in range(EXP_PER_GROUP):
        r = jnp.zeros(p_top.shape, jnp.int32)
        for kx in range(EXP_PER_GROUP):
            if kx == j:
                continue
            ahead = (vin[kx] > vin[j]) | ((vin[kx] == vin[j]) & (kx < j))
            r = r + ahead.astype(jnp.int32)
        rank.append(r)
    v1 = functools.reduce(jnp.maximum, vin)
    v2 = jnp.full_like(v1, -jnp.inf)
    for j in range(EXP_PER_GROUP):
        v2 = jnp.where(rank[j] == 1, vin[j], v2)
    e2 = jnp.exp(v2 - v1)
    w_first = 1.0 / (1.0 + e2)
    w_second = e2 / (1.0 + e2)
    wt_scr[...] = jnp.zeros_like(wt_scr)
    for g in range(N_GROUPS):
        for j in range(EXP_PER_GROUP):
            wj = jnp.where(rank[j] == 0, w_first, jnp.where(rank[j] == 1, w_second, 0.0)) * p_top
            e = g * EXP_PER_GROUP + j
            wt_scr[e:e + 1, :] = jnp.where(onehot[g], wj, 0.0)
    wts_ref[...] = wt_scr[...].T


def _merge_router(x, ret_in, att_in, gr, ga, gate1, wro, wao, wo, g_ffn, shift2, scale2, wr_t, br_col, tm, exact):
    s, d = x.shape
    w = HEAD_W
    assert s % tm == 0
    row = lambda width: pl.BlockSpec((tm, width), lambda i: (i, 0))
    mod = lambda a: pl.BlockSpec((tm if a.shape[0] > 1 else 1, d), lambda i: (i if a.shape[0] > 1 else 0, 0))
    if exact:
        layer = exact - 1
        wspec = lambda a: pl.BlockSpec((None,) + a.shape[1:], lambda i: (layer, 0, 0))
    else:
        wspec = lambda a: _full(a.shape)
    rr = wr_t.shape[0]
    return pl.pallas_call(
        functools.partial(_merge_router_kernel, exact=bool(exact)),
        grid=(s // tm,),
        in_specs=[row(d), row(w), row(w), row(d), row(d), mod(gate1),
                  wspec(wro), wspec(wao), wspec(wo),
                  _full((1, d)), mod(shift2), mod(scale2), _full((rr, d)), _full((rr, 1))],
        out_specs=(row(d), row(d), row(LANES)),
        out_shape=(jax.ShapeDtypeStruct((s, d), F32), jax.ShapeDtypeStruct((s, d), F32 if exact else BF16),
                   jax.ShapeDtypeStruct((s, LANES), F32)),
        scratch_shapes=[pltpu.VMEM((LANES, tm), F32)],
        compiler_params=_params("parallel"),
        name="merge_router",
    )(x, ret_in, att_in, gr, ga, gate1, wro, wao, wo, g_ffn, shift2, scale2, wr_t, br_col)


def _moe_kernel(x1_ref, h2_ref, wts_ref, g2_ref, wg_ref, wu_ref, wd_ref, gfin_ref, o_ref, acc_ref,
                *, final_norm, exact):
    e = pl.program_id(1)

    @pl.when(e == 0)
    def _():
        acc_ref[...] = jnp.zeros_like(acc_ref)

    h2 = h2_ref[...]
    wts = wts_ref[...]
    lane = lax.broadcasted_iota(jnp.int32, wts.shape, 1)
    wcol = jnp.sum(jnp.where(lane == e, wts, 0.0), axis=1, keepdims=True)
    act = _silu(_mm(h2, wg_ref, exact)) * _mm(h2, wu_ref, exact) * wcol
    acc_ref[...] += _mm(act, wd_ref, exact)

    @pl.when(e == pl.num_programs(1) - 1)
    def _():
        x2 = x1_ref[...] + g2_ref[...] * acc_ref[...]
        if final_norm:
            ms = jnp.mean(x2 * x2, axis=-1, keepdims=True)
            x2 = x2 * lax.rsqrt(ms + EPS) * gfin_ref[...]
        o_ref[...] = x2


def _moe(x1, h2, wts, gate2, wg, wu, wd, g_final, tm, final_norm, exact):
    s, d = x1.shape
    ne, f = wg.shape[-3], wg.shape[-1]
    assert s % tm == 0
    row = lambda width: pl.BlockSpec((tm, width), lambda i, e: (i, 0))
    mod = lambda a: pl.BlockSpec((tm if a.shape[0] > 1 else 1, d), lambda i, e: (i if a.shape[0] > 1 else 0, 0))
    if exact:
        layer = exact - 1
        wspec = lambda r, c: pl.BlockSpec((None, None, r, c), lambda i, e: (layer, e, 0, 0))
    else:
        wspec = lambda r, c: pl.BlockSpec((None, r, c), lambda i, e: (e, 0, 0))
    return pl.pallas_call(
        functools.partial(_moe_kernel, final_norm=final_norm, exact=bool(exact)),
        grid=(s // tm, ne),
        in_specs=[row(d), row(d), row(LANES), mod(gate2),
                  wspec(d, f), wspec(d, f), wspec(f, d),
                  pl.BlockSpec((1, d), lambda i, e: (0, 0))],
        out_specs=row(d),
        out_shape=jax.ShapeDtypeStruct((s, d), F32),
        scratch_shapes=[pltpu.VMEM((tm, d), F32)],
        compiler_params=_params("parallel", "arbitrary"),
        name="moe_experts",
    )(x1, h2, wts, gate2, wg, wu, wd, g_final)


def _paged_gate_kernel(pt_ref, q_ref, *rest):
    pages = rest[:PAGES_PER_STEP]
    sel_ref, qb_ref, acc_ref = rest[PAGES_PER_STEP:]
    j = pl.program_id(1)
    rows_per_block = ATT_HEADS * SUBLANES

    @pl.when(j == 0)
    def _():
        qb_ref[...] = jnp.broadcast_to(q_ref[...] * ATT_HD ** -0.5, qb_ref.shape)

    qb = qb_ref[...]
    blocks_per_step = PAGES_PER_STEP // PAGES_PER_BLOCK
    for t in range(blocks_per_step):
        tot = pages[PAGES_PER_BLOCK * t][...]
        for u in range(1, PAGES_PER_BLOCK):
            tot = tot + pages[PAGES_PER_BLOCK * t + u][...]
        part = (tot * qb).reshape(ATT_HEADS, ATT_HD // SUBLANES, SUBLANES, PAGE_SIZE).sum(axis=1)
        base = pl.multiple_of((j * blocks_per_step + t) * rows_per_block, rows_per_block)
        acc_ref[pl.ds(base, rows_per_block), :] = part.reshape(rows_per_block, PAGE_SIZE)

    @pl.when(j == pl.num_programs(1) - 1)
    def _():
        n_rows = acc_ref.shape[0]
        nb = n_rows // rows_per_block
        g = jnp.sum(acc_ref[...], axis=1, keepdims=True)
        r = lax.broadcasted_iota(jnp.int32, (n_rows, LANES), 0)
        lane = lax.broadcasted_iota(jnp.int32, (n_rows, LANES), 1)
        spread = jnp.where(lane == (r // SUBLANES) % ATT_HEADS, g, 0.0)
        gate = spread.reshape(nb, rows_per_block, LANES).sum(axis=1) * (1.0 / MOBA_BLOCK)
        _, idxs = _top3_rows(gate, nb)
        out = jnp.zeros(sel_ref.shape, jnp.int32)
        rr = lax.broadcasted_iota(jnp.int32, sel_ref.shape, 0)
        for t, idx in enumerate(idxs):
            out = jnp.where(rr == t, idx, out)
        sel_ref[...] = out


def _paged_gate(cache_kt, layer, page_table_flat, q_col, n_pages):
    b = q_col.shape[0]
    assert n_pages % PAGES_PER_STEP == 0 and PAGES_PER_STEP % PAGES_PER_BLOCK == 0
    nb = n_pages // PAGES_PER_BLOCK
    assert nb >= MOBA_TOPK

    def page_spec(g):
        return pl.BlockSpec((None, None, HEAD_W, PAGE_SIZE),
                            lambda bi, j, pt: (layer, pt[bi * n_pages + j * PAGES_PER_STEP + g], 0, 0))

    grid_spec = pltpu.PrefetchScalarGridSpec(
        num_scalar_prefetch=1,
        grid=(b, n_pages // PAGES_PER_STEP),
        in_specs=[pl.BlockSpec((None, HEAD_W, 1), lambda bi, j, pt: (bi, 0, 0))]
                 + [page_spec(g) for g in range(PAGES_PER_STEP)],
        out_specs=pl.BlockSpec((None, SUBLANES, LANES), lambda bi, j, pt: (bi, 0, 0)),
        scratch_shapes=[pltpu.VMEM((HEAD_W, PAGE_SIZE), F32),
                        pltpu.VMEM((nb * ATT_HEADS * SUBLANES, PAGE_SIZE), F32)],
    )
    return pl.pallas_call(
        _paged_gate_kernel,
        grid_spec=grid_spec,
        out_shape=jax.ShapeDtypeStruct((b, SUBLANES, LANES), jnp.int32),
        compiler_params=_params("parallel", "arbitrary"),
        name="paged_gate",
    )(page_table_flat, q_col, *([cache_kt] * PAGES_PER_STEP))


def _moba_sample_kernel(pt_ref, sel_ref, slopes_ref, q_ref, kn_ref, vn_ref, *rest, past_len):
    npg = MOBA_TOPK * PAGES_PER_BLOCK
    kp = rest[:npg]
    vp = rest[npg:2 * npg]
    o_ref = rest[2 * npg]
    b = pl.program_id(0)
    h = pl.program_id(1)
    slope = slopes_ref[h]
    lane = lax.broadcasted_iota(jnp.int32, (1, PAGE_SIZE), 1)
    q = q_ref[...] * ATT_HD ** -0.5
    scores = []
    for s in range(MOBA_TOPK):
        blk = sel_ref[(b * ATT_HEADS + h) * MOBA_TOPK + s]
        for t in range(PAGES_PER_BLOCK):
            kpos = blk * MOBA_BLOCK + t * PAGE_SIZE + lane
            dist = (past_len - kpos).astype(F32)
            sc = jnp.sum(kp[s * PAGES_PER_BLOCK + t][...] * q, axis=0, keepdims=True)
            scores.append(sc - slope * dist)
    s_own = jnp.sum(q * kn_ref[...], axis=0, keepdims=True)
    m = functools.reduce(jnp.maximum, [jnp.max(sc, axis=1, keepdims=True) for sc in scores] + [s_own])
    p_own = jnp.exp(s_own - m)
    l = p_own
    acc = p_own * vn_ref[...]
    for idx, sc in enumerate(scores):
        pr = jnp.exp(sc - m)
        l = l + jnp.sum(pr, axis=1, keepdims=True)
        acc = acc + jnp.sum(vp[idx][...] * pr, axis=1, keepdims=True)
    o_ref[...] = acc / l


def _moba_sample(cache_kt, cache_vt, layer, page_table_flat, sel_flat, q_col, k_col, v_col, n_pages):
    b = q_col.shape[0]
    npg = MOBA_TOPK * PAGES_PER_BLOCK
    h = jnp.arange(1, ATT_HEADS + 1, dtype=F32)
    slopes = 2.0 ** (-8.0 * h / ATT_HEADS)

    def page_spec(g):
        s, t = divmod(g, PAGES_PER_BLOCK)

        def imap(bi, hi, pt, sel):
            blk = sel[(bi * ATT_HEADS + hi) * MOBA_TOPK + s]
            return (layer, pt[bi * n_pages + blk * PAGES_PER_BLOCK + t], hi, 0)

        return pl.BlockSpec((None, None, ATT_HD, PAGE_SIZE), imap)

    vec = pl.BlockSpec((None, ATT_HD, 1), lambda bi, hi, pt, sel: (bi, hi, 0))
    grid_spec = pltpu.PrefetchScalarGridSpec(
        num_scalar_prefetch=2,
        grid=(b, ATT_HEADS),
        in_specs=[pl.BlockSpec(memory_space=pltpu.SMEM), vec, vec, vec]
                 + [page_spec(g) for g in range(npg)] * 2,
        out_specs=vec,
    )
    return pl.pallas_call(
        functools.partial(_moba_sample_kernel, past_len=n_pages * PAGE_SIZE),
        grid_spec=grid_spec,
        out_shape=jax.ShapeDtypeStruct((b, HEAD_W, 1), F32),
        compiler_params=_params("parallel", "arbitrary"),
        name="moba_sample",
    )(page_table_flat, sel_flat, slopes, q_col, k_col, v_col, *([cache_kt] * npg), *([cache_vt] * npg))


def _ret_sample_kernel(st_ref, q_ref, k_ref, v_ref, rg_ref, gamma_ref, ns_ref, o_ref):
    new = gamma_ref[...] * st_ref[...] + (k_ref[...] * RET_DK ** -0.5) * v_ref[...]
    ns_ref[...] = new
    o = jnp.sum(q_ref[...] * new, axis=1, keepdims=True)
    mu = jnp.mean(o, axis=-1, keepdims=True)
    dlt = o - mu
    var = jnp.mean(dlt * dlt, axis=-1, keepdims=True)
    o_ref[...] = _silu(rg_ref[...]) * (dlt * lax.rsqrt(var + EPS))


def _ret_sample(state, layer, rq, rk, rv, rg):
    b = state.shape[1]
    hh = jnp.arange(RET_HEADS, dtype=F32)
    gamma = jnp.exp(jnp.log1p(-(2.0 ** (-5.0 - hh)))).reshape(RET_HEADS, 1, 1)
    st_in = pl.BlockSpec((None, None, RET_HEADS, RET_DK, RET_DV), lambda i: (layer, i, 0, 0, 0))
    st_out = pl.BlockSpec((None, RET_HEADS, RET_DK, RET_DV), lambda i: (i, 0, 0, 0))
    col = pl.BlockSpec((None, RET_HEADS, RET_DK, 1), lambda i: (i, 0, 0, 0))
    rw = pl.BlockSpec((None, RET_HEADS, 1, RET_DV), lambda i: (i, 0, 0, 0))
    return pl.pallas_call(
        _ret_sample_kernel,
        grid=(b,),
        in_specs=[st_in, col, col, rw, rw, _full((RET_HEADS, 1, 1))],
        out_specs=(st_out, rw),
        out_shape=(jax.ShapeDtypeStruct(state.shape[1:], F32),
                   jax.ShapeDtypeStruct((b, RET_HEADS, 1, RET_DV), F32)),
        compiler_params=_params("parallel"),
        name="retention_sample",
    )(state, rq, rk, rv, rg, gamma)


def kernel(x_prompt, x_sample, cache_k, cache_v, state_ret, page_table, c_prompt, c_sample, w_ada, b_ada, g_mix, w_in, w_ret_o, w_att_o, w_out, g_ffn, w_rg, b_rg, w_re, b_re, w_gate_e, w_up_e, w_down_e, g_final):
    bp, s, d = x_prompt.shape
    bs, ds, _ = x_sample.shape
    assert bp == 1 and ds == 1
    depth = w_ada.shape[0]
    n_pool = cache_k.shape[1]
    n_pages = page_table.shape[1]
    w = HEAD_W

    n_rows = bp + bs
    pad = (-n_rows) % SUBLANES
    c_all = jnp.concatenate([c_prompt, c_sample, jnp.zeros((pad, d), F32)], axis=0)
    mods = _ada(c_all, w_ada, b_ada)

    cache_kt = cache_k.transpose(0, 1, 3, 4, 2).reshape(depth, n_pool, w, PAGE_SIZE)
    cache_vt = cache_v.transpose(0, 1, 3, 4, 2).reshape(depth, n_pool, w, PAGE_SIZE)
    pt_flat = page_table.reshape(-1)

    n_log = N_GROUPS + N_EXPERTS
    rr = 32
    g_fin = g_final.reshape(1, d)

    xp = x_prompt.reshape(s, d)
    xs = x_sample.reshape(bs, d)
    kp_l, vp_l, rp_l, ks_l, vs_l, rs_l = [], [], [], [], [], []
    for l in range(depth):
        last = l == depth - 1
        mp = [mods[l, 0:bp, j * d:(j + 1) * d] for j in range(6)]
        ms = [mods[l, bp:bp + bs, j * d:(j + 1) * d] for j in range(6)]
        w_in_b = w_in[l].astype(BF16)
        w_kvt_b = w_in[l][:, 5 * w:7 * w].T.astype(BF16)
        wro_b = w_ret_o[l].astype(BF16)
        wao_b = w_att_o[l].astype(BF16)
        wo_b = w_out[l].astype(BF16)
        wg_b = w_gate_e[l].astype(BF16)
        wu_b = w_up_e[l].astype(BF16)
        wd_b = w_down_e[l].astype(BF16)
        wr_t = jnp.concatenate([w_rg[l].T, w_re[l].T, jnp.zeros((rr - n_log, d), F32)], axis=0)
        br_col = jnp.concatenate([b_rg[l], b_re[l], jnp.zeros((rr - n_log,), F32)]).reshape(rr, 1)
        gm = g_mix[l].reshape(1, d)
        gf = g_ffn[l].reshape(1, d)

        rq, rk, rv, rg, aq, akb, akt, avt, avtb, gr, ga, sel = _proj_prompt(xp, gm, mp[0], mp[1], w_in_b, w_kvt_b)
        ret_in, st = _retention_prompt(rq, rk, rv, rg, chunk=MOBA_BLOCK)
        att_in = _moba_prompt(aq, akb, avtb, sel)
        x1, h2, wts = _merge_router(xp, ret_in, att_in, gr, ga, mp[2], wro_b, wao_b, wo_b, gf, mp[3], mp[4],
                                    wr_t, br_col, tm=512, exact=0)
        xp = _moe(x1, h2, wts, mp[5], wg_b, wu_b, wd_b, g_fin, tm=1024, final_norm=last, exact=0)
        kp_l.append(akt)
        vp_l.append(avt)
        rp_l.append(st.reshape(bp, RET_HEADS, RET_DK, RET_DV))

        proj = _proj_sample(xs, gm, ms[0], ms[1], w_in, l)
        cut = lambda j, n=1: proj[:, j * w:(j + n) * w]
        rq_s, rk_s, rv_s, rg_s, aq_s, ak_s, av_s = (cut(j) for j in range(7))
        gr_s, ga_s = cut(7, 2), cut(9, 2)
        new_state, ret_s = _ret_sample(
            state_ret, l,
            rq_s.reshape(bs, RET_HEADS, RET_DK, 1), rk_s.reshape(bs, RET_HEADS, RET_DK, 1),
            rv_s.reshape(bs, RET_HEADS, 1, RET_DV), rg_s.reshape(bs, RET_HEADS, 1, RET_DV))
        q_col = aq_s.reshape(bs, w, 1)
        sel_s = _paged_gate(cache_kt, l, pt_flat, q_col, n_pages)
        sel_flat = sel_s[:, :MOBA_TOPK, :ATT_HEADS].transpose(0, 2, 1).reshape(-1)
        att_s = _moba_sample(cache_kt, cache_vt, l, pt_flat, sel_flat, q_col, ak_s.reshape(bs, w, 1),
                             av_s.reshape(bs, w, 1), n_pages)
        x1s, h2s, wts_s = _merge_router(xs, ret_s.reshape(bs, w), att_s.reshape(bs, w), gr_s, ga_s, ms[2],
                                        w_ret_o, w_att_o, w_out, gf, ms[3], ms[4], wr_t, br_col,
                                        tm=bs, exact=l + 1)
        xs = _moe(x1s, h2s, wts_s, ms[5], w_gate_e, w_up_e, w_down_e, g_fin, tm=bs, final_norm=last,
                  exact=l + 1)
        ks_l.append(ak_s.reshape(bs, 1, ATT_HEADS, ATT_HD))
        vs_l.append(av_s.reshape(bs, 1, ATT_HEADS, ATT_HD))
        rs_l.append(new_state)

    def kv_out(parts):
        return jnp.stack(parts).reshape(depth, bp, ATT_HEADS, ATT_HD, s).transpose(0, 1, 4, 2, 3)

    return (xp.reshape(bp, s, d), xs.reshape(bs, 1, d), kv_out(kp_l), kv_out(vp_l), jnp.stack(rp_l),
            jnp.stack(ks_l), jnp.stack(vs_l), jnp.stack(rs_l))
```

```python
import functools

import jax
import jax.numpy as jnp
from jax import lax
from jax.experimental import pallas as pl
from jax.experimental.pallas import tpu as pltpu

EPS = 1e-6
RET_HEADS = 8
RET_DK = 64
RET_DV = 64
ATT_HEADS = 8
ATT_HD = 64
HEAD_W = RET_HEADS * RET_DK
MOBA_BLOCK = 256
MOBA_TOPK = 3
PAGE_SIZE = 128
PAGES_PER_BLOCK = MOBA_BLOCK // PAGE_SIZE
N_GROUPS = 4
EXP_PER_GROUP = 4
N_EXPERTS = N_GROUPS * EXP_PER_GROUP
SUBLANES = 8
LANES = 128
PAIRS = HEAD_W // LANES
NEG = -1e30
ALIBI_SLOPES = tuple(2.0 ** (-8.0 * (j + 1) / ATT_HEADS) for j in range(ATT_HEADS))
PAGES_PER_STEP = 16
VMEM_LIMIT = 56 * 1024 * 1024

F32 = jnp.float32
BF16 = jnp.bfloat16
HI = lax.Precision.HIGHEST
NT = (((1,), (1,)), ((), ()))
TN = (((0,), (0,)), ((), ()))


def _dot(a, b):
    return jnp.dot(a, b, preferred_element_type=F32)


def _dot_nt(a, b):
    return lax.dot_general(a, b, NT, preferred_element_type=F32)


def _dot_hi(a, b):
    return jnp.dot(a, b, preferred_element_type=F32, precision=HI)


def _dot_nt_hi(a, b):
    return lax.dot_general(a, b, NT, preferred_element_type=F32, precision=HI)


def _mm(a, w_ref, exact):
    if exact:
        return _dot_hi(a.astype(F32), w_ref[...])
    return _dot(a.astype(BF16), w_ref[...])


def _params(*sem):
    return pltpu.CompilerParams(dimension_semantics=sem, vmem_limit_bytes=VMEM_LIMIT)


def _full(shape):
    n = len(shape)
    return pl.BlockSpec(shape, lambda *_: (0,) * n)


def _modulate(x, g, shift, scale):
    ms = jnp.mean(x * x, axis=-1, keepdims=True)
    y = x * lax.rsqrt(ms + EPS) * g
    return y * (1.0 + scale) + shift


def _silu(x):
    return x * jax.nn.sigmoid(x)


def _head_lane_mask(shape, a):
    lane = lax.broadcasted_iota(jnp.int32, shape, len(shape) - 1)
    return (lane // RET_DK) == a


def _top3_rows(g, n_rows):
    row = lax.broadcasted_iota(jnp.int32, g.shape, 0)
    sel = jnp.zeros(g.shape, F32)
    idxs = []
    for _ in range(MOBA_TOPK):
        cmax = jnp.max(g, axis=0, keepdims=True)
        idx = jnp.min(jnp.where(g == cmax, row, n_rows), axis=0, keepdims=True)
        pick = jnp.logical_and(row == idx, cmax > -jnp.inf)
        sel = jnp.where(pick, 1.0, sel)
        g = jnp.where(pick, -jnp.inf, g)
        idxs.append(idx)
    return sel, idxs


def _ada_kernel(c_ref, w_ref, b_ref, o_ref):
    c = c_ref[...]
    o_ref[...] = _dot_hi(_silu(c), w_ref[...]) + b_ref[...]


def _ada(c_all, w_ada, b_ada):
    depth, d, d6 = w_ada.shape
    r = c_all.shape[0]
    return pl.pallas_call(
        _ada_kernel,
        grid=(depth, d6 // d),
        in_specs=[
            pl.BlockSpec((r, d), lambda l, j: (0, 0)),
            pl.BlockSpec((None, d, d), lambda l, j: (l, 0, j)),
            pl.BlockSpec((None, 1, d), lambda l, j: (l, 0, j)),
        ],
        out_specs=pl.BlockSpec((None, r, d), lambda l, j: (l, 0, j)),
        out_shape=jax.ShapeDtypeStruct((depth, r, d6), F32),
        compiler_params=_params("parallel", "parallel"),
        name="ada_mod",
    )(c_all, w_ada, b_ada.reshape(depth, 1, d6))


def _proj_prompt_kernel(x_ref, g_ref, sh_ref, sc_ref, w_ref, wkvt_ref,
                        rq_ref, rk_ref, rv_ref, rg_ref, aq_ref, akb_ref, akt_ref, avt_ref, avtb_ref,
                        gr_ref, ga_ref, qc_ref, means_ref, *, slopes):
    i = pl.program_id(0)
    nb = means_ref.shape[0]
    tm = x_ref.shape[0]
    w = HEAD_W

    @pl.when(i == 0)
    def _():
        means_ref[...] = jnp.zeros_like(means_ref)

    hb = _modulate(x_ref[...], g_ref[...], sh_ref[...], sc_ref[...]).astype(BF16)

    def col(c, n=1):
        return _dot(hb, w_ref[:, c * w:(c + n) * w])

    rq_ref[...] = col(0).astype(BF16)
    rk_ref[...] = (col(1) * RET_DK ** -0.5).astype(BF16)
    rv_ref[...] = col(2).astype(BF16)
    rg_ref[...] = col(3)
    q = col(4) * ATT_HD ** -0.5
    aq_ref[...] = q.astype(BF16)
    k = col(5)
    akb_ref[...] = k.astype(BF16)
    gr_ref[...] = col(7, 2)
    ga_ref[...] = col(9, 2)
    kvt = _dot_nt(wkvt_ref[...], hb)
    akt_ref[...] = kvt[:w]
    avt_ref[...] = kvt[w:]
    avtb_ref[...] = kvt[w:].astype(BF16)

    blk = lax.broadcasted_iota(jnp.int32, (nb, tm), 0)
    valid = blk < i
    blk_dist = (MOBA_BLOCK * (i - blk)).astype(F32)
    xrow = lax.broadcasted_iota(jnp.int32, (LANES - nb, tm), 0)
    xlane = lax.broadcasted_iota(jnp.int32, (LANES - nb, tm), 1).astype(F32)
    for p in range(PAIRS):
        mp = means_ref[:, p * LANES:(p + 1) * LANES]
        qp = q[:, p * LANES:(p + 1) * LANES]
        for a in range(2):
            slope = slopes[2 * p + a]
            qm = jnp.where(_head_lane_mask(qp.shape, a), qp, 0.0)
            gate = jnp.where(valid, _dot_nt_hi(mp, qm), -jnp.inf)
            sel, _ = _top3_rows(gate, nb)
            ct = jnp.where(sel > 0.5, -slope * blk_dist, NEG)
            extra = jnp.where(xrow == 0, -slope * xlane, jnp.where(xrow == 1, slope, 0.0))
            qc_ref[2 * p + a] = jnp.concatenate([ct, extra], axis=0).T.astype(BF16)

    means_ref[pl.ds(i, 1), :] = jnp.mean(k, axis=0, keepdims=True)


def _proj_prompt(x, g, shift, scale, w_in_b, w_kvt_b):
    s, d = x.shape
    tm = MOBA_BLOCK
    assert s % tm == 0
    nb = s // tm
    w = HEAD_W
    n_in = w_in_b.shape[1]
    row = lambda width: pl.BlockSpec((tm, width), lambda i: (i, 0))
    colblk = pl.BlockSpec((w, tm), lambda i: (0, i))
    vec = pl.BlockSpec((1, d), lambda i: (0, 0))
    out_shape = (
        jax.ShapeDtypeStruct((s, w), BF16),
        jax.ShapeDtypeStruct((s, w), BF16),
        jax.ShapeDtypeStruct((s, w), BF16),
        jax.ShapeDtypeStruct((s, w), F32),
        jax.ShapeDtypeStruct((s, w), BF16),
        jax.ShapeDtypeStruct((nb, tm, w), BF16),
        jax.ShapeDtypeStruct((w, s), F32),
        jax.ShapeDtypeStruct((w, s), F32),
        jax.ShapeDtypeStruct((nb, w, tm), BF16),
        jax.ShapeDtypeStruct((s, d), F32),
        jax.ShapeDtypeStruct((s, d), F32),
        jax.ShapeDtypeStruct((ATT_HEADS, s, LANES), BF16),
    )
    assert nb + 2 <= LANES
    out_specs = (
        row(w), row(w), row(w), row(w), row(w),
        pl.BlockSpec((None, tm, w), lambda i: (i, 0, 0)),
        colblk, colblk,
        pl.BlockSpec((None, w, tm), lambda i: (i, 0, 0)),
        row(d), row(d),
        pl.BlockSpec((ATT_HEADS, tm, LANES), lambda i: (0, i, 0)),
    )
    return pl.pallas_call(
        functools.partial(_proj_prompt_kernel, slopes=ALIBI_SLOPES),
        grid=(nb,),
        in_specs=[row(d), vec, vec, vec,
                  pl.BlockSpec((d, n_in), lambda i: (0, 0), pipeline_mode=pl.Buffered(1)),
                  pl.BlockSpec((2 * w, d), lambda i: (0, 0), pipeline_mode=pl.Buffered(1))],
        out_specs=out_specs,
        out_shape=out_shape,
        scratch_shapes=[pltpu.VMEM((nb, w), F32)],
        compiler_params=_params("arbitrary"),
        name="proj_prompt",
    )(x, g, shift, scale, w_in_b, w_kvt_b)


def _proj_sample_kernel(x_ref, g_ref, sh_ref, sc_ref, w_ref, o_ref):
    h = _modulate(x_ref[...], g_ref[...], sh_ref[...], sc_ref[...])
    o_ref[...] = _dot_hi(h, w_ref[...])


def _proj_sample(x, g, shift, scale, w_in, layer):
    b, d = x.shape
    n_in = w_in.shape[2]
    tn = HEAD_W
    return pl.pallas_call(
        _proj_sample_kernel,
        grid=(n_in // tn,),
        in_specs=[_full((b, d)), _full((1, d)), _full((b, d)), _full((b, d)),
                  pl.BlockSpec((None, d, tn), lambda j: (layer, 0, j))],
        out_specs=pl.BlockSpec((b, tn), lambda j: (0, j)),
        out_shape=jax.ShapeDtypeStruct((b, n_in), F32),
        compiler_params=_params("parallel"),
        name="proj_sample",
    )(x, g, shift, scale, w_in)


def _retention_kernel(q_ref, k_ref, v_ref, rg_ref, decay_ref, xi_ref, zeta_ref, gc_ref, avg_ref,
                      o_ref, st_ref):
    i = pl.program_id(0)

    @pl.when(i == 0)
    def _():
        st_ref[...] = jnp.zeros_like(st_ref)

    c = q_ref.shape[0]
    row = lax.broadcasted_iota(jnp.int32, (LANES, LANES), 0)
    lane = lax.broadcasted_iota(jnp.int32, (LANES, LANES), 1)
    same_head = (row // RET_DK) == (lane // RET_DV)
    avg = avg_ref[...]
    for p in range(PAIRS):
        sl = slice(p * LANES, (p + 1) * LANES)
        q = q_ref[:, sl]
        k = k_ref[:, sl]
        v = v_ref[:, sl]
        state = st_ref[p]
        cross = _dot(q, state.astype(BF16)) * xi_ref[:, sl]
        inner = []
        for a in range(2):
            qm = jnp.where(_head_lane_mask(q.shape, a), q, jnp.zeros_like(q))
            scores = _dot_nt(qm, k) * decay_ref[2 * p + a]
            inner.append(_dot(scores.astype(BF16), v))
        o = jnp.where(_head_lane_mask((c, LANES), 0), inner[0], inner[1]) + cross
        kz = (k.astype(F32) * zeta_ref[:, sl]).astype(BF16)
        upd = lax.dot_general(kz, v, TN, preferred_element_type=F32)
        st_ref[p] = gc_ref[:, sl] * state + jnp.where(same_head, upd, 0.0)
        mu = _dot_hi(o, avg)
        dlt = o - mu
        var = _dot_hi(dlt * dlt, avg)
        hn = dlt * lax.rsqrt(var + EPS)
        o_ref[:, sl] = (_silu(rg_ref[:, sl]) * hn).astype(BF16)


def _ret_tables(c):
    h = jnp.arange(RET_HEADS, dtype=F32)
    log_gamma = jnp.log1p(-(2.0 ** (-5.0 - h)))
    idx = jnp.arange(c, dtype=F32)
    diff = idx[:, None] - idx[None, :]
    decay = jnp.where(diff >= 0, jnp.exp(log_gamma[:, None, None] * jnp.maximum(diff, 0.0)), 0.0)
    xi = jnp.exp(log_gamma[None, :] * (idx[:, None] + 1.0))
    zeta = jnp.exp(log_gamma[None, :] * (c - 1.0 - idx[:, None]))
    gc = jnp.exp(log_gamma * c)[None, :]
    rep = lambda t: jnp.repeat(t, RET_DV, axis=1)
    return decay, rep(xi), rep(zeta), rep(gc)


def _head_avg_matrix():
    r = jnp.arange(LANES)
    return jnp.where((r[:, None] // RET_DV) == (r[None, :] // RET_DV), 1.0 / RET_DV, 0.0).astype(F32)


def _retention_prompt(rq, rk, rv, rg, chunk):
    s, w = rq.shape
    assert s % chunk == 0
    decay, xi, zeta, gc = _ret_tables(chunk)
    row = pl.BlockSpec((chunk, w), lambda i: (i, 0))
    o, st = pl.pallas_call(
        _retention_kernel,
        grid=(s // chunk,),
        in_specs=[row, row, row, row,
                  _full((RET_HEADS, chunk, chunk)), _full((chunk, w)), _full((chunk, w)), _full((1, w)),
                  _full((LANES, LANES))],
        out_specs=(row, _full((PAIRS, LANES, LANES))),
        out_shape=(jax.ShapeDtypeStruct((s, w), BF16), jax.ShapeDtypeStruct((PAIRS, LANES, LANES), F32)),
        compiler_params=_params("arbitrary"),
        name="retention_prompt",
    )(rq, rk, rv, rg, decay, xi, zeta, gc, _head_avg_matrix())
    st = st.reshape(PAIRS, 2, RET_DK, 2, RET_DV)
    st = jnp.stack([st[:, 0, :, 0, :], st[:, 1, :, 1, :]], axis=1).reshape(RET_HEADS, RET_DK, RET_DV)
    return o, st


def _moba_prompt_kernel(q_ref, qc_ref, k_ref, vt_ref, ctab_ref, cown_ref, o_ref,
                        qcat_ref, sa_ref, sb_ref, m_ref, l_ref, acc_ref):
    i = pl.program_id(0)
    tq = q_ref.shape[0]
    tk = k_ref.shape[1]
    nb = k_ref.shape[0]
    heads = range(ATT_HEADS)
    hd = ATT_HD

    for p in range(PAIRS):
        qp = q_ref[:, p * LANES:(p + 1) * LANES]
        for a in range(2):
            h = 2 * p + a
            qm = jnp.where(_head_lane_mask(qp.shape, a), qp, jnp.zeros_like(qp))
            qcat_ref[h] = jnp.concatenate([qm, qc_ref[h]], axis=1)

    ones = jnp.ones((2 * SUBLANES, tk), BF16)

    def scores(n, h, s_ref):
        p = h // 2
        kcat = jnp.concatenate([k_ref[n, :, p * LANES:(p + 1) * LANES], ctab_ref[n]], axis=1)
        s_ref[h] = _dot_nt(kcat, qcat_ref[h])

    def attend(n, h, s_ref):
        s = s_ref[h]
        m = m_ref[h:h + 1, :]
        m_new = jnp.maximum(m, jnp.max(s, axis=0, keepdims=True))
        alpha = jnp.exp(m - m_new)
        pr = jnp.exp((s - m_new).astype(BF16))
        rows = slice(h * hd, (h + 1) * hd)
        pv = _dot(jnp.concatenate([vt_ref[n, rows, :], ones], axis=0), pr)
        m_ref[h:h + 1, :] = m_new
        l_ref[h:h + 1, :] = alpha * l_ref[h:h + 1, :] + pv[hd:hd + 1, :]
        acc_ref[rows, :] = alpha * acc_ref[rows, :] + pv[:hd, :]

    kk = lax.broadcasted_iota(jnp.int32, (tk, tq), 0)
    qq = lax.broadcasted_iota(jnp.int32, (tk, tq), 1)
    causal = kk <= qq
    cown = cown_ref[...]
    vt_own = vt_ref[i]
    for p in range(PAIRS):
        kcat = jnp.concatenate([k_ref[i, :, p * LANES:(p + 1) * LANES], cown], axis=1)
        for a in range(2):
            h = 2 * p + a
            s = jnp.where(causal, _dot_nt(kcat, qcat_ref[h]), NEG)
            m0 = jnp.max(s, axis=0, keepdims=True)
            pr = jnp.exp(s - m0)
            m_ref[h:h + 1, :] = m0
            l_ref[h:h + 1, :] = jnp.sum(pr, axis=0, keepdims=True)
            acc_ref[h * hd:(h + 1) * hd, :] = _dot(vt_own[h * hd:(h + 1) * hd, :], pr.astype(BF16))

    for h in heads:
        scores(0, h, sa_ref)

    @pl.loop(0, (i + 1) // 2)
    def _(t):
        nxt = jnp.minimum(2 * t + 2, nb - 1)
        for h in heads:
            scores(2 * t + 1, h, sb_ref)
            attend(2 * t, h, sa_ref)
        for h in heads:
            scores(nxt, h, sa_ref)
            attend(2 * t + 1, h, sb_ref)

    for h in heads:
        rows = slice(h * hd, (h + 1) * hd)
        acc_ref[rows, :] = acc_ref[rows, :] / l_ref[h:h + 1, :]
    o_ref[...] = acc_ref[...].T.astype(BF16)


def _moba_bias_table(nb, tk):
    lane = jnp.arange(LANES)[None, None, :]
    blk = jnp.arange(nb)[:, None, None]
    kk = jnp.arange(tk, dtype=F32)[None, :, None]
    t = jnp.where(lane == blk, 1.0, 0.0)
    t = jnp.where(lane == nb, 1.0, t)
    return jnp.where(lane == nb + 1, kk, t).astype(BF16)


def _moba_prompt(aq, qc, akb, avtb):
    s, w = aq.shape
    nb, tk, _ = akb.shape
    tq = MOBA_BLOCK
    assert nb % 2 == 0 and tk == tq
    ctab = _moba_bias_table(nb, tk)
    cown = jnp.where(jnp.arange(LANES)[None, :] < nb, 0, ctab[0])
    once = dict(pipeline_mode=pl.Buffered(1))
    return pl.pallas_call(
        _moba_prompt_kernel,
        grid=(s // tq,),
        in_specs=[
            pl.BlockSpec((tq, w), lambda i: (i, 0)),
            pl.BlockSpec((ATT_HEADS, tq, LANES), lambda i: (0, i, 0)),
            pl.BlockSpec((nb, tk, w), lambda i: (0, 0, 0), **once),
            pl.BlockSpec((nb, w, tk), lambda i: (0, 0, 0), **once),
            pl.BlockSpec((nb, tk, LANES), lambda i: (0, 0, 0), **once),
            pl.BlockSpec((tk, LANES), lambda i: (0, 0), **once),
        ],
        out_specs=pl.BlockSpec((tq, w), lambda i: (i, 0)),
        out_shape=jax.ShapeDtypeStruct((s, w), BF16),
        scratch_shapes=[
            pltpu.VMEM((ATT_HEADS, tq, 2 * LANES), BF16),
            pltpu.VMEM((ATT_HEADS, tk, tq), F32),
            pltpu.VMEM((ATT_HEADS, tk, tq), F32),
            pltpu.VMEM((ATT_HEADS, tq), F32),
            pltpu.VMEM((ATT_HEADS, tq), F32),
            pltpu.VMEM((w, tq), F32),
        ],
        compiler_params=_params("arbitrary"),
        name="moba_prompt",
    )(aq, qc, akb, avtb, ctab, cown)


def _merge_router_kernel(x_ref, ret_ref, att_ref, gr_ref, ga_ref, g1_ref, wro_ref, wao_ref, wo_ref,
                         gf_ref, sh_ref, sc_ref, wr_ref, br_ref,
                         x1_ref, h2_ref, wts_ref, wt_scr, *, exact):
    ret = _mm(ret_ref[...], wro_ref, exact)
    att = _mm(att_ref[...], wao_ref, exact)
    merged = jax.nn.sigmoid(gr_ref[...]) * ret + jax.nn.sigmoid(ga_ref[...]) * att
    x1 = x_ref[...] + g1_ref[...] * _mm(merged, wo_ref, exact)
    x1_ref[...] = x1
    h2 = _modulate(x1, gf_ref[...], sh_ref[...], sc_ref[...])
    h2_ref[...] = h2.astype(h2_ref.dtype)

    logits = _dot_nt_hi(wr_ref[...], h2) + br_ref[...]
    lg = [logits[g:g + 1, :] for g in range(N_GROUPS)]
    le = [logits[N_GROUPS + e:N_GROUPS + e + 1, :] for e in range(N_EXPERTS)]
    mg = functools.reduce(jnp.maximum, lg)
    eg = [jnp.exp(v - mg) for v in lg]
    den = functools.reduce(jnp.add, eg)
    pg = [v / den for v in eg]
    p_top = functools.reduce(jnp.maximum, pg)
    taken = jnp.zeros(p_top.shape, jnp.bool_)
    onehot = []
    for g in range(N_GROUPS):
        hit = jnp.logical_and(pg[g] == p_top, jnp.logical_not(taken))
        onehot.append(hit)
        taken = jnp.logical_or(taken, hit)
    vin = []
    for j in range(EXP_PER_GROUP):
        v = jnp.zeros_like(p_top)
        for g in range(N_GROUPS):
            v = jnp.where(onehot[g], le[g * EXP_PER_GROUP + j], v)
        vin.append(v)
    rank = []
    for j in range(EXP_PER_GROUP):
        r = jnp.zeros(p_top.shape, jnp.int32)
        for kx in range(EXP_PER_GROUP):
            if kx == j:
                continue
            ahead = (vin[kx] > vin[j]) | ((vin[kx] == vin[j]) & (kx < j))
            r = r + ahead.astype(jnp.int32)
        rank.append(r)
    v1 = functools.reduce(jnp.maximum, vin)
    v2 = jnp.full_like(v1, -jnp.inf)
    for j in range(EXP_PER_GROUP):
        v2 = jnp.where(rank[j] == 1, vin[j], v2)
    e2 = jnp.exp(v2 - v1)
    w_first = 1.0 / (1.0 + e2)
    w_second = e2 / (1.0 + e2)
    wt_scr[...] = jnp.zeros_like(wt_scr)
    for g in range(N_GROUPS):
        for j in range(EXP_PER_GROUP):
            wj = jnp.where(rank[j] == 0, w_first, jnp.where(rank[j] == 1, w_second, 0.0)) * p_top
            e = g * EXP_PER_GROUP + j
            wt_scr[e:e + 1, :] = jnp.where(onehot[g], wj, 0.0)
    wts_ref[...] = wt_scr[...].T


def _merge_router(x, ret_in, att_in, gr, ga, gate1, wro, wao, wo, g_ffn, shift2, scale2, wr_t, br_col, tm, exact):
    s, d = x.shape
    w = HEAD_W
    assert s % tm == 0
    row = lambda width: pl.BlockSpec((tm, width), lambda i: (i, 0))
    mod = lambda a: pl.BlockSpec((tm if a.shape[0] > 1 else 1, d), lambda i: (i if a.shape[0] > 1 else 0, 0))
    if exact:
        layer = exact - 1
        wspec = lambda a: pl.BlockSpec((None,) + a.shape[1:], lambda i: (layer, 0, 0))
    else:
        wspec = lambda a: _full(a.shape)
    rr = wr_t.shape[0]
    return pl.pallas_call(
        functools.partial(_merge_router_kernel, exact=bool(exact)),
        grid=(s // tm,),
        in_specs=[row(d), row(w), row(w), row(d), row(d), mod(gate1),
                  wspec(wro), wspec(wao), wspec(wo),
                  _full((1, d)), mod(shift2), mod(scale2), _full((rr, d)), _full((rr, 1))],
        out_specs=(row(d), row(d), row(LANES)),
        out_shape=(jax.ShapeDtypeStruct((s, d), F32), jax.ShapeDtypeStruct((s, d), F32 if exact else BF16),
                   jax.ShapeDtypeStruct((s, LANES), F32)),
        scratch_shapes=[pltpu.VMEM((LANES, tm), F32)],
        compiler_params=_params("parallel"),
        name="merge_router",
    )(x, ret_in, att_in, gr, ga, gate1, wro, wao, wo, g_ffn, shift2, scale2, wr_t, br_col)


def _moe_kernel(x1_ref, h2_ref, wts_ref, g2_ref, wg_ref, wu_ref, wd_ref, gfin_ref, o_ref, acc_ref,
                *, final_norm, exact):
    e = pl.program_id(1)

    @pl.when(e == 0)
    def _():
        acc_ref[...] = jnp.zeros_like(acc_ref)

    h2 = h2_ref[...]
    wts = wts_ref[...]
    lane = lax.broadcasted_iota(jnp.int32, wts.shape, 1)
    wcol = jnp.sum(jnp.where(lane == e, wts, 0.0), axis=1, keepdims=True)
    act = _silu(_mm(h2, wg_ref, exact)) * _mm(h2, wu_ref, exact) * wcol
    acc_ref[...] += _mm(act, wd_ref, exact)

    @pl.when(e == pl.num_programs(1) - 1)
    def _():
        x2 = x1_ref[...] + g2_ref[...] * acc_ref[...]
        if final_norm:
            ms = jnp.mean(x2 * x2, axis=-1, keepdims=True)
            x2 = x2 * lax.rsqrt(ms + EPS) * gfin_ref[...]
        o_ref[...] = x2


def _moe(x1, h2, wts, gate2, wg, wu, wd, g_final, tm, final_norm, exact):
    s, d = x1.shape
    ne, f = wg.shape[-3], wg.shape[-1]
    assert s % tm == 0
    row = lambda width: pl.BlockSpec((tm, width), lambda i, e: (i, 0))
    mod = lambda a: pl.BlockSpec((tm if a.shape[0] > 1 else 1, d), lambda i, e: (i if a.shape[0] > 1 else 0, 0))
    if exact:
        layer = exact - 1
        wspec = lambda r, c: pl.BlockSpec((None, None, r, c), lambda i, e: (layer, e, 0, 0))
    else:
        wspec = lambda r, c: pl.BlockSpec((None, r, c), lambda i, e: (e, 0, 0))
    return pl.pallas_call(
        functools.partial(_moe_kernel, final_norm=final_norm, exact=bool(exact)),
        grid=(s // tm, ne),
        in_specs=[row(d), row(d), row(LANES), mod(gate2),
                  wspec(d, f), wspec(d, f), wspec(f, d),
                  pl.BlockSpec((1, d), lambda i, e: (0, 0))],
        out_specs=row(d),
        out_shape=jax.ShapeDtypeStruct((s, d), F32),
        scratch_shapes=[pltpu.VMEM((tm, d), F32)],
        compiler_params=_params("parallel", "arbitrary"),
        name="moe_experts",
    )(x1, h2, wts, gate2, wg, wu, wd, g_final)


def _paged_gate_kernel(pt_ref, q_ref, *rest):
    pages = rest[:PAGES_PER_STEP]
    sel_ref, qb_ref, acc_ref = rest[PAGES_PER_STEP:]
    j = pl.program_id(1)
    rows_per_block = ATT_HEADS * SUBLANES

    @pl.when(j == 0)
    def _():
        qb_ref[...] = jnp.broadcast_to(q_ref[...] * ATT_HD ** -0.5, qb_ref.shape)

    qb = qb_ref[...]
    blocks_per_step = PAGES_PER_STEP // PAGES_PER_BLOCK
    for t in range(blocks_per_step):
        tot = pages[PAGES_PER_BLOCK * t][...]
        for u in range(1, PAGES_PER_BLOCK):
            tot = tot + pages[PAGES_PER_BLOCK * t + u][...]
        part = (tot * qb).reshape(ATT_HEADS, ATT_HD // SUBLANES, SUBLANES, PAGE_SIZE).sum(axis=1)
        base = pl.multiple_of((j * blocks_per_step + t) * rows_per_block, rows_per_block)
        acc_ref[pl.ds(base, rows_per_block), :] = part.reshape(rows_per_block, PAGE_SIZE)

    @pl.when(j == pl.num_programs(1) - 1)
    def _():
        n_rows = acc_ref.shape[0]
        nb = n_rows // rows_per_block
        g = jnp.sum(acc_ref[...], axis=1, keepdims=True)
        r = lax.broadcasted_iota(jnp.int32, (n_rows, LANES), 0)
        lane = lax.broadcasted_iota(jnp.int32, (n_rows, LANES), 1)
        spread = jnp.where(lane == (r // SUBLANES) % ATT_HEADS, g, 0.0)
        gate = spread.reshape(nb, rows_per_block, LANES).sum(axis=1) * (1.0 / MOBA_BLOCK)
        _, idxs = _top3_rows(gate, nb)
        out = jnp.zeros(sel_ref.shape, jnp.int32)
        rr = lax.broadcasted_iota(jnp.int32, sel_ref.shape, 0)
        for t, idx in enumerate(idxs):
            out = jnp.where(rr == t, idx, out)
        sel_ref[...] = out


def _paged_gate(cache_kt, layer, page_table_flat, q_col, n_pages):
    b = q_col.shape[0]
    assert n_pages % PAGES_PER_STEP == 0 and PAGES_PER_STEP % PAGES_PER_BLOCK == 0
    nb = n_pages // PAGES_PER_BLOCK
    assert nb >= MOBA_TOPK

    def page_spec(g):
        return pl.BlockSpec((None, None, HEAD_W, PAGE_SIZE),
                            lambda bi, j, pt: (layer, pt[bi * n_pages + j * PAGES_PER_STEP + g], 0, 0))

    grid_spec = pltpu.PrefetchScalarGridSpec(
        num_scalar_prefetch=1,
        grid=(b, n_pages // PAGES_PER_STEP),
        in_specs=[pl.BlockSpec((None, HEAD_W, 1), lambda bi, j, pt: (bi, 0, 0))]
                 + [page_spec(g) for g in range(PAGES_PER_STEP)],
        out_specs=pl.BlockSpec((None, SUBLANES, LANES), lambda bi, j, pt: (bi, 0, 0)),
        scratch_shapes=[pltpu.VMEM((HEAD_W, PAGE_SIZE), F32),
                        pltpu.VMEM((nb * ATT_HEADS * SUBLANES, PAGE_SIZE), F32)],
    )
    return pl.pallas_call(
        _paged_gate_kernel,
        grid_spec=grid_spec,
        out_shape=jax.ShapeDtypeStruct((b, SUBLANES, LANES), jnp.int32),
        compiler_params=_params("parallel", "arbitrary"),
        name="paged_gate",
    )(page_table_flat, q_col, *([cache_kt] * PAGES_PER_STEP))


def _moba_sample_kernel(pt_ref, sel_ref, slopes_ref, q_ref, kn_ref, vn_ref, *rest, past_len):
    npg = MOBA_TOPK * PAGES_PER_BLOCK
    kp = rest[:npg]
    vp = rest[npg:2 * npg]
    o_ref = rest[2 * npg]
    b = pl.program_id(0)
    h = pl.program_id(1)
    slope = slopes_ref[h]
    lane = lax.broadcasted_iota(jnp.int32, (1, PAGE_SIZE), 1)
    q = q_ref[...] * ATT_HD ** -0.5
    scores = []
    for s in range(MOBA_TOPK):
        blk = sel_ref[(b * ATT_HEADS + h) * MOBA_TOPK + s]
        for t in range(PAGES_PER_BLOCK):
            kpos = blk * MOBA_BLOCK + t * PAGE_SIZE + lane
            dist = (past_len - kpos).astype(F32)
            sc = jnp.sum(kp[s * PAGES_PER_BLOCK + t][...] * q, axis=0, keepdims=True)
            scores.append(sc - slope * dist)
    s_own = jnp.sum(q * kn_ref[...], axis=0, keepdims=True)
    m = functools.reduce(jnp.maximum, [jnp.max(sc, axis=1, keepdims=True) for sc in scores] + [s_own])
    p_own = jnp.exp(s_own - m)
    l = p_own
    acc = p_own * vn_ref[...]
    for idx, sc in enumerate(scores):
        pr = jnp.exp(sc - m)
        l = l + jnp.sum(pr, axis=1, keepdims=True)
        acc = acc + jnp.sum(vp[idx][...] * pr, axis=1, keepdims=True)
    o_ref[...] = acc / l


def _moba_sample(cache_kt, cache_vt, layer, page_table_flat, sel_flat, q_col, k_col, v_col, n_pages):
    b = q_col.shape[0]
    npg = MOBA_TOPK * PAGES_PER_BLOCK
    h = jnp.arange(1, ATT_HEADS + 1, dtype=F32)
    slopes = 2.0 ** (-8.0 * h / ATT_HEADS)

    def page_spec(g):
        s, t = divmod(g, PAGES_PER_BLOCK)

        def imap(bi, hi, pt, sel):
            blk = sel[(bi * ATT_HEADS + hi) * MOBA_TOPK + s]
            return (layer, pt[bi * n_pages + blk * PAGES_PER_BLOCK + t], hi, 0)

        return pl.BlockSpec((None, None, ATT_HD, PAGE_SIZE), imap)

    vec = pl.BlockSpec((None, ATT_HD, 1), lambda bi, hi, pt, sel: (bi, hi, 0))
    grid_spec = pltpu.PrefetchScalarGridSpec(
        num_scalar_prefetch=2,
        grid=(b, ATT_HEADS),
        in_specs=[pl.BlockSpec(memory_space=pltpu.SMEM), vec, vec, vec]
                 + [page_spec(g) for g in range(npg)] * 2,
        out_specs=vec,
    )
    return pl.pallas_call(
        functools.partial(_moba_sample_kernel, past_len=n_pages * PAGE_SIZE),
        grid_spec=grid_spec,
        out_shape=jax.ShapeDtypeStruct((b, HEAD_W, 1), F32),
        compiler_params=_params("parallel", "arbitrary"),
        name="moba_sample",
    )(page_table_flat, sel_flat, slopes, q_col, k_col, v_col, *([cache_kt] * npg), *([cache_vt] * npg))


def _ret_sample_kernel(st_ref, q_ref, k_ref, v_ref, rg_ref, gamma_ref, ns_ref, o_ref):
    new = gamma_ref[...] * st_ref[...] + (k_ref[...] * RET_DK ** -0.5) * v_ref[...]
    ns_ref[...] = new
    o = jnp.sum(q_ref[...] * new, axis=1, keepdims=True)
    mu = jnp.mean(o, axis=-1, keepdims=True)
    dlt = o - mu
    var = jnp.mean(dlt * dlt, axis=-1, keepdims=True)
    o_ref[...] = _silu(rg_ref[...]) * (dlt * lax.rsqrt(var + EPS))


def _ret_sample(state, layer, rq, rk, rv, rg):
    b = state.shape[1]
    hh = jnp.arange(RET_HEADS, dtype=F32)
    gamma = jnp.exp(jnp.log1p(-(2.0 ** (-5.0 - hh)))).reshape(RET_HEADS, 1, 1)
    st_in = pl.BlockSpec((None, None, RET_HEADS, RET_DK, RET_DV), lambda i: (layer, i, 0, 0, 0))
    st_out = pl.BlockSpec((None, RET_HEADS, RET_DK, RET_DV), lambda i: (i, 0, 0, 0))
    col = pl.BlockSpec((None, RET_HEADS, RET_DK, 1), lambda i: (i, 0, 0, 0))
    rw = pl.BlockSpec((None, RET_HEADS, 1, RET_DV), lambda i: (i, 0, 0, 0))
    return pl.pallas_call(
        _ret_sample_kernel,
        grid=(b,),
        in_specs=[st_in, col, col, rw, rw, _full((RET_HEADS, 1, 1))],
        out_specs=(st_out, rw),
        out_shape=(jax.ShapeDtypeStruct(state.shape[1:], F32),
                   jax.ShapeDtypeStruct((b, RET_HEADS, 1, RET_DV), F32)),
        compiler_params=_params("parallel"),
        name="retention_sample",
    )(state, rq, rk, rv, rg, gamma)


def kernel(x_prompt, x_sample, cache_k, cache_v, state_ret, page_table, c_prompt, c_sample, w_ada, b_ada, g_mix, w_in, w_ret_o, w_att_o, w_out, g_ffn, w_rg, b_rg, w_re, b_re, w_gate_e, w_up_e, w_down_e, g_final):
    bp, s, d = x_prompt.shape
    bs, ds, _ = x_sample.shape
    assert bp == 1 and ds == 1
    depth = w_ada.shape[0]
    n_pool = cache_k.shape[1]
    n_pages = page_table.shape[1]
    w = HEAD_W

    n_rows = bp + bs
    pad = (-n_rows) % SUBLANES
    c_all = jnp.concatenate([c_prompt, c_sample, jnp.zeros((pad, d), F32)], axis=0)
    mods = _ada(c_all, w_ada, b_ada)

    cache_kt = cache_k.transpose(0, 1, 3, 4, 2).reshape(depth, n_pool, w, PAGE_SIZE)
    cache_vt = cache_v.transpose(0, 1, 3, 4, 2).reshape(depth, n_pool, w, PAGE_SIZE)
    pt_flat = page_table.reshape(-1)

    n_log = N_GROUPS + N_EXPERTS
    rr = 32
    g_fin = g_final.reshape(1, d)

    xp = x_prompt.reshape(s, d)
    xs = x_sample.reshape(bs, d)
    kp_l, vp_l, rp_l, ks_l, vs_l, rs_l = [], [], [], [], [], []
    for l in range(depth):
        last = l == depth - 1
        mp = [mods[l, 0:bp, j * d:(j + 1) * d] for j in range(6)]
        ms = [mods[l, bp:bp + bs, j * d:(j + 1) * d] for j in range(6)]
        w_in_b = w_in[l].astype(BF16)
        w_kvt_b = w_in[l][:, 5 * w:7 * w].T.astype(BF16)
        wro_b = w_ret_o[l].astype(BF16)
        wao_b = w_att_o[l].astype(BF16)
        wo_b = w_out[l].astype(BF16)
        wg_b = w_gate_e[l].astype(BF16)
        wu_b = w_up_e[l].astype(BF16)
        wd_b = w_down_e[l].astype(BF16)
        wr_t = jnp.concatenate([w_rg[l].T, w_re[l].T, jnp.zeros((rr - n_log, d), F32)], axis=0)
        br_col = jnp.concatenate([b_rg[l], b_re[l], jnp.zeros((rr - n_log,), F32)]).reshape(rr, 1)
        gm = g_mix[l].reshape(1, d)
        gf = g_ffn[l].reshape(1, d)

        rq, rk, rv, rg, aq, akb, akt, avt, avtb, gr, ga, qc = _proj_prompt(xp, gm, mp[0], mp[1], w_in_b, w_kvt_b)
        ret_in, st = _retention_prompt(rq, rk, rv, rg, chunk=MOBA_BLOCK)
        att_in = _moba_prompt(aq, qc, akb, avtb)
        x1, h2, wts = _merge_router(xp, ret_in, att_in, gr, ga, mp[2], wro_b, wao_b, wo_b, gf, mp[3], mp[4],
                                    wr_t, br_col, tm=512, exact=0)
        xp = _moe(x1, h2, wts, mp[5], wg_b, wu_b, wd_b, g_fin, tm=1024, final_norm=last, exact=0)
        kp_l.append(akt)
        vp_l.append(avt)
        rp_l.append(st.reshape(bp, RET_HEADS, RET_DK, RET_DV))

        proj = _proj_sample(xs, gm, ms[0], ms[1], w_in, l)
        cut = lambda j, n=1: proj[:, j * w:(j + n) * w]
        rq_s, rk_s, rv_s, rg_s, aq_s, ak_s, av_s = (cut(j) for j in range(7))
        gr_s, ga_s = cut(7, 2), cut(9, 2)
        new_state, ret_s = _ret_sample(
            state_ret, l,
            rq_s.reshape(bs, RET_HEADS, RET_DK, 1), rk_s.reshape(bs, RET_HEADS, RET_DK, 1),
            rv_s.reshape(bs, RET_HEADS, 1, RET_DV), rg_s.reshape(bs, RET_HEADS, 1, RET_DV))
        q_col = aq_s.reshape(bs, w, 1)
        sel_s = _paged_gate(cache_kt, l, pt_flat, q_col, n_pages)
        sel_flat = sel_s[:, :MOBA_TOPK, :ATT_HEADS].transpose(0, 2, 1).reshape(-1)
        att_s = _moba_sample(cache_kt, cache_vt, l, pt_flat, sel_flat, q_col, ak_s.reshape(bs, w, 1),
                             av_s.reshape(bs, w, 1), n_pages)
        x1s, h2s, wts_s = _merge_router(xs, ret_s.reshape(bs, w), att_s.reshape(bs, w), gr_s, ga_s, ms[2],
                                        w_ret_o, w_att_o, w_out, gf, ms[3], ms[4], wr_t, br_col,
                                        tm=bs, exact=l + 1)
        xs = _moe(x1s, h2s, wts_s, ms[5], w_gate_e, w_up_e, w_down_e, g_fin, tm=bs, final_norm=last,
                  exact=l + 1)
        ks_l.append(ak_s.reshape(bs, 1, ATT_HEADS, ATT_HD))
        vs_l.append(av_s.reshape(bs, 1, ATT_HEADS, ATT_HD))
        rs_l.append(new_state)

    def kv_out(parts):
        return jnp.stack(parts).reshape(depth, bp, ATT_HEADS, ATT_HD, s).transpose(0, 1, 4, 2, 3)

    return (xp.reshape(bp, s, d), xs.reshape(bs, 1, d), kv_out(kp_l), kv_out(vp_l), jnp.stack(rp_l),
            jnp.stack(ks_l), jnp.stack(vs_l), jnp.stack(rs_l))
```

```python
import functools

import jax
import jax.numpy as jnp
from jax import lax
from jax.experimental import pallas as pl
from jax.experimental.pallas import tpu as pltpu

EPS = 1e-6
RET_HEADS = 8
RET_DK = 64
RET_DV = 64
ATT_HEADS = 8
ATT_HD = 64
HEAD_W = RET_HEADS * RET_DK
MOBA_BLOCK = 256
MOBA_TOPK = 3
PAGE_SIZE = 128
PAGES_PER_BLOCK = MOBA_BLOCK // PAGE_SIZE
N_GROUPS = 4
EXP_PER_GROUP = 4
N_EXPERTS = N_GROUPS * EXP_PER_GROUP
SUBLANES = 8
LANES = 128
PAIRS = HEAD_W // LANES
NEG = -1e30
ALIBI_SLOPES = tuple(2.0 ** (-8.0 * (j + 1) / ATT_HEADS) for j in range(ATT_HEADS))
PAGES_PER_STEP = 16
VMEM_LIMIT = 56 * 1024 * 1024

F32 = jnp.float32
BF16 = jnp.bfloat16
HI = lax.Precision.HIGHEST
NT = (((1,), (1,)), ((), ()))
TN = (((0,), (0,)), ((), ()))


def _dot(a, b):
    return jnp.dot(a, b, preferred_element_type=F32)


def _dot_nt(a, b):
    return lax.dot_general(a, b, NT, preferred_element_type=F32)


def _dot_hi(a, b):
    return jnp.dot(a, b, preferred_element_type=F32, precision=HI)


def _dot_nt_hi(a, b):
    return lax.dot_general(a, b, NT, preferred_element_type=F32, precision=HI)


def _mm(a, w_ref, exact):
    if exact:
        return _dot_hi(a.astype(F32), w_ref[...])
    return _dot(a.astype(BF16), w_ref[...].astype(BF16))


def _dot_split(a, b_bf16):
    hi = a.astype(BF16)
    lo = (a - hi.astype(F32)).astype(BF16)
    return _dot(hi, b_bf16) + _dot(lo, b_bf16)


def _params(*sem):
    return pltpu.CompilerParams(dimension_semantics=sem, vmem_limit_bytes=VMEM_LIMIT)


def _full(shape):
    n = len(shape)
    return pl.BlockSpec(shape, lambda *_: (0,) * n)


def _modulate(x, g, shift, scale):
    ms = jnp.mean(x * x, axis=-1, keepdims=True)
    y = x * lax.rsqrt(ms + EPS) * g
    return y * (1.0 + scale) + shift


def _silu(x):
    return x * jax.nn.sigmoid(x)


def _head_lane_mask(shape, a):
    lane = lax.broadcasted_iota(jnp.int32, shape, len(shape) - 1)
    return (lane // RET_DK) == a


def _top3_rows(g, n_rows):
    row = lax.broadcasted_iota(jnp.int32, g.shape, 0)
    sel = jnp.zeros(g.shape, F32)
    idxs = []
    for _ in range(MOBA_TOPK):
        cmax = jnp.max(g, axis=0, keepdims=True)
        idx = jnp.min(jnp.where(g == cmax, row, n_rows), axis=0, keepdims=True)
        pick = jnp.logical_and(row == idx, cmax > -jnp.inf)
        sel = jnp.where(pick, 1.0, sel)
        g = jnp.where(pick, -jnp.inf, g)
        idxs.append(idx)
    return sel, idxs


def _ada_kernel(c_ref, w_ref, b_ref, o_ref):
    c = c_ref[...]
    o_ref[...] = _dot_hi(_silu(c), w_ref[...]) + b_ref[...]


def _ada(c_all, w_ada, b_ada):
    depth, d, d6 = w_ada.shape
    r = c_all.shape[0]
    return pl.pallas_call(
        _ada_kernel,
        grid=(depth, d6 // d),
        in_specs=[
            pl.BlockSpec((r, d), lambda l, j: (0, 0)),
            pl.BlockSpec((None, d, d), lambda l, j: (l, 0, j)),
            pl.BlockSpec((None, 1, d), lambda l, j: (l, 0, j)),
        ],
        out_specs=pl.BlockSpec((None, r, d), lambda l, j: (l, 0, j)),
        out_shape=jax.ShapeDtypeStruct((depth, r, d6), F32),
        compiler_params=_params("parallel", "parallel"),
        name="ada_mod",
    )(c_all, w_ada, b_ada.reshape(depth, 1, d6))


def _proj_prompt_kernel(x_ref, g_ref, sh_ref, sc_ref, w_ref, wkvt_ref,
                        rq_ref, rk_ref, rv_ref, rg_ref, aq_ref, akb_ref, akt_ref, avt_ref, avtb_ref,
                        gr_ref, ga_ref, qc_ref, means_ref, *, slopes):
    i = pl.program_id(0)
    nb = means_ref.shape[0]
    tm = x_ref.shape[0]
    w = HEAD_W

    @pl.when(i == 0)
    def _():
        means_ref[...] = jnp.zeros_like(means_ref)

    hb = _modulate(x_ref[...], g_ref[...], sh_ref[...], sc_ref[...]).astype(BF16)

    def col(c, n=1):
        return _dot(hb, w_ref[:, c * w:(c + n) * w])

    rq_ref[...] = col(0).astype(BF16)
    rk_ref[...] = (col(1) * RET_DK ** -0.5).astype(BF16)
    rv_ref[...] = col(2).astype(BF16)
    rg_ref[...] = col(3)
    q = col(4) * ATT_HD ** -0.5
    aq_ref[...] = q.astype(BF16)
    k = col(5)
    akb_ref[...] = k.astype(BF16)
    gr_ref[...] = col(7, 2)
    ga_ref[...] = col(9, 2)
    kvt = _dot_nt(wkvt_ref[...], hb)
    akt_ref[...] = kvt[:w]
    avt_ref[...] = kvt[w:]
    avtb_ref[...] = kvt[w:].astype(BF16)

    blk = lax.broadcasted_iota(jnp.int32, (nb, tm), 0)
    valid = blk < i
    blk_dist = (MOBA_BLOCK * (i - blk)).astype(F32)
    xrow = lax.broadcasted_iota(jnp.int32, (LANES - nb, tm), 0)
    xlane = lax.broadcasted_iota(jnp.int32, (LANES - nb, tm), 1).astype(F32)
    for p in range(PAIRS):
        mp = means_ref[:, p * LANES:(p + 1) * LANES]
        qp = q[:, p * LANES:(p + 1) * LANES]
        for a in range(2):
            slope = slopes[2 * p + a]
            qm = jnp.where(_head_lane_mask(qp.shape, a), qp, 0.0)
            gate = jnp.where(valid, _dot_nt_hi(mp, qm), -jnp.inf)
            sel, _ = _top3_rows(gate, nb)
            ct = jnp.where(sel > 0.5, -slope * blk_dist, NEG)
            extra = jnp.where(xrow == 0, -slope * xlane, jnp.where(xrow == 1, slope, 0.0))
            qc_ref[2 * p + a] = jnp.concatenate([ct, extra], axis=0).T.astype(BF16)

    means_ref[pl.ds(i, 1), :] = jnp.mean(k, axis=0, keepdims=True)


def _proj_prompt(x, g, shift, scale, w_in_b, w_kvt_b):
    s, d = x.shape
    tm = MOBA_BLOCK
    assert s % tm == 0
    nb = s // tm
    w = HEAD_W
    n_in = w_in_b.shape[1]
    row = lambda width: pl.BlockSpec((tm, width), lambda i: (i, 0))
    colblk = pl.BlockSpec((w, tm), lambda i: (0, i))
    vec = pl.BlockSpec((1, d), lambda i: (0, 0))
    out_shape = (
        jax.ShapeDtypeStruct((s, w), BF16),
        jax.ShapeDtypeStruct((s, w), BF16),
        jax.ShapeDtypeStruct((s, w), BF16),
        jax.ShapeDtypeStruct((s, w), F32),
        jax.ShapeDtypeStruct((s, w), BF16),
        jax.ShapeDtypeStruct((nb, tm, w), BF16),
        jax.ShapeDtypeStruct((w, s), F32),
        jax.ShapeDtypeStruct((w, s), F32),
        jax.ShapeDtypeStruct((nb, w, tm), BF16),
        jax.ShapeDtypeStruct((s, d), F32),
        jax.ShapeDtypeStruct((s, d), F32),
        jax.ShapeDtypeStruct((ATT_HEADS, s, LANES), BF16),
    )
    assert nb + 2 <= LANES
    out_specs = (
        row(w), row(w), row(w), row(w), row(w),
        pl.BlockSpec((None, tm, w), lambda i: (i, 0, 0)),
        colblk, colblk,
        pl.BlockSpec((None, w, tm), lambda i: (i, 0, 0)),
        row(d), row(d),
        pl.BlockSpec((ATT_HEADS, tm, LANES), lambda i: (0, i, 0)),
    )
    return pl.pallas_call(
        functools.partial(_proj_prompt_kernel, slopes=ALIBI_SLOPES),
        grid=(nb,),
        in_specs=[row(d), vec, vec, vec,
                  pl.BlockSpec((d, n_in), lambda i: (0, 0), pipeline_mode=pl.Buffered(1)),
                  pl.BlockSpec((2 * w, d), lambda i: (0, 0), pipeline_mode=pl.Buffered(1))],
        out_specs=out_specs,
        out_shape=out_shape,
        scratch_shapes=[pltpu.VMEM((nb, w), F32)],
        compiler_params=_params("arbitrary"),
        name="proj_prompt",
    )(x, g, shift, scale, w_in_b, w_kvt_b)


def _proj_sample_kernel(x_ref, g_ref, sh_ref, sc_ref, w_ref, o_ref):
    h = _modulate(x_ref[...], g_ref[...], sh_ref[...], sc_ref[...])
    o_ref[...] = _dot_hi(h, w_ref[...])


def _proj_sample(x, g, shift, scale, w_in, layer):
    b, d = x.shape
    n_in = w_in.shape[2]
    tn = HEAD_W
    return pl.pallas_call(
        _proj_sample_kernel,
        grid=(n_in // tn,),
        in_specs=[_full((b, d)), _full((1, d)), _full((b, d)), _full((b, d)),
                  pl.BlockSpec((None, d, tn), lambda j: (layer, 0, j))],
        out_specs=pl.BlockSpec((b, tn), lambda j: (0, j)),
        out_shape=jax.ShapeDtypeStruct((b, n_in), F32),
        compiler_params=_params("parallel"),
        name="proj_sample",
    )(x, g, shift, scale, w_in)


def _retention_kernel(q_ref, k_ref, v_ref, rg_ref, decay_ref, xi_ref, zeta_ref, gc_ref, avg_ref,
                      o_ref, st_ref):
    i = pl.program_id(0)

    @pl.when(i == 0)
    def _():
        st_ref[...] = jnp.zeros_like(st_ref)

    c = q_ref.shape[0]
    row = lax.broadcasted_iota(jnp.int32, (LANES, LANES), 0)
    lane = lax.broadcasted_iota(jnp.int32, (LANES, LANES), 1)
    same_head = (row // RET_DK) == (lane // RET_DV)
    avg = avg_ref[...]
    for p in range(PAIRS):
        sl = slice(p * LANES, (p + 1) * LANES)
        q = q_ref[:, sl]
        k = k_ref[:, sl]
        v = v_ref[:, sl]
        state = st_ref[p]
        cross = _dot(q, state.astype(BF16)) * xi_ref[:, sl]
        inner = []
        for a in range(2):
            qm = jnp.where(_head_lane_mask(q.shape, a), q, jnp.zeros_like(q))
            scores = _dot_nt(qm, k) * decay_ref[2 * p + a]
            inner.append(_dot(scores.astype(BF16), v))
        o = jnp.where(_head_lane_mask((c, LANES), 0), inner[0], inner[1]) + cross
        kz = (k.astype(F32) * zeta_ref[:, sl]).astype(BF16)
        upd = lax.dot_general(kz, v, TN, preferred_element_type=F32)
        st_ref[p] = gc_ref[:, sl] * state + jnp.where(same_head, upd, 0.0)
        mu = _dot_split(o, avg)
        dlt = o - mu
        var = _dot_split(dlt * dlt, avg)
        hn = dlt * lax.rsqrt(var + EPS)
        o_ref[:, sl] = (_silu(rg_ref[:, sl]) * hn).astype(BF16)


def _ret_tables(c):
    h = jnp.arange(RET_HEADS, dtype=F32)
    log_gamma = jnp.log1p(-(2.0 ** (-5.0 - h)))
    idx = jnp.arange(c, dtype=F32)
    diff = idx[:, None] - idx[None, :]
    decay = jnp.where(diff >= 0, jnp.exp(log_gamma[:, None, None] * jnp.maximum(diff, 0.0)), 0.0)
    xi = jnp.exp(log_gamma[None, :] * (idx[:, None] + 1.0))
    zeta = jnp.exp(log_gamma[None, :] * (c - 1.0 - idx[:, None]))
    gc = jnp.exp(log_gamma * c)[None, :]
    rep = lambda t: jnp.repeat(t, RET_DV, axis=1)
    return decay, rep(xi), rep(zeta), rep(gc)


def _head_avg_matrix():
    r = jnp.arange(LANES)
    return jnp.where((r[:, None] // RET_DV) == (r[None, :] // RET_DV), 1.0 / RET_DV, 0.0).astype(BF16)


def _retention_prompt(rq, rk, rv, rg, chunk):
    s, w = rq.shape
    assert s % chunk == 0
    decay, xi, zeta, gc = _ret_tables(chunk)
    row = pl.BlockSpec((chunk, w), lambda i: (i, 0))
    o, st = pl.pallas_call(
        _retention_kernel,
        grid=(s // chunk,),
        in_specs=[row, row, row, row,
                  _full((RET_HEADS, chunk, chunk)), _full((chunk, w)), _full((chunk, w)), _full((1, w)),
                  _full((LANES, LANES))],
        out_specs=(row, _full((PAIRS, LANES, LANES))),
        out_shape=(jax.ShapeDtypeStruct((s, w), BF16), jax.ShapeDtypeStruct((PAIRS, LANES, LANES), F32)),
        compiler_params=_params("arbitrary"),
        name="retention_prompt",
    )(rq, rk, rv, rg, decay, xi, zeta, gc, _head_avg_matrix())
    st = st.reshape(PAIRS, 2, RET_DK, 2, RET_DV)
    st = jnp.stack([st[:, 0, :, 0, :], st[:, 1, :, 1, :]], axis=1).reshape(RET_HEADS, RET_DK, RET_DV)
    return o, st


def _moba_prompt_kernel(q_ref, qc_ref, k_ref, vt_ref, ctab_ref, cown_ref, o_ref,
                        qcat_ref, sa_ref, sb_ref, ma_ref, mb_ref, m_ref, l_ref, acc_ref):
    i = pl.program_id(0)
    tq = q_ref.shape[0]
    tk = k_ref.shape[1]
    nb = k_ref.shape[0]
    heads = range(ATT_HEADS)
    hd = ATT_HD

    for p in range(PAIRS):
        qp = q_ref[:, p * LANES:(p + 1) * LANES]
        for a in range(2):
            h = 2 * p + a
            qm = jnp.where(_head_lane_mask(qp.shape, a), qp, jnp.zeros_like(qp))
            qcat_ref[h] = jnp.concatenate([qm, qc_ref[h]], axis=1)

    ones = jnp.ones((2 * SUBLANES, tk), BF16)

    def scores(n, h, s_ref, mx_ref):
        p = h // 2
        kcat = jnp.concatenate([k_ref[n, :, p * LANES:(p + 1) * LANES], ctab_ref[n]], axis=1)
        s = _dot_nt(kcat, qcat_ref[h])
        s_ref[h] = s
        mx_ref[h:h + 1, :] = jnp.max(s, axis=0, keepdims=True)

    def attend(n, h, s_ref, mx_ref):
        s = s_ref[h]
        m = m_ref[h:h + 1, :]
        m_new = jnp.maximum(m, mx_ref[h:h + 1, :])
        alpha = jnp.exp(m - m_new)
        pr = jnp.exp((s - m_new).astype(BF16))
        rows = slice(h * hd, (h + 1) * hd)
        pv = _dot(jnp.concatenate([vt_ref[n, rows, :], ones], axis=0), pr)
        m_ref[h:h + 1, :] = m_new
        l_ref[h:h + 1, :] = alpha * l_ref[h:h + 1, :] + pv[hd:hd + 1, :]
        acc_ref[rows, :] = alpha * acc_ref[rows, :] + pv[:hd, :]

    kk = lax.broadcasted_iota(jnp.int32, (tk, tq), 0)
    qq = lax.broadcasted_iota(jnp.int32, (tk, tq), 1)
    causal = kk <= qq
    cown = cown_ref[...]
    vt_own = vt_ref[i]
    for p in range(PAIRS):
        kcat = jnp.concatenate([k_ref[i, :, p * LANES:(p + 1) * LANES], cown], axis=1)
        for a in range(2):
            h = 2 * p + a
            s = jnp.where(causal, _dot_nt(kcat, qcat_ref[h]), NEG)
            m0 = jnp.max(s, axis=0, keepdims=True)
            pr = jnp.exp(s - m0)
            m_ref[h:h + 1, :] = m0
            l_ref[h:h + 1, :] = jnp.sum(pr, axis=0, keepdims=True)
            acc_ref[h * hd:(h + 1) * hd, :] = _dot(vt_own[h * hd:(h + 1) * hd, :], pr.astype(BF16))

    for h in heads:
        scores(0, h, sa_ref, ma_ref)

    @pl.loop(0, (i + 1) // 2)
    def _(t):
        nxt = jnp.minimum(2 * t + 2, nb - 1)
        for h in heads:
            scores(2 * t + 1, h, sb_ref, mb_ref)
            attend(2 * t, h, sa_ref, ma_ref)
        for h in heads:
            scores(nxt, h, sa_ref, ma_ref)
            attend(2 * t + 1, h, sb_ref, mb_ref)

    for h in heads:
        rows = slice(h * hd, (h + 1) * hd)
        acc_ref[rows, :] = acc_ref[rows, :] / l_ref[h:h + 1, :]
    o_ref[...] = acc_ref[...].T.astype(BF16)


def _moba_bias_table(nb, tk):
    lane = jnp.arange(LANES)[None, None, :]
    blk = jnp.arange(nb)[:, None, None]
    kk = jnp.arange(tk, dtype=F32)[None, :, None]
    t = jnp.where(lane == blk, 1.0, 0.0)
    t = jnp.where(lane == nb, 1.0, t)
    return jnp.where(lane == nb + 1, kk, t).astype(BF16)


def _moba_prompt(aq, qc, akb, avtb):
    s, w = aq.shape
    nb, tk, _ = akb.shape
    tq = MOBA_BLOCK
    assert nb % 2 == 0 and tk == tq
    ctab = _moba_bias_table(nb, tk)
    cown = jnp.where(jnp.arange(LANES)[None, :] < nb, 0, ctab[0])
    once = dict(pipeline_mode=pl.Buffered(1))
    return pl.pallas_call(
        _moba_prompt_kernel,
        grid=(s // tq,),
        in_specs=[
            pl.BlockSpec((tq, w), lambda i: (i, 0)),
            pl.BlockSpec((ATT_HEADS, tq, LANES), lambda i: (0, i, 0)),
            pl.BlockSpec((nb, tk, w), lambda i: (0, 0, 0), **once),
            pl.BlockSpec((nb, w, tk), lambda i: (0, 0, 0), **once),
            pl.BlockSpec((nb, tk, LANES), lambda i: (0, 0, 0), **once),
            pl.BlockSpec((tk, LANES), lambda i: (0, 0), **once),
        ],
        out_specs=pl.BlockSpec((tq, w), lambda i: (i, 0)),
        out_shape=jax.ShapeDtypeStruct((s, w), BF16),
        scratch_shapes=[
            pltpu.VMEM((ATT_HEADS, tq, 2 * LANES), BF16),
            pltpu.VMEM((ATT_HEADS, tk, tq), F32),
            pltpu.VMEM((ATT_HEADS, tk, tq), F32),
            pltpu.VMEM((ATT_HEADS, tq), F32),
            pltpu.VMEM((ATT_HEADS, tq), F32),
            pltpu.VMEM((ATT_HEADS, tq), F32),
            pltpu.VMEM((ATT_HEADS, tq), F32),
            pltpu.VMEM((w, tq), F32),
        ],
        compiler_params=_params("arbitrary"),
        name="moba_prompt",
    )(aq, qc, akb, avtb, ctab, cown)


def _merge_router_kernel(x_ref, ret_ref, att_ref, gr_ref, ga_ref, g1_ref, wro_ref, wao_ref, wo_ref,
                         gf_ref, sh_ref, sc_ref, wr_ref, br_ref,
                         x1_ref, h2_ref, wts_ref, wt_scr, *, exact):
    ret = _mm(ret_ref[...], wro_ref, exact)
    att = _mm(att_ref[...], wao_ref, exact)
    merged = jax.nn.sigmoid(gr_ref[...]) * ret + jax.nn.sigmoid(ga_ref[...]) * att
    x1 = x_ref[...] + g1_ref[...] * _mm(merged, wo_ref, exact)
    x1_ref[...] = x1
    h2 = _modulate(x1, gf_ref[...], sh_ref[...], sc_ref[...])
    h2_ref[...] = h2.astype(h2_ref.dtype)

    logits = _dot_nt_hi(wr_ref[...], h2) + br_ref[...]
    lg = [logits[g:g + 1, :] for g in range(N_GROUPS)]
    le = [logits[N_GROUPS + e:N_GROUPS + e + 1, :] for e in range(N_EXPERTS)]
    mg = functools.reduce(jnp.maximum, lg)
    eg = [jnp.exp(v - mg) for v in lg]
    den = functools.reduce(jnp.add, eg)
    pg = [v / den for v in eg]
    p_top = functools.reduce(jnp.maximum, pg)
    taken = jnp.zeros(p_top.shape, jnp.bool_)
    onehot = []
    for g in range(N_GROUPS):
        hit = jnp.logical_and(pg[g] == p_top, jnp.logical_not(taken))
        onehot.append(hit)
        taken = jnp.logical_or(taken, hit)
    vin = []
    for j in range(EXP_PER_GROUP):
        v = jnp.zeros_like(p_top)
        for g in range(N_GROUPS):
            v = jnp.where(onehot[g], le[g * EXP_PER_GROUP + j], v)
        vin.append(v)
    rank = []
    for j in range(EXP_PER_GROUP):
        r = jnp.zeros(p_top.shape, jnp.int32)
        for kx in range(EXP_PER_GROUP):
            if kx == j:
                continue
            ahead = (vin[kx] > vin[j]) | ((vin[kx] == vin[j]) & (kx < j))
            r = r + ahead.astype(jnp.int32)
        rank.append(r)
    v1 = functools.reduce(jnp.maximum, vin)
    v2 = jnp.full_like(v1, -jnp.inf)
    for j in range(EXP_PER_GROUP):
        v2 = jnp.where(rank[j] == 1, vin[j], v2)
    e2 = jnp.exp(v2 - v1)
    w_first = 1.0 / (1.0 + e2)
    w_second = e2 / (1.0 + e2)
    wt_scr[...] = jnp.zeros_like(wt_scr)
    for g in range(N_GROUPS):
        for j in range(EXP_PER_GROUP):
            wj = jnp.where(rank[j] == 0, w_first, jnp.where(rank[j] == 1, w_second, 0.0)) * p_top
            e = g * EXP_PER_GROUP + j
            wt_scr[e:e + 1, :] = jnp.where(onehot[g], wj, 0.0)
    wts_ref[...] = wt_scr[...].T


def _merge_router(x, ret_in, att_in, gr, ga, gate1, wro, wao, wo, g_ffn, shift2, scale2, wr_t, br_col, tm, exact):
    s, d = x.shape
    w = HEAD_W
    assert s % tm == 0
    row = lambda width: pl.BlockSpec((tm, width), lambda i: (i, 0))
    mod = lambda a: pl.BlockSpec((tm if a.shape[0] > 1 else 1, d), lambda i: (i if a.shape[0] > 1 else 0, 0))
    if exact:
        layer = exact - 1
        wspec = lambda a: pl.BlockSpec((None,) + a.shape[1:], lambda i: (layer, 0, 0))
    else:
        wspec = lambda a: _full(a.shape)
    rr = wr_t.shape[0]
    return pl.pallas_call(
        functools.partial(_merge_router_kernel, exact=bool(exact)),
        grid=(s // tm,),
        in_specs=[row(d), row(w), row(w), row(d), row(d), mod(gate1),
                  wspec(wro), wspec(wao), wspec(wo),
                  _full((1, d)), mod(shift2), mod(scale2), _full((rr, d)), _full((rr, 1))],
        out_specs=(row(d), row(d), row(LANES)),
        out_shape=(jax.ShapeDtypeStruct((s, d), F32), jax.ShapeDtypeStruct((s, d), F32 if exact else BF16),
                   jax.ShapeDtypeStruct((s, LANES), F32)),
        scratch_shapes=[pltpu.VMEM((LANES, tm), F32)],
        compiler_params=_params("parallel"),
        name="merge_router",
    )(x, ret_in, att_in, gr, ga, gate1, wro, wao, wo, g_ffn, shift2, scale2, wr_t, br_col)


def _moe_kernel(x1_ref, h2_ref, wts_ref, g2_ref, wg_ref, wu_ref, wd_ref, gfin_ref, o_ref, acc_ref,
                *, final_norm, exact):
    e = pl.program_id(1)

    @pl.when(e == 0)
    def _():
        acc_ref[...] = jnp.zeros_like(acc_ref)

    h2 = h2_ref[...]
    wts = wts_ref[...]
    lane = lax.broadcasted_iota(jnp.int32, wts.shape, 1)
    wcol = jnp.sum(jnp.where(lane == e, wts, 0.0), axis=1, keepdims=True)
    act = _silu(_mm(h2, wg_ref, exact)) * _mm(h2, wu_ref, exact) * wcol
    acc_ref[...] += _mm(act, wd_ref, exact)

    @pl.when(e == pl.num_programs(1) - 1)
    def _():
        x2 = x1_ref[...] + g2_ref[...] * acc_ref[...]
        if final_norm:
            ms = jnp.mean(x2 * x2, axis=-1, keepdims=True)
            x2 = x2 * lax.rsqrt(ms + EPS) * gfin_ref[...]
        o_ref[...] = x2


def _moe(x1, h2, wts, gate2, wg, wu, wd, layer, g_final, tm, final_norm, exact):
    s, d = x1.shape
    ne, f = wg.shape[-3], wg.shape[-1]
    assert s % tm == 0
    row = lambda width: pl.BlockSpec((tm, width), lambda i, e: (i, 0))
    mod = lambda a: pl.BlockSpec((tm if a.shape[0] > 1 else 1, d), lambda i, e: (i if a.shape[0] > 1 else 0, 0))
    wspec = lambda r, c: pl.BlockSpec((None, None, r, c), lambda i, e: (layer, e, 0, 0))
    return pl.pallas_call(
        functools.partial(_moe_kernel, final_norm=final_norm, exact=exact),
        grid=(s // tm, ne),
        in_specs=[row(d), row(d), row(LANES), mod(gate2),
                  wspec(d, f), wspec(d, f), wspec(f, d),
                  pl.BlockSpec((1, d), lambda i, e: (0, 0))],
        out_specs=row(d),
        out_shape=jax.ShapeDtypeStruct((s, d), F32),
        scratch_shapes=[pltpu.VMEM((tm, d), F32)],
        compiler_params=_params("parallel", "arbitrary"),
        name="moe_experts",
    )(x1, h2, wts, gate2, wg, wu, wd, g_final)


def _paged_gate_kernel(pt_ref, q_ref, *rest):
    pages = rest[:PAGES_PER_STEP]
    sel_ref, qb_ref, acc_ref = rest[PAGES_PER_STEP:]
    j = pl.program_id(1)
    rows_per_block = ATT_HEADS * SUBLANES

    @pl.when(j == 0)
    def _():
        qb_ref[...] = jnp.broadcast_to(q_ref[...] * ATT_HD ** -0.5, qb_ref.shape)

    qb = qb_ref[...]
    blocks_per_step = PAGES_PER_STEP // PAGES_PER_BLOCK
    for t in range(blocks_per_step):
        tot = pages[PAGES_PER_BLOCK * t][...]
        for u in range(1, PAGES_PER_BLOCK):
            tot = tot + pages[PAGES_PER_BLOCK * t + u][...]
        part = (tot * qb).reshape(ATT_HEADS, ATT_HD // SUBLANES, SUBLANES, PAGE_SIZE).sum(axis=1)
        base = pl.multiple_of((j * blocks_per_step + t) * rows_per_block, rows_per_block)
        acc_ref[pl.ds(base, rows_per_block), :] = part.reshape(rows_per_block, PAGE_SIZE)

    @pl.when(j == pl.num_programs(1) - 1)
    def _():
        n_rows = acc_ref.shape[0]
        nb = n_rows // rows_per_block
        g = jnp.sum(acc_ref[...], axis=1, keepdims=True)
        r = lax.broadcasted_iota(jnp.int32, (n_rows, LANES), 0)
        lane = lax.broadcasted_iota(jnp.int32, (n_rows, LANES), 1)
        spread = jnp.where(lane == (r // SUBLANES) % ATT_HEADS, g, 0.0)
        gate = spread.reshape(nb, rows_per_block, LANES).sum(axis=1) * (1.0 / MOBA_BLOCK)
        _, idxs = _top3_rows(gate, nb)
        out = jnp.zeros(sel_ref.shape, jnp.int32)
        rr = lax.broadcasted_iota(jnp.int32, sel_ref.shape, 0)
        for t, idx in enumerate(idxs):
            out = jnp.where(rr == t, idx, out)
        sel_ref[...] = out


def _paged_gate(cache_kt, layer, page_table_flat, q_col, n_pages):
    b = q_col.shape[0]
    assert n_pages % PAGES_PER_STEP == 0 and PAGES_PER_STEP % PAGES_PER_BLOCK == 0
    nb = n_pages // PAGES_PER_BLOCK
    assert nb >= MOBA_TOPK

    def page_spec(g):
        return pl.BlockSpec((None, None, HEAD_W, PAGE_SIZE),
                            lambda bi, j, pt: (layer, pt[bi * n_pages + j * PAGES_PER_STEP + g], 0, 0))

    grid_spec = pltpu.PrefetchScalarGridSpec(
        num_scalar_prefetch=1,
        grid=(b, n_pages // PAGES_PER_STEP),
        in_specs=[pl.BlockSpec((None, HEAD_W, 1), lambda bi, j, pt: (bi, 0, 0))]
                 + [page_spec(g) for g in range(PAGES_PER_STEP)],
        out_specs=pl.BlockSpec((None, SUBLANES, LANES), lambda bi, j, pt: (bi, 0, 0)),
        scratch_shapes=[pltpu.VMEM((HEAD_W, PAGE_SIZE), F32),
                        pltpu.VMEM((nb * ATT_HEADS * SUBLANES, PAGE_SIZE), F32)],
    )
    return pl.pallas_call(
        _paged_gate_kernel,
        grid_spec=grid_spec,
        out_shape=jax.ShapeDtypeStruct((b, SUBLANES, LANES), jnp.int32),
        compiler_params=_params("parallel", "arbitrary"),
        name="paged_gate",
    )(page_table_flat, q_col, *([cache_kt] * PAGES_PER_STEP))


def _moba_sample_kernel(pt_ref, sel_ref, slopes_ref, q_ref, kn_ref, vn_ref, *rest, past_len):
    npg = MOBA_TOPK * PAGES_PER_BLOCK
    kp = rest[:npg]
    vp = rest[npg:2 * npg]
    o_ref = rest[2 * npg]
    b = pl.program_id(0)
    h = pl.program_id(1)
    slope = slopes_ref[h]
    lane = lax.broadcasted_iota(jnp.int32, (1, PAGE_SIZE), 1)
    q = q_ref[...] * ATT_HD ** -0.5
    scores = []
    for s in range(MOBA_TOPK):
        blk = sel_ref[(b * ATT_HEADS + h) * MOBA_TOPK + s]
        for t in range(PAGES_PER_BLOCK):
            kpos = blk * MOBA_BLOCK + t * PAGE_SIZE + lane
            dist = (past_len - kpos).astype(F32)
            sc = jnp.sum(kp[s * PAGES_PER_BLOCK + t][...] * q, axis=0, keepdims=True)
            scores.append(sc - slope * dist)
    s_own = jnp.sum(q * kn_ref[...], axis=0, keepdims=True)
    m = functools.reduce(jnp.maximum, [jnp.max(sc, axis=1, keepdims=True) for sc in scores] + [s_own])
    p_own = jnp.exp(s_own - m)
    l = p_own
    acc = p_own * vn_ref[...]
    for idx, sc in enumerate(scores):
        pr = jnp.exp(sc - m)
        l = l + jnp.sum(pr, axis=1, keepdims=True)
        acc = acc + jnp.sum(vp[idx][...] * pr, axis=1, keepdims=True)
    o_ref[...] = acc / l


def _moba_sample(cache_kt, cache_vt, layer, page_table_flat, sel_flat, q_col, k_col, v_col, n_pages):
    b = q_col.shape[0]
    npg = MOBA_TOPK * PAGES_PER_BLOCK
    h = jnp.arange(1, ATT_HEADS + 1, dtype=F32)
    slopes = 2.0 ** (-8.0 * h / ATT_HEADS)

    def page_spec(g):
        s, t = divmod(g, PAGES_PER_BLOCK)

        def imap(bi, hi, pt, sel):
            blk = sel[(bi * ATT_HEADS + hi) * MOBA_TOPK + s]
            return (layer, pt[bi * n_pages + blk * PAGES_PER_BLOCK + t], hi, 0)

        return pl.BlockSpec((None, None, ATT_HD, PAGE_SIZE), imap)

    vec = pl.BlockSpec((None, ATT_HD, 1), lambda bi, hi, pt, sel: (bi, hi, 0))
    grid_spec = pltpu.PrefetchScalarGridSpec(
        num_scalar_prefetch=2,
        grid=(b, ATT_HEADS),
        in_specs=[pl.BlockSpec(memory_space=pltpu.SMEM), vec, vec, vec]
                 + [page_spec(g) for g in range(npg)] * 2,
        out_specs=vec,
    )
    return pl.pallas_call(
        functools.partial(_moba_sample_kernel, past_len=n_pages * PAGE_SIZE),
        grid_spec=grid_spec,
        out_shape=jax.ShapeDtypeStruct((b, HEAD_W, 1), F32),
        compiler_params=_params("parallel", "arbitrary"),
        name="moba_sample",
    )(page_table_flat, sel_flat, slopes, q_col, k_col, v_col, *([cache_kt] * npg), *([cache_vt] * npg))


def _ret_sample_kernel(st_ref, q_ref, k_ref, v_ref, rg_ref, gamma_ref, ns_ref, o_ref):
    new = gamma_ref[...] * st_ref[...] + (k_ref[...] * RET_DK ** -0.5) * v_ref[...]
    ns_ref[...] = new
    o = jnp.sum(q_ref[...] * new, axis=1, keepdims=True)
    mu = jnp.mean(o, axis=-1, keepdims=True)
    dlt = o - mu
    var = jnp.mean(dlt * dlt, axis=-1, keepdims=True)
    o_ref[...] = _silu(rg_ref[...]) * (dlt * lax.rsqrt(var + EPS))


def _ret_sample(state, layer, rq, rk, rv, rg):
    b = state.shape[1]
    hh = jnp.arange(RET_HEADS, dtype=F32)
    gamma = jnp.exp(jnp.log1p(-(2.0 ** (-5.0 - hh)))).reshape(RET_HEADS, 1, 1)
    st_in = pl.BlockSpec((None, None, RET_HEADS, RET_DK, RET_DV), lambda i: (layer, i, 0, 0, 0))
    st_out = pl.BlockSpec((None, RET_HEADS, RET_DK, RET_DV), lambda i: (i, 0, 0, 0))
    col = pl.BlockSpec((None, RET_HEADS, RET_DK, 1), lambda i: (i, 0, 0, 0))
    rw = pl.BlockSpec((None, RET_HEADS, 1, RET_DV), lambda i: (i, 0, 0, 0))
    return pl.pallas_call(
        _ret_sample_kernel,
        grid=(b,),
        in_specs=[st_in, col, col, rw, rw, _full((RET_HEADS, 1, 1))],
        out_specs=(st_out, rw),
        out_shape=(jax.ShapeDtypeStruct(state.shape[1:], F32),
                   jax.ShapeDtypeStruct((b, RET_HEADS, 1, RET_DV), F32)),
        compiler_params=_params("parallel"),
        name="retention_sample",
    )(state, rq, rk, rv, rg, gamma)


def kernel(x_prompt, x_sample, cache_k, cache_v, state_ret, page_table, c_prompt, c_sample, w_ada, b_ada, g_mix, w_in, w_ret_o, w_att_o, w_out, g_ffn, w_rg, b_rg, w_re, b_re, w_gate_e, w_up_e, w_down_e, g_final):
    bp, s, d = x_prompt.shape
    bs, ds, _ = x_sample.shape
    assert bp == 1 and ds == 1
    depth = w_ada.shape[0]
    n_pool = cache_k.shape[1]
    n_pages = page_table.shape[1]
    w = HEAD_W

    n_rows = bp + bs
    pad = (-n_rows) % SUBLANES
    c_all = jnp.concatenate([c_prompt, c_sample, jnp.zeros((pad, d), F32)], axis=0)
    mods = _ada(c_all, w_ada, b_ada)

    cache_kt = cache_k.transpose(0, 1, 3, 4, 2).reshape(depth, n_pool, w, PAGE_SIZE)
    cache_vt = cache_v.transpose(0, 1, 3, 4, 2).reshape(depth, n_pool, w, PAGE_SIZE)
    pt_flat = page_table.reshape(-1)

    n_log = N_GROUPS + N_EXPERTS
    rr = 32
    g_fin = g_final.reshape(1, d)

    xp = x_prompt.reshape(s, d)
    xs = x_sample.reshape(bs, d)
    kp_l, vp_l, rp_l, ks_l, vs_l, rs_l = [], [], [], [], [], []
    for l in range(depth):
        last = l == depth - 1
        mp = [mods[l, 0:bp, j * d:(j + 1) * d] for j in range(6)]
        ms = [mods[l, bp:bp + bs, j * d:(j + 1) * d] for j in range(6)]
        w_in_b = w_in[l].astype(BF16)
        w_kvt_b = w_in[l][:, 5 * w:7 * w].T.astype(BF16)
        wro_b = w_ret_o[l].astype(BF16)
        wao_b = w_att_o[l].astype(BF16)
        wo_b = w_out[l].astype(BF16)
        wr_t = jnp.concatenate([w_rg[l].T, w_re[l].T, jnp.zeros((rr - n_log, d), F32)], axis=0)
        br_col = jnp.concatenate([b_rg[l], b_re[l], jnp.zeros((rr - n_log,), F32)]).reshape(rr, 1)
        gm = g_mix[l].reshape(1, d)
        gf = g_ffn[l].reshape(1, d)

        rq, rk, rv, rg, aq, akb, akt, avt, avtb, gr, ga, qc = _proj_prompt(xp, gm, mp[0], mp[1], w_in_b, w_kvt_b)
        ret_in, st = _retention_prompt(rq, rk, rv, rg, chunk=MOBA_BLOCK)
        att_in = _moba_prompt(aq, qc, akb, avtb)
        x1, h2, wts = _merge_router(xp, ret_in, att_in, gr, ga, mp[2], wro_b, wao_b, wo_b, gf, mp[3], mp[4],
                                    wr_t, br_col, tm=512, exact=0)
        xp = _moe(x1, h2, wts, mp[5], w_gate_e, w_up_e, w_down_e, l, g_fin, tm=1024, final_norm=last, exact=False)
        kp_l.append(akt)
        vp_l.append(avt)
        rp_l.append(st.reshape(bp, RET_HEADS, RET_DK, RET_DV))

        proj = _proj_sample(xs, gm, ms[0], ms[1], w_in, l)
        cut = lambda j, n=1: proj[:, j * w:(j + n) * w]
        rq_s, rk_s, rv_s, rg_s, aq_s, ak_s, av_s = (cut(j) for j in range(7))
        gr_s, ga_s = cut(7, 2), cut(9, 2)
        new_state, ret_s = _ret_sample(
            state_ret, l,
            rq_s.reshape(bs, RET_HEADS, RET_DK, 1), rk_s.reshape(bs, RET_HEADS, RET_DK, 1),
            rv_s.reshape(bs, RET_HEADS, 1, RET_DV), rg_s.reshape(bs, RET_HEADS, 1, RET_DV))
        q_col = aq_s.reshape(bs, w, 1)
        sel_s = _paged_gate(cache_kt, l, pt_flat, q_col, n_pages)
        sel_flat = sel_s[:, :MOBA_TOPK, :ATT_HEADS].transpose(0, 2, 1).reshape(-1)
        att_s = _moba_sample(cache_kt, cache_vt, l, pt_flat, sel_flat, q_col, ak_s.reshape(bs, w, 1),
                             av_s.reshape(bs, w, 1), n_pages)
        x1s, h2s, wts_s = _merge_router(xs, ret_s.reshape(bs, w), att_s.reshape(bs, w), gr_s, ga_s, ms[2],
                                        w_ret_o, w_att_o, w_out, gf, ms[3], ms[4], wr_t, br_col,
                                        tm=bs, exact=l + 1)
        xs = _moe(x1s, h2s, wts_s, ms[5], w_gate_e, w_up_e, w_down_e, l, g_fin, tm=bs, final_norm=last,
                  exact=True)
        ks_l.append(ak_s.reshape(bs, 1, ATT_HEADS, ATT_HD))
        vs_l.append(av_s.reshape(bs, 1, ATT_HEADS, ATT_HD))
        rs_l.append(new_state)

    def kv_out(parts):
        return jnp.stack(parts).reshape(depth, bp, ATT_HEADS, ATT_HD, s).transpose(0, 1, 4, 2, 3)

    return (xp.reshape(bp, s, d), xs.reshape(bs, 1, d), kv_out(kp_l), kv_out(vp_l), jnp.stack(rp_l),
            jnp.stack(ks_l), jnp.stack(vs_l), jnp.stack(rs_l))
```

```python
import functools

import jax
import jax.numpy as jnp
from jax import lax
from jax.experimental import pallas as pl
from jax.experimental.pallas import tpu as pltpu

EPS = 1e-6
RET_HEADS = 8
RET_DK = 64
RET_DV = 64
ATT_HEADS = 8
ATT_HD = 64
HEAD_W = RET_HEADS * RET_DK
MOBA_BLOCK = 256
MOBA_TOPK = 3
PAGE_SIZE = 128
PAGES_PER_BLOCK = MOBA_BLOCK // PAGE_SIZE
N_GROUPS = 4
EXP_PER_GROUP = 4
N_EXPERTS = N_GROUPS * EXP_PER_GROUP
SUBLANES = 8
LANES = 128
PAIRS = HEAD_W // LANES
NEG = -1e30
ALIBI_SLOPES = tuple(2.0 ** (-8.0 * (j + 1) / ATT_HEADS) for j in range(ATT_HEADS))
PAGES_PER_STEP = 16
MOE_CHUNK = 128
MOE_SORT_ROWS = 256
VMEM_LIMIT = 56 * 1024 * 1024

F32 = jnp.float32
BF16 = jnp.bfloat16
HI = lax.Precision.HIGHEST
NT = (((1,), (1,)), ((), ()))
TN = (((0,), (0,)), ((), ()))


def _dot(a, b):
    return jnp.dot(a, b, preferred_element_type=F32)


def _dot_nt(a, b):
    return lax.dot_general(a, b, NT, preferred_element_type=F32)


def _dot_hi(a, b):
    return jnp.dot(a, b, preferred_element_type=F32, precision=HI)


def _dot_nt_hi(a, b):
    return lax.dot_general(a, b, NT, preferred_element_type=F32, precision=HI)


def _mm(a, w_ref, exact):
    if exact:
        return _dot_hi(a.astype(F32), w_ref[...])
    return _dot(a.astype(BF16), w_ref[...].astype(BF16))


def _dot_split(a, b_bf16, lhs_exact=False):
    hi = a.astype(BF16)
    lo = (a - hi.astype(F32)).astype(BF16)
    if lhs_exact:
        return _dot(b_bf16, hi) + _dot(b_bf16, lo)
    return _dot(hi, b_bf16) + _dot(lo, b_bf16)


def _params(*sem):
    return pltpu.CompilerParams(dimension_semantics=sem, vmem_limit_bytes=VMEM_LIMIT)


def _full(shape):
    n = len(shape)
    return pl.BlockSpec(shape, lambda *_: (0,) * n)


def _modulate(x, g, shift, scale):
    ms = jnp.mean(x * x, axis=-1, keepdims=True)
    y = x * lax.rsqrt(ms + EPS) * g
    return y * (1.0 + scale) + shift


def _silu(x):
    return x * jax.nn.sigmoid(x)


def _head_lane_mask(shape, a):
    lane = lax.broadcasted_iota(jnp.int32, shape, len(shape) - 1)
    return (lane // RET_DK) == a


def _top3_rows(g, n_rows):
    row = lax.broadcasted_iota(jnp.int32, g.shape, 0)
    sel = jnp.zeros(g.shape, F32)
    idxs = []
    for _ in range(MOBA_TOPK):
        cmax = jnp.max(g, axis=0, keepdims=True)
        idx = jnp.min(jnp.where(g == cmax, row, n_rows), axis=0, keepdims=True)
        pick = jnp.logical_and(row == idx, cmax > -jnp.inf)
        sel = jnp.where(pick, 1.0, sel)
        g = jnp.where(pick, -jnp.inf, g)
        idxs.append(idx)
    return sel, idxs


def _ada_kernel(c_ref, w_ref, b_ref, o_ref):
    c = c_ref[...]
    o_ref[...] = _dot_hi(_silu(c), w_ref[...]) + b_ref[...]


def _ada(c_all, w_ada, b_ada):
    depth, d, d6 = w_ada.shape
    r = c_all.shape[0]
    return pl.pallas_call(
        _ada_kernel,
        grid=(depth, d6 // d),
        in_specs=[
            pl.BlockSpec((r, d), lambda l, j: (0, 0)),
            pl.BlockSpec((None, d, d), lambda l, j: (l, 0, j)),
            pl.BlockSpec((None, 1, d), lambda l, j: (l, 0, j)),
        ],
        out_specs=pl.BlockSpec((None, r, d), lambda l, j: (l, 0, j)),
        out_shape=jax.ShapeDtypeStruct((depth, r, d6), F32),
        compiler_params=_params("parallel", "parallel"),
        name="ada_mod",
    )(c_all, w_ada, b_ada.reshape(depth, 1, d6))


def _proj_prompt_kernel(x_ref, g_ref, sh_ref, sc_ref, w_ref, wkvt_ref,
                        rq_ref, rk_ref, rv_ref, rg_ref, aq_ref, akb_ref, akt_ref, avt_ref, avtb_ref,
                        gr_ref, ga_ref, qc_ref, means_ref, *, slopes):
    i = pl.program_id(0)
    nb = means_ref.shape[0]
    tm = x_ref.shape[0]
    w = HEAD_W

    @pl.when(i == 0)
    def _():
        means_ref[...] = jnp.zeros_like(means_ref)

    hb = _modulate(x_ref[...], g_ref[...], sh_ref[...], sc_ref[...]).astype(BF16)

    def col(c, n=1):
        return _dot(hb, w_ref[:, c * w:(c + n) * w])

    rq_ref[...] = col(0).astype(BF16)
    rk_ref[...] = (col(1) * RET_DK ** -0.5).astype(BF16)
    rv_ref[...] = col(2).astype(BF16)
    rg_ref[...] = col(3)
    q = col(4) * ATT_HD ** -0.5
    aq_ref[...] = q.astype(BF16)
    k = col(5)
    akb_ref[...] = k.astype(BF16)
    gr_ref[...] = col(7, 2)
    ga_ref[...] = col(9, 2)
    kvt = _dot_nt(wkvt_ref[...], hb)
    akt_ref[...] = kvt[:w]
    avt_ref[...] = kvt[w:]
    avtb_ref[...] = kvt[w:].astype(BF16)

    blk = lax.broadcasted_iota(jnp.int32, (nb, tm), 0)
    valid = blk < i
    blk_dist = (MOBA_BLOCK * (i - blk)).astype(F32)
    xrow = lax.broadcasted_iota(jnp.int32, (LANES - nb, tm), 0)
    xlane = lax.broadcasted_iota(jnp.int32, (LANES - nb, tm), 1).astype(F32)
    for p in range(PAIRS):
        mp = means_ref[:, p * LANES:(p + 1) * LANES]
        qp = q[:, p * LANES:(p + 1) * LANES]
        for a in range(2):
            slope = slopes[2 * p + a]
            qm = jnp.where(_head_lane_mask(qp.shape, a), qp, 0.0)
            gate = jnp.where(valid, _dot_nt_hi(mp, qm), -jnp.inf)
            sel, _ = _top3_rows(gate, nb)
            ct = jnp.where(sel > 0.5, -slope * blk_dist, NEG)
            extra = jnp.where(xrow == 0, -slope * xlane, jnp.where(xrow == 1, slope, 0.0))
            qc_ref[2 * p + a] = jnp.concatenate([ct, extra], axis=0).T.astype(BF16)

    means_ref[pl.ds(i, 1), :] = jnp.mean(k, axis=0, keepdims=True)


def _proj_prompt(x, g, shift, scale, w_in_b, w_kvt_b):
    s, d = x.shape
    tm = MOBA_BLOCK
    assert s % tm == 0
    nb = s // tm
    w = HEAD_W
    n_in = w_in_b.shape[1]
    row = lambda width: pl.BlockSpec((tm, width), lambda i: (i, 0))
    colblk = pl.BlockSpec((w, tm), lambda i: (0, i))
    vec = pl.BlockSpec((1, d), lambda i: (0, 0))
    out_shape = (
        jax.ShapeDtypeStruct((s, w), BF16),
        jax.ShapeDtypeStruct((s, w), BF16),
        jax.ShapeDtypeStruct((s, w), BF16),
        jax.ShapeDtypeStruct((s, w), F32),
        jax.ShapeDtypeStruct((s, w), BF16),
        jax.ShapeDtypeStruct((nb, tm, w), BF16),
        jax.ShapeDtypeStruct((w, s), F32),
        jax.ShapeDtypeStruct((w, s), F32),
        jax.ShapeDtypeStruct((nb, w, tm), BF16),
        jax.ShapeDtypeStruct((s, d), F32),
        jax.ShapeDtypeStruct((s, d), F32),
        jax.ShapeDtypeStruct((ATT_HEADS, s, LANES), BF16),
    )
    assert nb + 2 <= LANES
    out_specs = (
        row(w), row(w), row(w), row(w), row(w),
        pl.BlockSpec((None, tm, w), lambda i: (i, 0, 0)),
        colblk, colblk,
        pl.BlockSpec((None, w, tm), lambda i: (i, 0, 0)),
        row(d), row(d),
        pl.BlockSpec((ATT_HEADS, tm, LANES), lambda i: (0, i, 0)),
    )
    return pl.pallas_call(
        functools.partial(_proj_prompt_kernel, slopes=ALIBI_SLOPES),
        grid=(nb,),
        in_specs=[row(d), vec, vec, vec,
                  pl.BlockSpec((d, n_in), lambda i: (0, 0), pipeline_mode=pl.Buffered(1)),
                  pl.BlockSpec((2 * w, d), lambda i: (0, 0), pipeline_mode=pl.Buffered(1))],
        out_specs=out_specs,
        out_shape=out_shape,
        scratch_shapes=[pltpu.VMEM((nb, w), F32)],
        compiler_params=_params("arbitrary"),
        name="proj_prompt",
    )(x, g, shift, scale, w_in_b, w_kvt_b)


def _proj_sample_kernel(x_ref, g_ref, sh_ref, sc_ref, w_ref, o_ref):
    h = _modulate(x_ref[...], g_ref[...], sh_ref[...], sc_ref[...])
    o_ref[...] = _dot_hi(h, w_ref[...])


def _proj_sample(x, g, shift, scale, w_in, layer):
    b, d = x.shape
    n_in = w_in.shape[2]
    tn = HEAD_W
    return pl.pallas_call(
        _proj_sample_kernel,
        grid=(n_in // tn,),
        in_specs=[_full((b, d)), _full((1, d)), _full((b, d)), _full((b, d)),
                  pl.BlockSpec((None, d, tn), lambda j: (layer, 0, j))],
        out_specs=pl.BlockSpec((b, tn), lambda j: (0, j)),
        out_shape=jax.ShapeDtypeStruct((b, n_in), F32),
        compiler_params=_params("parallel"),
        name="proj_sample",
    )(x, g, shift, scale, w_in)


def _retention_kernel(q_ref, k_ref, v_ref, rg_ref, decay_ref, xi_ref, zeta_ref, gc_ref, avg_ref,
                      o_ref, st_ref):
    i = pl.program_id(0)

    @pl.when(i == 0)
    def _():
        st_ref[...] = jnp.zeros_like(st_ref)

    c = q_ref.shape[0]
    row = lax.broadcasted_iota(jnp.int32, (LANES, LANES), 0)
    lane = lax.broadcasted_iota(jnp.int32, (LANES, LANES), 1)
    same_head = (row // RET_DK) == (lane // RET_DV)
    avg = avg_ref[...]
    for p in range(PAIRS):
        sl = slice(p * LANES, (p + 1) * LANES)
        q = q_ref[:, sl]
        k = k_ref[:, sl]
        v = v_ref[:, sl]
        state = st_ref[p]
        cross = _dot(q, state.astype(BF16)) * xi_ref[:, sl]
        inner = []
        for a in range(2):
            qm = jnp.where(_head_lane_mask(q.shape, a), q, jnp.zeros_like(q))
            scores = _dot_nt(qm, k) * decay_ref[2 * p + a]
            inner.append(_dot(scores.astype(BF16), v))
        o = jnp.where(_head_lane_mask((c, LANES), 0), inner[0], inner[1]) + cross
        kz = (k.astype(F32) * zeta_ref[:, sl]).astype(BF16)
        upd = lax.dot_general(kz, v, TN, preferred_element_type=F32)
        st_ref[p] = gc_ref[:, sl] * state + jnp.where(same_head, upd, 0.0)
        mu = _dot_split(o, avg)
        dlt = o - mu
        var = _dot_split(dlt * dlt, avg)
        hn = dlt * lax.rsqrt(var + EPS)
        o_ref[:, sl] = (_silu(rg_ref[:, sl]) * hn).astype(BF16)


def _ret_tables(c):
    h = jnp.arange(RET_HEADS, dtype=F32)
    log_gamma = jnp.log1p(-(2.0 ** (-5.0 - h)))
    idx = jnp.arange(c, dtype=F32)
    diff = idx[:, None] - idx[None, :]
    decay = jnp.where(diff >= 0, jnp.exp(log_gamma[:, None, None] * jnp.maximum(diff, 0.0)), 0.0)
    xi = jnp.exp(log_gamma[None, :] * (idx[:, None] + 1.0))
    zeta = jnp.exp(log_gamma[None, :] * (c - 1.0 - idx[:, None]))
    gc = jnp.exp(log_gamma * c)[None, :]
    rep = lambda t: jnp.repeat(t, RET_DV, axis=1)
    return decay, rep(xi), rep(zeta), rep(gc)


def _head_avg_matrix():
    r = jnp.arange(LANES)
    return jnp.where((r[:, None] // RET_DV) == (r[None, :] // RET_DV), 1.0 / RET_DV, 0.0).astype(BF16)


def _retention_prompt(rq, rk, rv, rg, chunk):
    s, w = rq.shape
    assert s % chunk == 0
    decay, xi, zeta, gc = _ret_tables(chunk)
    row = pl.BlockSpec((chunk, w), lambda i: (i, 0))
    o, st = pl.pallas_call(
        _retention_kernel,
        grid=(s // chunk,),
        in_specs=[row, row, row, row,
                  _full((RET_HEADS, chunk, chunk)), _full((chunk, w)), _full((chunk, w)), _full((1, w)),
                  _full((LANES, LANES))],
        out_specs=(row, _full((PAIRS, LANES, LANES))),
        out_shape=(jax.ShapeDtypeStruct((s, w), BF16), jax.ShapeDtypeStruct((PAIRS, LANES, LANES), F32)),
        compiler_params=_params("arbitrary"),
        name="retention_prompt",
    )(rq, rk, rv, rg, decay, xi, zeta, gc, _head_avg_matrix())
    st = st.reshape(PAIRS, 2, RET_DK, 2, RET_DV)
    st = jnp.stack([st[:, 0, :, 0, :], st[:, 1, :, 1, :]], axis=1).reshape(RET_HEADS, RET_DK, RET_DV)
    return o, st


def _moba_prompt_kernel(q_ref, qc_ref, k_ref, vt_ref, ctab_ref, cown_ref, o_ref,
                        qcat_ref, sa_ref, sb_ref, ma_ref, mb_ref, m_ref, l_ref, acc_ref):
    i = pl.program_id(0)
    tq = q_ref.shape[0]
    tk = k_ref.shape[1]
    nb = k_ref.shape[0]
    heads = range(ATT_HEADS)
    hd = ATT_HD

    for p in range(PAIRS):
        qp = q_ref[:, p * LANES:(p + 1) * LANES]
        for a in range(2):
            h = 2 * p + a
            qm = jnp.where(_head_lane_mask(qp.shape, a), qp, jnp.zeros_like(qp))
            qcat_ref[h] = jnp.concatenate([qm, qc_ref[h]], axis=1)

    ones = jnp.ones((2 * SUBLANES, tk), BF16)

    def scores(n, h, s_ref, mx_ref):
        p = h // 2
        kcat = jnp.concatenate([k_ref[n, :, p * LANES:(p + 1) * LANES], ctab_ref[n]], axis=1)
        s = _dot_nt(kcat, qcat_ref[h])
        s_ref[h] = s
        mx_ref[h:h + 1, :] = jnp.max(s, axis=0, keepdims=True)

    def attend(n, h, s_ref, mx_ref):
        s = s_ref[h]
        m = m_ref[h:h + 1, :]
        m_new = jnp.maximum(m, mx_ref[h:h + 1, :])
        alpha = jnp.exp(m - m_new)
        pr = jnp.exp((s - m_new).astype(BF16))
        rows = slice(h * hd, (h + 1) * hd)
        pv = _dot(jnp.concatenate([vt_ref[n, rows, :], ones], axis=0), pr)
        m_ref[h:h + 1, :] = m_new
        l_ref[h:h + 1, :] = alpha * l_ref[h:h + 1, :] + pv[hd:hd + 1, :]
        acc_ref[rows, :] = alpha * acc_ref[rows, :] + pv[:hd, :]

    kk = lax.broadcasted_iota(jnp.int32, (tk, tq), 0)
    qq = lax.broadcasted_iota(jnp.int32, (tk, tq), 1)
    causal = kk <= qq
    cown = cown_ref[...]
    vt_own = vt_ref[i]
    for p in range(PAIRS):
        kcat = jnp.concatenate([k_ref[i, :, p * LANES:(p + 1) * LANES], cown], axis=1)
        for a in range(2):
            h = 2 * p + a
            s = jnp.where(causal, _dot_nt(kcat, qcat_ref[h]), NEG)
            m0 = jnp.max(s, axis=0, keepdims=True)
            pr = jnp.exp(s - m0)
            m_ref[h:h + 1, :] = m0
            l_ref[h:h + 1, :] = jnp.sum(pr, axis=0, keepdims=True)
            acc_ref[h * hd:(h + 1) * hd, :] = _dot(vt_own[h * hd:(h + 1) * hd, :], pr.astype(BF16))

    for h in heads:
        scores(0, h, sa_ref, ma_ref)

    @pl.loop(0, (i + 1) // 2)
    def _(t):
        nxt = jnp.minimum(2 * t + 2, nb - 1)
        for h in heads:
            scores(2 * t + 1, h, sb_ref, mb_ref)
            attend(2 * t, h, sa_ref, ma_ref)
        for h in heads:
            scores(nxt, h, sa_ref, ma_ref)
            attend(2 * t + 1, h, sb_ref, mb_ref)

    for h in heads:
        rows = slice(h * hd, (h + 1) * hd)
        acc_ref[rows, :] = acc_ref[rows, :] / l_ref[h:h + 1, :]
    o_ref[...] = acc_ref[...].T.astype(BF16)


def _moba_bias_table(nb, tk):
    lane = jnp.arange(LANES)[None, None, :]
    blk = jnp.arange(nb)[:, None, None]
    kk = jnp.arange(tk, dtype=F32)[None, :, None]
    t = jnp.where(lane == blk, 1.0, 0.0)
    t = jnp.where(lane == nb, 1.0, t)
    return jnp.where(lane == nb + 1, kk, t).astype(BF16)


def _moba_prompt(aq, qc, akb, avtb):
    s, w = aq.shape
    nb, tk, _ = akb.shape
    tq = MOBA_BLOCK
    assert nb % 2 == 0 and tk == tq
    ctab = _moba_bias_table(nb, tk)
    cown = jnp.where(jnp.arange(LANES)[None, :] < nb, 0, ctab[0])
    once = dict(pipeline_mode=pl.Buffered(1))
    return pl.pallas_call(
        _moba_prompt_kernel,
        grid=(s // tq,),
        in_specs=[
            pl.BlockSpec((tq, w), lambda i: (i, 0)),
            pl.BlockSpec((ATT_HEADS, tq, LANES), lambda i: (0, i, 0)),
            pl.BlockSpec((nb, tk, w), lambda i: (0, 0, 0), **once),
            pl.BlockSpec((nb, w, tk), lambda i: (0, 0, 0), **once),
            pl.BlockSpec((nb, tk, LANES), lambda i: (0, 0, 0), **once),
            pl.BlockSpec((tk, LANES), lambda i: (0, 0), **once),
        ],
        out_specs=pl.BlockSpec((tq, w), lambda i: (i, 0)),
        out_shape=jax.ShapeDtypeStruct((s, w), BF16),
        scratch_shapes=[
            pltpu.VMEM((ATT_HEADS, tq, 2 * LANES), BF16),
            pltpu.VMEM((ATT_HEADS, tk, tq), F32),
            pltpu.VMEM((ATT_HEADS, tk, tq), F32),
            pltpu.VMEM((ATT_HEADS, tq), F32),
            pltpu.VMEM((ATT_HEADS, tq), F32),
            pltpu.VMEM((ATT_HEADS, tq), F32),
            pltpu.VMEM((ATT_HEADS, tq), F32),
            pltpu.VMEM((w, tq), F32),
        ],
        compiler_params=_params("arbitrary"),
        name="moba_prompt",
    )(aq, qc, akb, avtb, ctab, cown)


def _merge_router_kernel(x_ref, ret_ref, att_ref, gr_ref, ga_ref, g1_ref, wro_ref, wao_ref, wo_ref,
                         gf_ref, sh_ref, sc_ref, wr_ref, br_ref,
                         x1_ref, h2_ref, wts_ref, wtst_ref, gcnt_ref, *, exact):
    ret = _mm(ret_ref[...], wro_ref, exact)
    att = _mm(att_ref[...], wao_ref, exact)
    merged = jax.nn.sigmoid(gr_ref[...]) * ret + jax.nn.sigmoid(ga_ref[...]) * att
    x1 = x_ref[...] + g1_ref[...] * _mm(merged, wo_ref, exact)
    x1_ref[...] = x1
    h2 = _modulate(x1, gf_ref[...], sh_ref[...], sc_ref[...])
    h2_ref[...] = h2.astype(h2_ref.dtype)

    logits = _dot_nt_hi(wr_ref[...], h2) + br_ref[...]
    lg = [logits[g:g + 1, :] for g in range(N_GROUPS)]
    le = [logits[N_GROUPS + e:N_GROUPS + e + 1, :] for e in range(N_EXPERTS)]
    mg = functools.reduce(jnp.maximum, lg)
    eg = [jnp.exp(v - mg) for v in lg]
    den = functools.reduce(jnp.add, eg)
    pg = [v / den for v in eg]
    p_top = functools.reduce(jnp.maximum, pg)
    taken = jnp.zeros(p_top.shape, jnp.bool_)
    onehot = []
    for g in range(N_GROUPS):
        hit = jnp.logical_and(pg[g] == p_top, jnp.logical_not(taken))
        onehot.append(hit)
        taken = jnp.logical_or(taken, hit)
    vin = []
    for j in range(EXP_PER_GROUP):
        v = jnp.zeros_like(p_top)
        for g in range(N_GROUPS):
            v = jnp.where(onehot[g], le[g * EXP_PER_GROUP + j], v)
        vin.append(v)
    rank = []
    for j in range(EXP_PER_GROUP):
        r = jnp.zeros(p_top.shape, jnp.int32)
        for kx in range(EXP_PER_GROUP):
            if kx == j:
                continue
            ahead = (vin[kx] > vin[j]) | ((vin[kx] == vin[j]) & (kx < j))
            r = r + ahead.astype(jnp.int32)
        rank.append(r)
    v1 = functools.reduce(jnp.maximum, vin)
    v2 = jnp.full_like(v1, -jnp.inf)
    for j in range(EXP_PER_GROUP):
        v2 = jnp.where(rank[j] == 1, vin[j], v2)
    e2 = jnp.exp(v2 - v1)
    w_first = 1.0 / (1.0 + e2)
    w_second = e2 / (1.0 + e2)
    wtst_ref[...] = jnp.zeros_like(wtst_ref)
    row8 = lax.broadcasted_iota(jnp.int32, gcnt_ref.shape, 0)
    lane8 = lax.broadcasted_iota(jnp.int32, gcnt_ref.shape, 1)
    gcnt = jnp.zeros(gcnt_ref.shape, F32)
    for g in range(N_GROUPS):
        hot = jnp.where(onehot[g], 1.0, 0.0)
        wtst_ref[N_EXPERTS + g:N_EXPERTS + g + 1, :] = hot
        n_g = jnp.sum(hot, axis=1, keepdims=True)
        gcnt = jnp.where(jnp.logical_and(row8 == 0, lane8 == g), n_g, gcnt)
        for j in range(EXP_PER_GROUP):
            wj = jnp.where(rank[j] == 0, w_first, jnp.where(rank[j] == 1, w_second, 0.0)) * p_top
            e = g * EXP_PER_GROUP + j
            wtst_ref[e:e + 1, :] = jnp.where(onehot[g], wj, 0.0)
    wts_ref[...] = wtst_ref[...].T
    gcnt_ref[...] = gcnt.astype(jnp.int32)


def _merge_router(x, ret_in, att_in, gr, ga, gate1, wro, wao, wo, g_ffn, shift2, scale2, wr_t, br_col, tm, exact):
    s, d = x.shape
    w = HEAD_W
    assert s % tm == 0
    row = lambda width: pl.BlockSpec((tm, width), lambda i: (i, 0))
    mod = lambda a: pl.BlockSpec((tm if a.shape[0] > 1 else 1, d), lambda i: (i if a.shape[0] > 1 else 0, 0))
    if exact:
        layer = exact - 1
        wspec = lambda a: pl.BlockSpec((None,) + a.shape[1:], lambda i: (layer, 0, 0))
    else:
        wspec = lambda a: _full(a.shape)
    rr = wr_t.shape[0]
    return pl.pallas_call(
        functools.partial(_merge_router_kernel, exact=bool(exact)),
        grid=(s // tm,),
        in_specs=[row(d), row(w), row(w), row(d), row(d), mod(gate1),
                  wspec(wro), wspec(wao), wspec(wo),
                  _full((1, d)), mod(shift2), mod(scale2), _full((rr, d)), _full((rr, 1))],
        out_specs=(row(d), row(d), row(LANES),
                   pl.BlockSpec((LANES, tm), lambda i: (0, i)),
                   pl.BlockSpec((None, SUBLANES, LANES), lambda i: (i, 0, 0))),
        out_shape=(jax.ShapeDtypeStruct((s, d), F32), jax.ShapeDtypeStruct((s, d), F32 if exact else BF16),
                   jax.ShapeDtypeStruct((s, LANES), F32),
                   jax.ShapeDtypeStruct((LANES, s), F32),
                   jax.ShapeDtypeStruct((s // tm, SUBLANES, LANES), jnp.int32)),
        compiler_params=_params("parallel"),
        name="merge_router",
    )(x, ret_in, att_in, gr, ga, gate1, wro, wao, wo, g_ffn, shift2, scale2, wr_t, br_col)


def _moe_kernel(x1_ref, h2_ref, wts_ref, g2_ref, wg_ref, wu_ref, wd_ref, gfin_ref, o_ref, acc_ref,
                *, final_norm, exact):
    e = pl.program_id(1)

    @pl.when(e == 0)
    def _():
        acc_ref[...] = jnp.zeros_like(acc_ref)

    h2 = h2_ref[...]
    wts = wts_ref[...]
    lane = lax.broadcasted_iota(jnp.int32, wts.shape, 1)
    wcol = jnp.sum(jnp.where(lane == e, wts, 0.0), axis=1, keepdims=True)
    act = _silu(_mm(h2, wg_ref, exact)) * _mm(h2, wu_ref, exact) * wcol
    acc_ref[...] += _mm(act, wd_ref, exact)

    @pl.when(e == pl.num_programs(1) - 1)
    def _():
        x2 = x1_ref[...] + g2_ref[...] * acc_ref[...]
        if final_norm:
            ms = jnp.mean(x2 * x2, axis=-1, keepdims=True)
            x2 = x2 * lax.rsqrt(ms + EPS) * gfin_ref[...]
        o_ref[...] = x2


def _moe(x1, h2, wts, gate2, wg, wu, wd, layer, g_final, tm, final_norm, exact):
    s, d = x1.shape
    ne, f = wg.shape[-3], wg.shape[-1]
    assert s % tm == 0
    row = lambda width: pl.BlockSpec((tm, width), lambda i, e: (i, 0))
    mod = lambda a: pl.BlockSpec((tm if a.shape[0] > 1 else 1, d), lambda i, e: (i if a.shape[0] > 1 else 0, 0))
    wspec = lambda r, c: pl.BlockSpec((None, None, r, c), lambda i, e: (layer, e, 0, 0))
    return pl.pallas_call(
        functools.partial(_moe_kernel, final_norm=final_norm, exact=exact),
        grid=(s // tm, ne),
        in_specs=[row(d), row(d), row(LANES), mod(gate2),
                  wspec(d, f), wspec(d, f), wspec(f, d),
                  pl.BlockSpec((1, d), lambda i, e: (0, 0))],
        out_specs=row(d),
        out_shape=jax.ShapeDtypeStruct((s, d), F32),
        scratch_shapes=[pltpu.VMEM((tm, d), F32)],
        compiler_params=_params("parallel", "arbitrary"),
        name="moe_experts",
    )(x1, h2, wts, gate2, wg, wu, wd, g_final)


def _moe_sort(starts, h2_ref, wts_ref, wtst_ref, p_scr, pt_scr, hs_scr, ws_scr, acc_scr):
    tm = h2_ref.shape[0]
    ch = MOE_SORT_ROWS
    lo, hi = N_EXPERTS, N_EXPERTS + N_GROUPS
    wts = wts_ref[...]
    lane_t = lax.broadcasted_iota(jnp.int32, wts.shape, 1)
    hot_t = jnp.where(jnp.logical_and(lane_t >= lo, lane_t < hi), wts, 0.0)
    wtst = wtst_ref[...]
    row_l = lax.broadcasted_iota(jnp.int32, wtst.shape, 0)
    hot_l = jnp.where(jnp.logical_and(row_l >= lo, row_l < hi), wtst, 0.0)
    start_t = jnp.zeros((1, LANES), F32)
    start_l = jnp.zeros((LANES, 1), F32)
    lane1 = lax.broadcasted_iota(jnp.int32, (1, LANES), 1)
    row1 = lax.broadcasted_iota(jnp.int32, (LANES, 1), 0)
    for g in range(N_GROUPS):
        sg = starts[g].astype(F32)
        start_t = jnp.where(lane1 == lo + g, sg, start_t)
        start_l = jnp.where(row1 == lo + g, sg, start_l)
    hot_t_b = hot_t.astype(BF16)
    hot_l_b = hot_l.astype(BF16)
    for c in range(tm // ch):
        sl = slice(c * ch, (c + 1) * ch)
        tok = lax.broadcasted_iota(jnp.int32, (ch, tm), 0) + c * ch
        other = lax.broadcasted_iota(jnp.int32, (ch, tm), 1)
        earlier = jnp.where(other < tok, 1.0, 0.0).astype(BF16)
        rank = _dot(earlier, hot_t_b)
        pos = jnp.sum(hot_t[sl] * (rank + start_t), axis=1, keepdims=True).astype(jnp.int32)
        pt_scr[sl, :] = jnp.where(other == pos, 1.0, 0.0).astype(BF16)
        tok = lax.broadcasted_iota(jnp.int32, (tm, ch), 1) + c * ch
        other = lax.broadcasted_iota(jnp.int32, (tm, ch), 0)
        earlier = jnp.where(other < tok, 1.0, 0.0).astype(BF16)
        rank = _dot(hot_l_b, earlier)
        pos = jnp.sum(hot_l[:, sl] * (rank + start_l), axis=0, keepdims=True).astype(jnp.int32)
        p_scr[:, sl] = jnp.where(other == pos, 1.0, 0.0).astype(BF16)
    p = p_scr[...]
    hs_scr[...] = _dot(p, h2_ref[...]).astype(BF16)
    ws_scr[...] = _dot_split(wts, p, lhs_exact=True)
    acc_scr[...] = jnp.zeros_like(acc_scr)


def _moe_sorted_kernel(cnt_ref, x1_ref, h2_ref, wts_ref, wtst_ref, g2_ref, wg_ref, wu_ref, wd_ref, gfin_ref,
                       o_ref, p_scr, pt_scr, hs_scr, ws_scr, acc_scr, *, final_norm, router_tiles):
    i = pl.program_id(0)
    e = pl.program_id(1)
    ch = MOE_CHUNK
    counts = []
    for g in range(N_GROUPS):
        n = cnt_ref[i * router_tiles * N_GROUPS + g]
        for j in range(1, router_tiles):
            n = n + cnt_ref[(i * router_tiles + j) * N_GROUPS + g]
        counts.append(n)
    starts = [jnp.int32(0)]
    for g in range(N_GROUPS - 1):
        starts.append(starts[-1] + counts[g])

    @pl.when(e == 0)
    def _():
        _moe_sort(starts, h2_ref, wts_ref, wtst_ref, p_scr, pt_scr, hs_scr, ws_scr, acc_scr)

    group = e // EXP_PER_GROUP
    seg0, seg_n = starts[0], counts[0]
    for g in range(1, N_GROUPS):
        seg0 = jnp.where(group == g, starts[g], seg0)
        seg_n = jnp.where(group == g, counts[g], seg_n)
    lane = lax.broadcasted_iota(jnp.int32, (ch, LANES), 1)

    def chunk(c, carry):
        rows = pl.ds(pl.multiple_of(c * ch, ch), ch)
        hs = hs_scr[rows, :]
        wcol = jnp.sum(jnp.where(lane == e, ws_scr[rows, :], 0.0), axis=1, keepdims=True)
        act = _silu(_dot(hs, wg_ref[...])) * _dot(hs, wu_ref[...]) * wcol
        acc_scr[rows, :] += _dot(act.astype(BF16), wd_ref[...])
        return carry

    lax.fori_loop(seg0 // ch, (seg0 + seg_n + ch - 1) // ch, chunk, 0)

    @pl.when(e == pl.num_programs(1) - 1)
    def _():
        y = _dot(pt_scr[...], acc_scr[...].astype(BF16))
        x2 = x1_ref[...] + g2_ref[...] * y
        if final_norm:
            ms = jnp.mean(x2 * x2, axis=-1, keepdims=True)
            x2 = x2 * lax.rsqrt(ms + EPS) * gfin_ref[...]
        o_ref[...] = x2


def _moe_sorted(x1, h2, wts, wtst, gcnt, gate2, wg_b, wu_b, wd_b, g_final, tm, final_norm):
    s, d = x1.shape
    ne, _, f = wg_b.shape
    router_tiles = gcnt.shape[0] * tm // s
    assert s % tm == 0 and tm % MOE_CHUNK == 0 and tm % MOE_SORT_ROWS == 0 and router_tiles >= 1
    cnt_flat = gcnt[:, 0, :N_GROUPS].reshape(-1)
    row = lambda width: pl.BlockSpec((tm, width), lambda i, e, cnt: (i, 0))
    wspec = lambda r, c: pl.BlockSpec((None, r, c), lambda i, e, cnt: (e, 0, 0))
    vec = pl.BlockSpec((1, d), lambda i, e, cnt: (0, 0))
    grid_spec = pltpu.PrefetchScalarGridSpec(
        num_scalar_prefetch=1,
        grid=(s // tm, ne),
        in_specs=[row(d), row(d), row(LANES), pl.BlockSpec((LANES, tm), lambda i, e, cnt: (0, i)), vec,
                  wspec(d, f), wspec(d, f), wspec(f, d), vec],
        out_specs=row(d),
        scratch_shapes=[pltpu.VMEM((tm, tm), BF16),
                        pltpu.VMEM((tm, tm), BF16),
                        pltpu.VMEM((tm, d), BF16),
                        pltpu.VMEM((tm, LANES), F32),
                        pltpu.VMEM((tm, d), F32)],
    )
    return pl.pallas_call(
        functools.partial(_moe_sorted_kernel, final_norm=final_norm, router_tiles=router_tiles),
        grid_spec=grid_spec,
        out_shape=jax.ShapeDtypeStruct((s, d), F32),
        compiler_params=_params("parallel", "arbitrary"),
        name="moe_sorted",
    )(cnt_flat, x1, h2, wts, wtst, gate2, wg_b, wu_b, wd_b, g_final)


def _paged_gate_kernel(pt_ref, q_ref, *rest):
    pages = rest[:PAGES_PER_STEP]
    sel_ref, qb_ref, acc_ref = rest[PAGES_PER_STEP:]
    j = pl.program_id(1)
    rows_per_block = ATT_HEADS * SUBLANES

    @pl.when(j == 0)
    def _():
        qb_ref[...] = jnp.broadcast_to(q_ref[...] * ATT_HD ** -0.5, qb_ref.shape)

    qb = qb_ref[...]
    blocks_per_step = PAGES_PER_STEP // PAGES_PER_BLOCK
    for t in range(blocks_per_step):
        tot = pages[PAGES_PER_BLOCK * t][...]
        for u in range(1, PAGES_PER_BLOCK):
            tot = tot + pages[PAGES_PER_BLOCK * t + u][...]
        part = (tot * qb).reshape(ATT_HEADS, ATT_HD // SUBLANES, SUBLANES, PAGE_SIZE).sum(axis=1)
        base = pl.multiple_of((j * blocks_per_step + t) * rows_per_block, rows_per_block)
        acc_ref[pl.ds(base, rows_per_block), :] = part.reshape(rows_per_block, PAGE_SIZE)

    @pl.when(j == pl.num_programs(1) - 1)
    def _():
        n_rows = acc_ref.shape[0]
        nb = n_rows // rows_per_block
        g = jnp.sum(acc_ref[...], axis=1, keepdims=True)
        r = lax.broadcasted_iota(jnp.int32, (n_rows, LANES), 0)
        lane = lax.broadcasted_iota(jnp.int32, (n_rows, LANES), 1)
        spread = jnp.where(lane == (r // SUBLANES) % ATT_HEADS, g, 0.0)
        gate = spread.reshape(nb, rows_per_block, LANES).sum(axis=1) * (1.0 / MOBA_BLOCK)
        _, idxs = _top3_rows(gate, nb)
        out = jnp.zeros(sel_ref.shape, jnp.int32)
        rr = lax.broadcasted_iota(jnp.int32, sel_ref.shape, 0)
        for t, idx in enumerate(idxs):
            out = jnp.where(rr == t, idx, out)
        sel_ref[...] = out


def _paged_gate(cache_kt, layer, page_table_flat, q_col, n_pages):
    b = q_col.shape[0]
    assert n_pages % PAGES_PER_STEP == 0 and PAGES_PER_STEP % PAGES_PER_BLOCK == 0
    nb = n_pages // PAGES_PER_BLOCK
    assert nb >= MOBA_TOPK

    def page_spec(g):
        return pl.BlockSpec((None, None, HEAD_W, PAGE_SIZE),
                            lambda bi, j, pt: (layer, pt[bi * n_pages + j * PAGES_PER_STEP + g], 0, 0))

    grid_spec = pltpu.PrefetchScalarGridSpec(
        num_scalar_prefetch=1,
        grid=(b, n_pages // PAGES_PER_STEP),
        in_specs=[pl.BlockSpec((None, HEAD_W, 1), lambda bi, j, pt: (bi, 0, 0))]
                 + [page_spec(g) for g in range(PAGES_PER_STEP)],
        out_specs=pl.BlockSpec((None, SUBLANES, LANES), lambda bi, j, pt: (bi, 0, 0)),
        scratch_shapes=[pltpu.VMEM((HEAD_W, PAGE_SIZE), F32),
                        pltpu.VMEM((nb * ATT_HEADS * SUBLANES, PAGE_SIZE), F32)],
    )
    return pl.pallas_call(
        _paged_gate_kernel,
        grid_spec=grid_spec,
        out_shape=jax.ShapeDtypeStruct((b, SUBLANES, LANES), jnp.int32),
        compiler_params=_params("parallel", "arbitrary"),
        name="paged_gate",
    )(page_table_flat, q_col, *([cache_kt] * PAGES_PER_STEP))


def _moba_sample_kernel(pt_ref, sel_ref, slopes_ref, q_ref, kn_ref, vn_ref, *rest, past_len):
    npg = MOBA_TOPK * PAGES_PER_BLOCK
    kp = rest[:npg]
    vp = rest[npg:2 * npg]
    o_ref = rest[2 * npg]
    b = pl.program_id(0)
    h = pl.program_id(1)
    slope = slopes_ref[h]
    lane = lax.broadcasted_iota(jnp.int32, (1, PAGE_SIZE), 1)
    q = q_ref[...] * ATT_HD ** -0.5
    scores = []
    for s in range(MOBA_TOPK):
        blk = sel_ref[(b * ATT_HEADS + h) * MOBA_TOPK + s]
        for t in range(PAGES_PER_BLOCK):
            kpos = blk * MOBA_BLOCK + t * PAGE_SIZE + lane
            dist = (past_len - kpos).astype(F32)
            sc = jnp.sum(kp[s * PAGES_PER_BLOCK + t][...] * q, axis=0, keepdims=True)
            scores.append(sc - slope * dist)
    s_own = jnp.sum(q * kn_ref[...], axis=0, keepdims=True)
    m = functools.reduce(jnp.maximum, [jnp.max(sc, axis=1, keepdims=True) for sc in scores] + [s_own])
    p_own = jnp.exp(s_own - m)
    l = p_own
    acc = p_own * vn_ref[...]
    for idx, sc in enumerate(scores):
        pr = jnp.exp(sc - m)
        l = l + jnp.sum(pr, axis=1, keepdims=True)
        acc = acc + jnp.sum(vp[idx][...] * pr, axis=1, keepdims=True)
    o_ref[...] = acc / l


def _moba_sample(cache_kt, cache_vt, layer, page_table_flat, sel_flat, q_col, k_col, v_col, n_pages):
    b = q_col.shape[0]
    npg = MOBA_TOPK * PAGES_PER_BLOCK
    h = jnp.arange(1, ATT_HEADS + 1, dtype=F32)
    slopes = 2.0 ** (-8.0 * h / ATT_HEADS)

    def page_spec(g):
        s, t = divmod(g, PAGES_PER_BLOCK)

        def imap(bi, hi, pt, sel):
            blk = sel[(bi * ATT_HEADS + hi) * MOBA_TOPK + s]
            return (layer, pt[bi * n_pages + blk * PAGES_PER_BLOCK + t], hi, 0)

        return pl.BlockSpec((None, None, ATT_HD, PAGE_SIZE), imap)

    vec = pl.BlockSpec((None, ATT_HD, 1), lambda bi, hi, pt, sel: (bi, hi, 0))
    grid_spec = pltpu.PrefetchScalarGridSpec(
        num_scalar_prefetch=2,
        grid=(b, ATT_HEADS),
        in_specs=[pl.BlockSpec(memory_space=pltpu.SMEM), vec, vec, vec]
                 + [page_spec(g) for g in range(npg)] * 2,
        out_specs=vec,
    )
    return pl.pallas_call(
        functools.partial(_moba_sample_kernel, past_len=n_pages * PAGE_SIZE),
        grid_spec=grid_spec,
        out_shape=jax.ShapeDtypeStruct((b, HEAD_W, 1), F32),
        compiler_params=_params("parallel", "arbitrary"),
        name="moba_sample",
    )(page_table_flat, sel_flat, slopes, q_col, k_col, v_col, *([cache_kt] * npg), *([cache_vt] * npg))


def _ret_sample_kernel(st_ref, q_ref, k_ref, v_ref, rg_ref, gamma_ref, ns_ref, o_ref):
    new = gamma_ref[...] * st_ref[...] + (k_ref[...] * RET_DK ** -0.5) * v_ref[...]
    ns_ref[...] = new
    o = jnp.sum(q_ref[...] * new, axis=1, keepdims=True)
    mu = jnp.mean(o, axis=-1, keepdims=True)
    dlt = o - mu
    var = jnp.mean(dlt * dlt, axis=-1, keepdims=True)
    o_ref[...] = _silu(rg_ref[...]) * (dlt * lax.rsqrt(var + EPS))


def _ret_sample(state, layer, rq, rk, rv, rg):
    b = state.shape[1]
    hh = jnp.arange(RET_HEADS, dtype=F32)
    gamma = jnp.exp(jnp.log1p(-(2.0 ** (-5.0 - hh)))).reshape(RET_HEADS, 1, 1)
    st_in = pl.BlockSpec((None, None, RET_HEADS, RET_DK, RET_DV), lambda i: (layer, i, 0, 0, 0))
    st_out = pl.BlockSpec((None, RET_HEADS, RET_DK, RET_DV), lambda i: (i, 0, 0, 0))
    col = pl.BlockSpec((None, RET_HEADS, RET_DK, 1), lambda i: (i, 0, 0, 0))
    rw = pl.BlockSpec((None, RET_HEADS, 1, RET_DV), lambda i: (i, 0, 0, 0))
    return pl.pallas_call(
        _ret_sample_kernel,
        grid=(b,),
        in_specs=[st_in, col, col, rw, rw, _full((RET_HEADS, 1, 1))],
        out_specs=(st_out, rw),
        out_shape=(jax.ShapeDtypeStruct(state.shape[1:], F32),
                   jax.ShapeDtypeStruct((b, RET_HEADS, 1, RET_DV), F32)),
        compiler_params=_params("parallel"),
        name="retention_sample",
    )(state, rq, rk, rv, rg, gamma)


def kernel(x_prompt, x_sample, cache_k, cache_v, state_ret, page_table, c_prompt, c_sample, w_ada, b_ada, g_mix, w_in, w_ret_o, w_att_o, w_out, g_ffn, w_rg, b_rg, w_re, b_re, w_gate_e, w_up_e, w_down_e, g_final):
    bp, s, d = x_prompt.shape
    bs, ds, _ = x_sample.shape
    assert bp == 1 and ds == 1
    depth = w_ada.shape[0]
    n_pool = cache_k.shape[1]
    n_pages = page_table.shape[1]
    w = HEAD_W

    n_rows = bp + bs
    pad = (-n_rows) % SUBLANES
    c_all = jnp.concatenate([c_prompt, c_sample, jnp.zeros((pad, d), F32)], axis=0)
    mods = _ada(c_all, w_ada, b_ada)

    cache_kt = cache_k.transpose(0, 1, 3, 4, 2).reshape(depth, n_pool, w, PAGE_SIZE)
    cache_vt = cache_v.transpose(0, 1, 3, 4, 2).reshape(depth, n_pool, w, PAGE_SIZE)
    pt_flat = page_table.reshape(-1)

    n_log = N_GROUPS + N_EXPERTS
    rr = 32
    g_fin = g_final.reshape(1, d)

    xp = x_prompt.reshape(s, d)
    xs = x_sample.reshape(bs, d)
    kp_l, vp_l, rp_l, ks_l, vs_l, rs_l = [], [], [], [], [], []
    for l in range(depth):
        last = l == depth - 1
        mp = [mods[l, 0:bp, j * d:(j + 1) * d] for j in range(6)]
        ms = [mods[l, bp:bp + bs, j * d:(j + 1) * d] for j in range(6)]
        w_in_b = w_in[l].astype(BF16)
        w_kvt_b = w_in[l][:, 5 * w:7 * w].T.astype(BF16)
        wro_b = w_ret_o[l].astype(BF16)
        wao_b = w_att_o[l].astype(BF16)
        wo_b = w_out[l].astype(BF16)
        wr_t = jnp.concatenate([w_rg[l].T, w_re[l].T, jnp.zeros((rr - n_log, d), F32)], axis=0)
        br_col = jnp.concatenate([b_rg[l], b_re[l], jnp.zeros((rr - n_log,), F32)]).reshape(rr, 1)
        gm = g_mix[l].reshape(1, d)
        gf = g_ffn[l].reshape(1, d)

        rq, rk, rv, rg, aq, akb, akt, avt, avtb, gr, ga, qc = _proj_prompt(xp, gm, mp[0], mp[1], w_in_b, w_kvt_b)
        ret_in, st = _retention_prompt(rq, rk, rv, rg, chunk=MOBA_BLOCK)
        att_in = _moba_prompt(aq, qc, akb, avtb)
        x1, h2, wts, wtst, gcnt = _merge_router(xp, ret_in, att_in, gr, ga, mp[2], wro_b, wao_b, wo_b, gf,
                                                mp[3], mp[4], wr_t, br_col, tm=512, exact=0)
        xp = _moe_sorted(x1, h2, wts, wtst, gcnt, mp[5], w_gate_e[l].astype(BF16), w_up_e[l].astype(BF16),
                         w_down_e[l].astype(BF16), g_fin, tm=1024, final_norm=last)
        kp_l.append(akt)
        vp_l.append(avt)
        rp_l.append(st.reshape(bp, RET_HEADS, RET_DK, RET_DV))

        proj = _proj_sample(xs, gm, ms[0], ms[1], w_in, l)
        cut = lambda j, n=1: proj[:, j * w:(j + n) * w]
        rq_s, rk_s, rv_s, rg_s, aq_s, ak_s, av_s = (cut(j) for j in range(7))
        gr_s, ga_s = cut(7, 2), cut(9, 2)
        new_state, ret_s = _ret_sample(
            state_ret, l,
            rq_s.reshape(bs, RET_HEADS, RET_DK, 1), rk_s.reshape(bs, RET_HEADS, RET_DK, 1),
            rv_s.reshape(bs, RET_HEADS, 1, RET_DV), rg_s.reshape(bs, RET_HEADS, 1, RET_DV))
        q_col = aq_s.reshape(bs, w, 1)
        sel_s = _paged_gate(cache_kt, l, pt_flat, q_col, n_pages)
        sel_flat = sel_s[:, :MOBA_TOPK, :ATT_HEADS].transpose(0, 2, 1).reshape(-1)
        att_s = _moba_sample(cache_kt, cache_vt, l, pt_flat, sel_flat, q_col, ak_s.reshape(bs, w, 1),
                             av_s.reshape(bs, w, 1), n_pages)
        x1s, h2s, wts_s, _, _ = _merge_router(xs, ret_s.reshape(bs, w), att_s.reshape(bs, w), gr_s, ga_s, ms[2],
                                              w_ret_o, w_att_o, w_out, gf, ms[3], ms[4], wr_t, br_col,
                                              tm=bs, exact=l + 1)
        xs = _moe(x1s, h2s, wts_s, ms[5], w_gate_e, w_up_e, w_down_e, l, g_fin, tm=bs, final_norm=last,
                  exact=True)
        ks_l.append(ak_s.reshape(bs, 1, ATT_HEADS, ATT_HD))
        vs_l.append(av_s.reshape(bs, 1, ATT_HEADS, ATT_HD))
        rs_l.append(new_state)

    def kv_out(parts):
        return jnp.stack(parts).reshape(depth, bp, ATT_HEADS, ATT_HD, s).transpose(0, 1, 4, 2, 3)

    return (xp.reshape(bp, s, d), xs.reshape(bs, 1, d), kv_out(kp_l), kv_out(vp_l), jnp.stack(rp_l),
            jnp.stack(ks_l), jnp.stack(vs_l), jnp.stack(rs_l))
```

```python
import functools

import jax
import jax.numpy as jnp
from jax import lax
from jax.experimental import pallas as pl
from jax.experimental.pallas import tpu as pltpu

EPS = 1e-6
RET_HEADS = 8
RET_DK = 64
RET_DV = 64
ATT_HEADS = 8
ATT_HD = 64
HEAD_W = RET_HEADS * RET_DK
MOBA_BLOCK = 256
MOBA_TOPK = 3
PAGE_SIZE = 128
PAGES_PER_BLOCK = MOBA_BLOCK // PAGE_SIZE
N_GROUPS = 4
EXP_PER_GROUP = 4
N_EXPERTS = N_GROUPS * EXP_PER_GROUP
SUBLANES = 8
LANES = 128
PAIRS = HEAD_W // LANES
NEG = -1e30
ALIBI_SLOPES = tuple(2.0 ** (-8.0 * (j + 1) / ATT_HEADS) for j in range(ATT_HEADS))
PAGES_PER_STEP = 16
MOE_CHUNK = 128
MOE_SORT_ROWS = 256
VMEM_LIMIT = 56 * 1024 * 1024

F32 = jnp.float32
BF16 = jnp.bfloat16
HI = lax.Precision.HIGHEST
NT = (((1,), (1,)), ((), ()))
TN = (((0,), (0,)), ((), ()))


def _dot(a, b):
    return jnp.dot(a, b, preferred_element_type=F32)


def _dot_nt(a, b):
    return lax.dot_general(a, b, NT, preferred_element_type=F32)


def _dot_hi(a, b):
    return jnp.dot(a, b, preferred_element_type=F32, precision=HI)


def _dot_nt_hi(a, b):
    return lax.dot_general(a, b, NT, preferred_element_type=F32, precision=HI)


def _mm(a, w_ref, exact):
    if exact:
        return _dot_hi(a.astype(F32), w_ref[...])
    return _dot(a.astype(BF16), w_ref[...].astype(BF16))


def _dot_split(a, b_bf16, lhs_exact=False):
    hi = a.astype(BF16)
    lo = (a - hi.astype(F32)).astype(BF16)
    if lhs_exact:
        return _dot(b_bf16, hi) + _dot(b_bf16, lo)
    return _dot(hi, b_bf16) + _dot(lo, b_bf16)


def _params(*sem):
    return pltpu.CompilerParams(dimension_semantics=sem, vmem_limit_bytes=VMEM_LIMIT)


def _full(shape):
    n = len(shape)
    return pl.BlockSpec(shape, lambda *_: (0,) * n)


def _modulate(x, g, shift, scale):
    ms = jnp.mean(x * x, axis=-1, keepdims=True)
    y = x * lax.rsqrt(ms + EPS) * g
    return y * (1.0 + scale) + shift


def _silu(x):
    return x * jax.nn.sigmoid(x)


def _head_lane_mask(shape, a):
    lane = lax.broadcasted_iota(jnp.int32, shape, len(shape) - 1)
    return (lane // RET_DK) == a


def _top3_rows(g, n_rows):
    row = lax.broadcasted_iota(jnp.int32, g.shape, 0)
    sel = jnp.zeros(g.shape, F32)
    idxs = []
    for _ in range(MOBA_TOPK):
        cmax = jnp.max(g, axis=0, keepdims=True)
        idx = jnp.min(jnp.where(g == cmax, row, n_rows), axis=0, keepdims=True)
        pick = jnp.logical_and(row == idx, cmax > -jnp.inf)
        sel = jnp.where(pick, 1.0, sel)
        g = jnp.where(pick, -jnp.inf, g)
        idxs.append(idx)
    return sel, idxs


def _ada_kernel(c_ref, w_ref, b_ref, o_ref):
    c = c_ref[...]
    o_ref[...] = _dot_hi(_silu(c), w_ref[...]) + b_ref[...]


def _ada(c_all, w_ada, b_ada):
    depth, d, d6 = w_ada.shape
    r = c_all.shape[0]
    return pl.pallas_call(
        _ada_kernel,
        grid=(depth, d6 // d),
        in_specs=[
            pl.BlockSpec((r, d), lambda l, j: (0, 0)),
            pl.BlockSpec((None, d, d), lambda l, j: (l, 0, j)),
            pl.BlockSpec((None, 1, d), lambda l, j: (l, 0, j)),
        ],
        out_specs=pl.BlockSpec((None, r, d), lambda l, j: (l, 0, j)),
        out_shape=jax.ShapeDtypeStruct((depth, r, d6), F32),
        compiler_params=_params("parallel", "parallel"),
        name="ada_mod",
    )(c_all, w_ada, b_ada.reshape(depth, 1, d6))


def _proj_prompt_kernel(x_ref, g_ref, sh_ref, sc_ref, w_ref, wkvt_ref,
                        rq_ref, rk_ref, rv_ref, rg_ref, aq_ref, akb_ref, akt_ref, avt_ref, avtb_ref,
                        gr_ref, ga_ref, qc_ref, means_ref, *, slopes):
    i = pl.program_id(0)
    nb = means_ref.shape[0]
    tm = x_ref.shape[0]
    w = HEAD_W

    @pl.when(i == 0)
    def _():
        means_ref[...] = jnp.zeros_like(means_ref)

    hb = _modulate(x_ref[...], g_ref[...], sh_ref[...], sc_ref[...]).astype(BF16)

    def col(c, n=1):
        return _dot(hb, w_ref[:, c * w:(c + n) * w])

    rq_ref[...] = col(0).astype(BF16)
    rk_ref[...] = (col(1) * RET_DK ** -0.5).astype(BF16)
    rv_ref[...] = col(2).astype(BF16)
    rg_ref[...] = col(3)
    q = col(4) * ATT_HD ** -0.5
    aq_ref[...] = q.astype(BF16)
    k = col(5)
    akb_ref[...] = k.astype(BF16)
    gr_ref[...] = col(7, 2)
    ga_ref[...] = col(9, 2)
    kvt = _dot_nt(wkvt_ref[...], hb)
    akt_ref[...] = kvt[:w]
    avt_ref[...] = kvt[w:]
    avtb_ref[...] = kvt[w:].astype(BF16)

    blk = lax.broadcasted_iota(jnp.int32, (nb, tm), 0)
    valid = blk < i
    blk_dist = (MOBA_BLOCK * (i - blk)).astype(F32)
    xrow = lax.broadcasted_iota(jnp.int32, (LANES - nb, tm), 0)
    xlane = lax.broadcasted_iota(jnp.int32, (LANES - nb, tm), 1).astype(F32)
    for p in range(PAIRS):
        mp = means_ref[:, p * LANES:(p + 1) * LANES]
        qp = q[:, p * LANES:(p + 1) * LANES]
        for a in range(2):
            slope = slopes[2 * p + a]
            qm = jnp.where(_head_lane_mask(qp.shape, a), qp, 0.0)
            gate = jnp.where(valid, _dot_nt_hi(mp, qm), -jnp.inf)
            sel, _ = _top3_rows(gate, nb)
            ct = jnp.where(sel > 0.5, -slope * blk_dist, NEG)
            extra = jnp.where(xrow == 0, -slope * xlane, jnp.where(xrow == 1, slope, 0.0))
            qc_ref[2 * p + a] = jnp.concatenate([ct, extra], axis=0).T.astype(BF16)

    means_ref[pl.ds(i, 1), :] = jnp.mean(k, axis=0, keepdims=True)


def _proj_prompt(x, g, shift, scale, w_in_b, w_kvt_b):
    s, d = x.shape
    tm = MOBA_BLOCK
    assert s % tm == 0
    nb = s // tm
    w = HEAD_W
    n_in = w_in_b.shape[1]
    row = lambda width: pl.BlockSpec((tm, width), lambda i: (i, 0))
    colblk = pl.BlockSpec((w, tm), lambda i: (0, i))
    vec = pl.BlockSpec((1, d), lambda i: (0, 0))
    out_shape = (
        jax.ShapeDtypeStruct((s, w), BF16),
        jax.ShapeDtypeStruct((s, w), BF16),
        jax.ShapeDtypeStruct((s, w), BF16),
        jax.ShapeDtypeStruct((s, w), F32),
        jax.ShapeDtypeStruct((s, w), BF16),
        jax.ShapeDtypeStruct((nb, tm, w), BF16),
        jax.ShapeDtypeStruct((w, s), F32),
        jax.ShapeDtypeStruct((w, s), F32),
        jax.ShapeDtypeStruct((nb, w, tm), BF16),
        jax.ShapeDtypeStruct((s, d), F32),
        jax.ShapeDtypeStruct((s, d), F32),
        jax.ShapeDtypeStruct((ATT_HEADS, s, LANES), BF16),
    )
    assert nb + 2 <= LANES
    out_specs = (
        row(w), row(w), row(w), row(w), row(w),
        pl.BlockSpec((None, tm, w), lambda i: (i, 0, 0)),
        colblk, colblk,
        pl.BlockSpec((None, w, tm), lambda i: (i, 0, 0)),
        row(d), row(d),
        pl.BlockSpec((ATT_HEADS, tm, LANES), lambda i: (0, i, 0)),
    )
    return pl.pallas_call(
        functools.partial(_proj_prompt_kernel, slopes=ALIBI_SLOPES),
        grid=(nb,),
        in_specs=[row(d), vec, vec, vec,
                  pl.BlockSpec((d, n_in), lambda i: (0, 0), pipeline_mode=pl.Buffered(1)),
                  pl.BlockSpec((2 * w, d), lambda i: (0, 0), pipeline_mode=pl.Buffered(1))],
        out_specs=out_specs,
        out_shape=out_shape,
        scratch_shapes=[pltpu.VMEM((nb, w), F32)],
        compiler_params=_params("arbitrary"),
        name="proj_prompt",
    )(x, g, shift, scale, w_in_b, w_kvt_b)


def _proj_sample_kernel(x_ref, g_ref, sh_ref, sc_ref, w_ref, o_ref):
    h = _modulate(x_ref[...], g_ref[...], sh_ref[...], sc_ref[...])
    o_ref[...] = _dot_hi(h, w_ref[...])


def _proj_sample(x, g, shift, scale, w_in, layer):
    b, d = x.shape
    n_in = w_in.shape[2]
    tn = HEAD_W
    return pl.pallas_call(
        _proj_sample_kernel,
        grid=(n_in // tn,),
        in_specs=[_full((b, d)), _full((1, d)), _full((b, d)), _full((b, d)),
                  pl.BlockSpec((None, d, tn), lambda j: (layer, 0, j))],
        out_specs=pl.BlockSpec((b, tn), lambda j: (0, j)),
        out_shape=jax.ShapeDtypeStruct((b, n_in), F32),
        compiler_params=_params("parallel"),
        name="proj_sample",
    )(x, g, shift, scale, w_in)


def _retention_kernel(q_ref, k_ref, v_ref, rg_ref, decay_ref, xi_ref, zeta_ref, gc_ref, avg_ref,
                      o_ref, st_ref):
    i = pl.program_id(0)

    @pl.when(i == 0)
    def _():
        st_ref[...] = jnp.zeros_like(st_ref)

    c = q_ref.shape[0]
    row = lax.broadcasted_iota(jnp.int32, (LANES, LANES), 0)
    lane = lax.broadcasted_iota(jnp.int32, (LANES, LANES), 1)
    same_head = (row // RET_DK) == (lane // RET_DV)
    avg = avg_ref[...]
    for p in range(PAIRS):
        sl = slice(p * LANES, (p + 1) * LANES)
        q = q_ref[:, sl]
        k = k_ref[:, sl]
        v = v_ref[:, sl]
        state = st_ref[p]
        cross = _dot(q, state.astype(BF16)) * xi_ref[:, sl]
        inner = []
        for a in range(2):
            qm = jnp.where(_head_lane_mask(q.shape, a), q, jnp.zeros_like(q))
            scores = _dot_nt(qm, k) * decay_ref[2 * p + a]
            inner.append(_dot(scores.astype(BF16), v))
        o = jnp.where(_head_lane_mask((c, LANES), 0), inner[0], inner[1]) + cross
        kz = (k.astype(F32) * zeta_ref[:, sl]).astype(BF16)
        upd = lax.dot_general(kz, v, TN, preferred_element_type=F32)
        st_ref[p] = gc_ref[:, sl] * state + jnp.where(same_head, upd, 0.0)
        mu = _dot_split(o, avg)
        dlt = o - mu
        var = _dot_split(dlt * dlt, avg)
        hn = dlt * lax.rsqrt(var + EPS)
        o_ref[:, sl] = (_silu(rg_ref[:, sl]) * hn).astype(BF16)


def _ret_tables(c):
    h = jnp.arange(RET_HEADS, dtype=F32)
    log_gamma = jnp.log1p(-(2.0 ** (-5.0 - h)))
    idx = jnp.arange(c, dtype=F32)
    diff = idx[:, None] - idx[None, :]
    decay = jnp.where(diff >= 0, jnp.exp(log_gamma[:, None, None] * jnp.maximum(diff, 0.0)), 0.0)
    xi = jnp.exp(log_gamma[None, :] * (idx[:, None] + 1.0))
    zeta = jnp.exp(log_gamma[None, :] * (c - 1.0 - idx[:, None]))
    gc = jnp.exp(log_gamma * c)[None, :]
    rep = lambda t: jnp.repeat(t, RET_DV, axis=1)
    return decay, rep(xi), rep(zeta), rep(gc)


def _head_avg_matrix():
    r = jnp.arange(LANES)
    return jnp.where((r[:, None] // RET_DV) == (r[None, :] // RET_DV), 1.0 / RET_DV, 0.0).astype(BF16)


def _retention_prompt(rq, rk, rv, rg, chunk):
    s, w = rq.shape
    assert s % chunk == 0
    decay, xi, zeta, gc = _ret_tables(chunk)
    row = pl.BlockSpec((chunk, w), lambda i: (i, 0))
    o, st = pl.pallas_call(
        _retention_kernel,
        grid=(s // chunk,),
        in_specs=[row, row, row, row,
                  _full((RET_HEADS, chunk, chunk)), _full((chunk, w)), _full((chunk, w)), _full((1, w)),
                  _full((LANES, LANES))],
        out_specs=(row, _full((PAIRS, LANES, LANES))),
        out_shape=(jax.ShapeDtypeStruct((s, w), BF16), jax.ShapeDtypeStruct((PAIRS, LANES, LANES), F32)),
        compiler_params=_params("arbitrary"),
        name="retention_prompt",
    )(rq, rk, rv, rg, decay, xi, zeta, gc, _head_avg_matrix())
    st = st.reshape(PAIRS, 2, RET_DK, 2, RET_DV)
    st = jnp.stack([st[:, 0, :, 0, :], st[:, 1, :, 1, :]], axis=1).reshape(RET_HEADS, RET_DK, RET_DV)
    return o, st


def _moba_prompt_kernel(q_ref, qc_ref, k_ref, vt_ref, ctab_ref, cown_ref, o_ref,
                        qcat_ref, sa_ref, sb_ref, ma_ref, mb_ref, m_ref, l_ref, acc_ref):
    i = pl.program_id(0)
    tq = q_ref.shape[0]
    tk = k_ref.shape[1]
    nb = k_ref.shape[0]
    heads = range(ATT_HEADS)
    hd = ATT_HD

    for p in range(PAIRS):
        qp = q_ref[:, p * LANES:(p + 1) * LANES]
        for a in range(2):
            h = 2 * p + a
            qm = jnp.where(_head_lane_mask(qp.shape, a), qp, jnp.zeros_like(qp))
            qcat_ref[h] = jnp.concatenate([qm, qc_ref[h]], axis=1)

    ones = jnp.ones((2 * SUBLANES, tk), BF16)

    def scores(n, h, s_ref, mx_ref):
        p = h // 2
        kcat = jnp.concatenate([k_ref[n, :, p * LANES:(p + 1) * LANES], ctab_ref[n]], axis=1)
        s = _dot_nt(kcat, qcat_ref[h])
        s_ref[h] = s
        mx_ref[h:h + 1, :] = jnp.max(s, axis=0, keepdims=True)

    def attend(n, h, s_ref, mx_ref):
        s = s_ref[h]
        m = m_ref[h:h + 1, :]
        m_new = jnp.maximum(m, mx_ref[h:h + 1, :])
        alpha = jnp.exp(m - m_new)
        pr = jnp.exp((s - m_new).astype(BF16))
        rows = slice(h * hd, (h + 1) * hd)
        pv = _dot(jnp.concatenate([vt_ref[n, rows, :], ones], axis=0), pr)
        m_ref[h:h + 1, :] = m_new
        l_ref[h:h + 1, :] = alpha * l_ref[h:h + 1, :] + pv[hd:hd + 1, :]
        acc_ref[rows, :] = alpha * acc_ref[rows, :] + pv[:hd, :]

    kk = lax.broadcasted_iota(jnp.int32, (tk, tq), 0)
    qq = lax.broadcasted_iota(jnp.int32, (tk, tq), 1)
    causal = kk <= qq
    m_ref[...] = jnp.full(m_ref.shape, NEG, F32)
    l_ref[...] = jnp.zeros_like(l_ref)
    acc_ref[...] = jnp.zeros_like(acc_ref)
    for h in heads:
        p = h // 2
        kcat = jnp.concatenate([k_ref[i, :, p * LANES:(p + 1) * LANES], cown_ref[...]], axis=1)
        s = jnp.where(causal, _dot_nt(kcat, qcat_ref[h]), NEG)
        sb_ref[h] = s
        mb_ref[h:h + 1, :] = jnp.max(s, axis=0, keepdims=True)
        scores(0, h, sa_ref, ma_ref)
    for h in heads:
        attend(i, h, sb_ref, mb_ref)

    @pl.loop(0, (i + 1) // 2)
    def _(t):
        nxt = jnp.minimum(2 * t + 2, nb - 1)
        for h in heads:
            scores(2 * t + 1, h, sb_ref, mb_ref)
            attend(2 * t, h, sa_ref, ma_ref)
        for h in heads:
            scores(nxt, h, sa_ref, ma_ref)
            attend(2 * t + 1, h, sb_ref, mb_ref)

    for h in heads:
        rows = slice(h * hd, (h + 1) * hd)
        acc_ref[rows, :] = acc_ref[rows, :] / l_ref[h:h + 1, :]
    o_ref[...] = acc_ref[...].T.astype(BF16)


def _moba_bias_table(nb, tk):
    lane = jnp.arange(LANES)[None, None, :]
    blk = jnp.arange(nb)[:, None, None]
    kk = jnp.arange(tk, dtype=F32)[None, :, None]
    t = jnp.where(lane == blk, 1.0, 0.0)
    t = jnp.where(lane == nb, 1.0, t)
    return jnp.where(lane == nb + 1, kk, t).astype(BF16)


def _moba_prompt(aq, qc, akb, avtb):
    s, w = aq.shape
    nb, tk, _ = akb.shape
    tq = MOBA_BLOCK
    assert nb % 2 == 0 and tk == tq
    ctab = _moba_bias_table(nb, tk)
    cown = jnp.where(jnp.arange(LANES)[None, :] < nb, 0, ctab[0])
    once = dict(pipeline_mode=pl.Buffered(1))
    return pl.pallas_call(
        _moba_prompt_kernel,
        grid=(s // tq,),
        in_specs=[
            pl.BlockSpec((tq, w), lambda i: (i, 0)),
            pl.BlockSpec((ATT_HEADS, tq, LANES), lambda i: (0, i, 0)),
            pl.BlockSpec((nb, tk, w), lambda i: (0, 0, 0), **once),
            pl.BlockSpec((nb, w, tk), lambda i: (0, 0, 0), **once),
            pl.BlockSpec((nb, tk, LANES), lambda i: (0, 0, 0), **once),
            pl.BlockSpec((tk, LANES), lambda i: (0, 0), **once),
        ],
        out_specs=pl.BlockSpec((tq, w), lambda i: (i, 0)),
        out_shape=jax.ShapeDtypeStruct((s, w), BF16),
        scratch_shapes=[
            pltpu.VMEM((ATT_HEADS, tq, 2 * LANES), BF16),
            pltpu.VMEM((ATT_HEADS, tk, tq), F32),
            pltpu.VMEM((ATT_HEADS, tk, tq), F32),
            pltpu.VMEM((ATT_HEADS, tq), F32),
            pltpu.VMEM((ATT_HEADS, tq), F32),
            pltpu.VMEM((ATT_HEADS, tq), F32),
            pltpu.VMEM((ATT_HEADS, tq), F32),
            pltpu.VMEM((w, tq), F32),
        ],
        compiler_params=_params("arbitrary"),
        name="moba_prompt",
    )(aq, qc, akb, avtb, ctab, cown)


def _merge_router_kernel(x_ref, ret_ref, att_ref, gr_ref, ga_ref, g1_ref, wro_ref, wao_ref, wo_ref,
                         gf_ref, sh_ref, sc_ref, wr_ref, br_ref,
                         x1_ref, h2_ref, wts_ref, wtst_ref, gcnt_ref, *, exact):
    ret = _mm(ret_ref[...], wro_ref, exact)
    att = _mm(att_ref[...], wao_ref, exact)
    merged = jax.nn.sigmoid(gr_ref[...]) * ret + jax.nn.sigmoid(ga_ref[...]) * att
    x1 = x_ref[...] + g1_ref[...] * _mm(merged, wo_ref, exact)
    x1_ref[...] = x1
    h2 = _modulate(x1, gf_ref[...], sh_ref[...], sc_ref[...])
    h2_ref[...] = h2.astype(h2_ref.dtype)

    logits = _dot_nt_hi(wr_ref[...], h2) + br_ref[...]
    lg = [logits[g:g + 1, :] for g in range(N_GROUPS)]
    le = [logits[N_GROUPS + e:N_GROUPS + e + 1, :] for e in range(N_EXPERTS)]
    mg = functools.reduce(jnp.maximum, lg)
    eg = [jnp.exp(v - mg) for v in lg]
    den = functools.reduce(jnp.add, eg)
    pg = [v / den for v in eg]
    p_top = functools.reduce(jnp.maximum, pg)
    taken = jnp.zeros(p_top.shape, jnp.bool_)
    onehot = []
    for g in range(N_GROUPS):
        hit = jnp.logical_and(pg[g] == p_top, jnp.logical_not(taken))
        onehot.append(hit)
        taken = jnp.logical_or(taken, hit)
    vin = []
    for j in range(EXP_PER_GROUP):
        v = jnp.zeros_like(p_top)
        for g in range(N_GROUPS):
            v = jnp.where(onehot[g], le[g * EXP_PER_GROUP + j], v)
        vin.append(v)
    rank = []
    for j in range(EXP_PER_GROUP):
        r = jnp.zeros(p_top.shape, jnp.int32)
        for kx in range(EXP_PER_GROUP):
            if kx == j:
                continue
            ahead = (vin[kx] > vin[j]) | ((vin[kx] == vin[j]) & (kx < j))
            r = r + ahead.astype(jnp.int32)
        rank.append(r)
    v1 = functools.reduce(jnp.maximum, vin)
    v2 = jnp.full_like(v1, -jnp.inf)
    for j in range(EXP_PER_GROUP):
        v2 = jnp.where(rank[j] == 1, vin[j], v2)
    e2 = jnp.exp(v2 - v1)
    w_first = 1.0 / (1.0 + e2)
    w_second = e2 / (1.0 + e2)
    wtst_ref[...] = jnp.zeros_like(wtst_ref)
    row8 = lax.broadcasted_iota(jnp.int32, gcnt_ref.shape, 0)
    lane8 = lax.broadcasted_iota(jnp.int32, gcnt_ref.shape, 1)
    gcnt = jnp.zeros(gcnt_ref.shape, F32)
    for g in range(N_GROUPS):
        hot = jnp.where(onehot[g], 1.0, 0.0)
        wtst_ref[N_EXPERTS + g:N_EXPERTS + g + 1, :] = hot
        n_g = jnp.sum(hot, axis=1, keepdims=True)
        gcnt = jnp.where(jnp.logical_and(row8 == 0, lane8 == g), n_g, gcnt)
        for j in range(EXP_PER_GROUP):
            wj = jnp.where(rank[j] == 0, w_first, jnp.where(rank[j] == 1, w_second, 0.0)) * p_top
            e = g * EXP_PER_GROUP + j
            wtst_ref[e:e + 1, :] = jnp.where(onehot[g], wj, 0.0)
    wts_ref[...] = wtst_ref[...].T
    gcnt_ref[...] = gcnt.astype(jnp.int32)


def _merge_router(x, ret_in, att_in, gr, ga, gate1, wro, wao, wo, g_ffn, shift2, scale2, wr_t, br_col, tm, exact):
    s, d = x.shape
    w = HEAD_W
    assert s % tm == 0
    row = lambda width: pl.BlockSpec((tm, width), lambda i: (i, 0))
    mod = lambda a: pl.BlockSpec((tm if a.shape[0] > 1 else 1, d), lambda i: (i if a.shape[0] > 1 else 0, 0))
    if exact:
        layer = exact - 1
        wspec = lambda a: pl.BlockSpec((None,) + a.shape[1:], lambda i: (layer, 0, 0))
    else:
        wspec = lambda a: _full(a.shape)
    rr = wr_t.shape[0]
    return pl.pallas_call(
        functools.partial(_merge_router_kernel, exact=bool(exact)),
        grid=(s // tm,),
        in_specs=[row(d), row(w), row(w), row(d), row(d), mod(gate1),
                  wspec(wro), wspec(wao), wspec(wo),
                  _full((1, d)), mod(shift2), mod(scale2), _full((rr, d)), _full((rr, 1))],
        out_specs=(row(d), row(d), row(LANES),
                   pl.BlockSpec((LANES, tm), lambda i: (0, i)),
                   pl.BlockSpec((None, SUBLANES, LANES), lambda i: (i, 0, 0))),
        out_shape=(jax.ShapeDtypeStruct((s, d), F32), jax.ShapeDtypeStruct((s, d), F32 if exact else BF16),
                   jax.ShapeDtypeStruct((s, LANES), F32),
                   jax.ShapeDtypeStruct((LANES, s), F32),
                   jax.ShapeDtypeStruct((s // tm, SUBLANES, LANES), jnp.int32)),
        compiler_params=_params("parallel"),
        name="merge_router",
    )(x, ret_in, att_in, gr, ga, gate1, wro, wao, wo, g_ffn, shift2, scale2, wr_t, br_col)


def _moe_kernel(x1_ref, h2_ref, wts_ref, g2_ref, wg_ref, wu_ref, wd_ref, gfin_ref, o_ref, acc_ref,
                *, final_norm, exact):
    e = pl.program_id(1)

    @pl.when(e == 0)
    def _():
        acc_ref[...] = jnp.zeros_like(acc_ref)

    h2 = h2_ref[...]
    wts = wts_ref[...]
    lane = lax.broadcasted_iota(jnp.int32, wts.shape, 1)
    wcol = jnp.sum(jnp.where(lane == e, wts, 0.0), axis=1, keepdims=True)
    act = _silu(_mm(h2, wg_ref, exact)) * _mm(h2, wu_ref, exact) * wcol
    acc_ref[...] += _mm(act, wd_ref, exact)

    @pl.when(e == pl.num_programs(1) - 1)
    def _():
        x2 = x1_ref[...] + g2_ref[...] * acc_ref[...]
        if final_norm:
            ms = jnp.mean(x2 * x2, axis=-1, keepdims=True)
            x2 = x2 * lax.rsqrt(ms + EPS) * gfin_ref[...]
        o_ref[...] = x2


def _moe(x1, h2, wts, gate2, wg, wu, wd, layer, g_final, tm, final_norm, exact):
    s, d = x1.shape
    ne, f = wg.shape[-3], wg.shape[-1]
    assert s % tm == 0
    row = lambda width: pl.BlockSpec((tm, width), lambda i, e: (i, 0))
    mod = lambda a: pl.BlockSpec((tm if a.shape[0] > 1 else 1, d), lambda i, e: (i if a.shape[0] > 1 else 0, 0))
    wspec = lambda r, c: pl.BlockSpec((None, None, r, c), lambda i, e: (layer, e, 0, 0))
    return pl.pallas_call(
        functools.partial(_moe_kernel, final_norm=final_norm, exact=exact),
        grid=(s // tm, ne),
        in_specs=[row(d), row(d), row(LANES), mod(gate2),
                  wspec(d, f), wspec(d, f), wspec(f, d),
                  pl.BlockSpec((1, d), lambda i, e: (0, 0))],
        out_specs=row(d),
        out_shape=jax.ShapeDtypeStruct((s, d), F32),
        scratch_shapes=[pltpu.VMEM((tm, d), F32)],
        compiler_params=_params("parallel", "arbitrary"),
        name="moe_experts",
    )(x1, h2, wts, gate2, wg, wu, wd, g_final)


def _moe_sort(starts, h2_ref, wts_ref, wtst_ref, p_scr, pt_scr, hs_scr, ws_scr, acc_scr):
    tm = h2_ref.shape[0]
    ch = MOE_SORT_ROWS
    lo, hi = N_EXPERTS, N_EXPERTS + N_GROUPS
    wts = wts_ref[...]
    lane_t = lax.broadcasted_iota(jnp.int32, wts.shape, 1)
    hot_t = jnp.where(jnp.logical_and(lane_t >= lo, lane_t < hi), wts, 0.0)
    wtst = wtst_ref[...]
    row_l = lax.broadcasted_iota(jnp.int32, wtst.shape, 0)
    hot_l = jnp.where(jnp.logical_and(row_l >= lo, row_l < hi), wtst, 0.0)
    start_t = jnp.zeros((1, LANES), F32)
    start_l = jnp.zeros((LANES, 1), F32)
    lane1 = lax.broadcasted_iota(jnp.int32, (1, LANES), 1)
    row1 = lax.broadcasted_iota(jnp.int32, (LANES, 1), 0)
    for g in range(N_GROUPS):
        sg = starts[g].astype(F32)
        start_t = jnp.where(lane1 == lo + g, sg, start_t)
        start_l = jnp.where(row1 == lo + g, sg, start_l)
    hot_t_b = hot_t.astype(BF16)
    hot_l_b = hot_l.astype(BF16)
    for c in range(tm // ch):
        sl = slice(c * ch, (c + 1) * ch)
        tok = lax.broadcasted_iota(jnp.int32, (ch, tm), 0) + c * ch
        other = lax.broadcasted_iota(jnp.int32, (ch, tm), 1)
        earlier = jnp.where(other < tok, 1.0, 0.0).astype(BF16)
        rank = _dot(earlier, hot_t_b)
        pos = jnp.sum(hot_t[sl] * (rank + start_t), axis=1, keepdims=True).astype(jnp.int32)
        pt_scr[sl, :] = jnp.where(other == pos, 1.0, 0.0).astype(BF16)
        tok = lax.broadcasted_iota(jnp.int32, (tm, ch), 1) + c * ch
        other = lax.broadcasted_iota(jnp.int32, (tm, ch), 0)
        earlier = jnp.where(other < tok, 1.0, 0.0).astype(BF16)
        rank = _dot(hot_l_b, earlier)
        pos = jnp.sum(hot_l[:, sl] * (rank + start_l), axis=0, keepdims=True).astype(jnp.int32)
        p_scr[:, sl] = jnp.where(other == pos, 1.0, 0.0).astype(BF16)
    p = p_scr[...]
    hs_scr[...] = _dot(p, h2_ref[...]).astype(BF16)
    ws_scr[...] = _dot_split(wts, p, lhs_exact=True)
    acc_scr[...] = jnp.zeros_like(acc_scr)


def _moe_sorted_kernel(cnt_ref, *refs, final_norm, router_tiles):
    _moe_sorted_body(cnt_ref, *refs, final_norm=final_norm, router_tiles=router_tiles)


def _moe_sorted_paged_kernel(cnt_ref, ptab_ref, *refs, final_norm, router_tiles, paged_steps, steps_per_seq):
    n_moe_in = 9
    moe_in, q_ref = refs[:n_moe_in], refs[n_moe_in]
    pages = refs[n_moe_in + 1:n_moe_in + 1 + PAGES_PER_STEP]
    o_ref, sel_ref, p_scr, pt_scr, hs_scr, ws_scr, acc_scr, qb_ref, gacc_ref = refs[n_moe_in + 1 + PAGES_PER_STEP:]
    _moe_sorted_body(cnt_ref, *moe_in, o_ref, p_scr, pt_scr, hs_scr, ws_scr, acc_scr,
                     final_norm=final_norm, router_tiles=router_tiles)
    step = pl.program_id(0) * pl.num_programs(1) + pl.program_id(1)

    @pl.when(step < paged_steps)
    def _():
        _paged_gate_step(step % steps_per_seq, steps_per_seq - 1, q_ref, pages, sel_ref, qb_ref, gacc_ref)


def _moe_sorted_body(cnt_ref, x1_ref, h2_ref, wts_ref, wtst_ref, g2_ref, wg_ref, wu_ref, wd_ref, gfin_ref,
                     o_ref, p_scr, pt_scr, hs_scr, ws_scr, acc_scr, *, final_norm, router_tiles):
    i = pl.program_id(0)
    e = pl.program_id(1)
    ch = MOE_CHUNK
    counts = []
    for g in range(N_GROUPS):
        n = cnt_ref[i * router_tiles * N_GROUPS + g]
        for j in range(1, router_tiles):
            n = n + cnt_ref[(i * router_tiles + j) * N_GROUPS + g]
        counts.append(n)
    starts = [jnp.int32(0)]
    for g in range(N_GROUPS - 1):
        starts.append(starts[-1] + counts[g])

    @pl.when(e == 0)
    def _():
        _moe_sort(starts, h2_ref, wts_ref, wtst_ref, p_scr, pt_scr, hs_scr, ws_scr, acc_scr)

    group = e // EXP_PER_GROUP
    seg0, seg_n = starts[0], counts[0]
    for g in range(1, N_GROUPS):
        seg0 = jnp.where(group == g, starts[g], seg0)
        seg_n = jnp.where(group == g, counts[g], seg_n)
    lane = lax.broadcasted_iota(jnp.int32, (ch, LANES), 1)

    def chunk(c, carry):
        rows = pl.ds(pl.multiple_of(c * ch, ch), ch)
        hs = hs_scr[rows, :]
        wcol = jnp.sum(jnp.where(lane == e, ws_scr[rows, :], 0.0), axis=1, keepdims=True)
        act = _silu(_dot(hs, wg_ref[...])) * _dot(hs, wu_ref[...]) * wcol
        acc_scr[rows, :] += _dot(act.astype(BF16), wd_ref[...])
        return carry

    lax.fori_loop(seg0 // ch, (seg0 + seg_n + ch - 1) // ch, chunk, 0)

    @pl.when(e == pl.num_programs(1) - 1)
    def _():
        y = _dot(pt_scr[...], acc_scr[...].astype(BF16))
        x2 = x1_ref[...] + g2_ref[...] * y
        if final_norm:
            ms = jnp.mean(x2 * x2, axis=-1, keepdims=True)
            x2 = x2 * lax.rsqrt(ms + EPS) * gfin_ref[...]
        o_ref[...] = x2


def _moe_sorted(x1, h2, wts, wtst, gcnt, gate2, wg_b, wu_b, wd_b, g_final, tm, final_norm, paged=None):
    s, d = x1.shape
    ne, _, f = wg_b.shape
    router_tiles = gcnt.shape[0] * tm // s
    assert s % tm == 0 and tm % MOE_CHUNK == 0 and tm % MOE_SORT_ROWS == 0 and router_tiles >= 1
    cnt_flat = gcnt[:, 0, :N_GROUPS].reshape(-1)
    n_steps = (s // tm) * ne
    fuse = False
    if paged is not None:
        cache_kt, layer, page_table_flat, q_col, n_pages = paged
        assert n_pages % PAGES_PER_STEP == 0 and PAGES_PER_STEP % PAGES_PER_BLOCK == 0
        steps_per_seq = n_pages // PAGES_PER_STEP
        paged_steps = q_col.shape[0] * steps_per_seq
        n_blocks = n_pages // PAGES_PER_BLOCK
        assert n_blocks >= MOBA_TOPK
        fuse = paged_steps <= n_steps
    n_pre = 2 if fuse else 1
    row = lambda width: pl.BlockSpec((tm, width), lambda i, e, *_: (i, 0))
    wspec = lambda r, c: pl.BlockSpec((None, r, c), lambda i, e, *_: (e, 0, 0))
    vec = pl.BlockSpec((1, d), lambda i, e, *_: (0, 0))
    in_specs = [row(d), row(d), row(LANES), pl.BlockSpec((LANES, tm), lambda i, e, *_: (0, i)), vec,
                wspec(d, f), wspec(d, f), wspec(f, d), vec]
    out_specs = [row(d)]
    out_shape = [jax.ShapeDtypeStruct((s, d), F32)]
    scratch = [pltpu.VMEM((tm, tm), BF16),
               pltpu.VMEM((tm, tm), BF16),
               pltpu.VMEM((tm, d), BF16),
               pltpu.VMEM((tm, LANES), F32),
               pltpu.VMEM((tm, d), F32)]
    operands = [cnt_flat, x1, h2, wts, wtst, gate2, wg_b, wu_b, wd_b, g_final]
    body = functools.partial(_moe_sorted_kernel, final_norm=final_norm, router_tiles=router_tiles)
    if fuse:
        def seq_step(i, e):
            st = jnp.minimum(i * ne + e, paged_steps - 1)
            return st // steps_per_seq, st % steps_per_seq

        def page_spec(g):
            def imap(i, e, cnt, pt):
                bi, j = seq_step(i, e)
                return (layer, pt[bi * n_pages + j * PAGES_PER_STEP + g], 0, 0)
            return pl.BlockSpec((None, None, HEAD_W, PAGE_SIZE), imap)

        in_specs += [pl.BlockSpec((None, HEAD_W, 1), lambda i, e, cnt, pt: (seq_step(i, e)[0], 0, 0))]
        in_specs += [page_spec(g) for g in range(PAGES_PER_STEP)]
        out_specs += [pl.BlockSpec((None, SUBLANES, LANES), lambda i, e, cnt, pt: (seq_step(i, e)[0], 0, 0))]
        out_shape += [jax.ShapeDtypeStruct((q_col.shape[0], SUBLANES, LANES), jnp.int32)]
        scratch += [pltpu.VMEM((HEAD_W, PAGE_SIZE), F32),
                    pltpu.VMEM((n_blocks * ATT_HEADS * SUBLANES, PAGE_SIZE), F32)]
        operands = [cnt_flat, page_table_flat] + operands[1:] + [q_col] + [cache_kt] * PAGES_PER_STEP
        body = functools.partial(_moe_sorted_paged_kernel, final_norm=final_norm, router_tiles=router_tiles,
                                 paged_steps=paged_steps, steps_per_seq=steps_per_seq)
    grid_spec = pltpu.PrefetchScalarGridSpec(
        num_scalar_prefetch=n_pre, grid=(s // tm, ne), in_specs=in_specs, out_specs=out_specs,
        scratch_shapes=scratch)
    outs = pl.pallas_call(
        body,
        grid_spec=grid_spec,
        out_shape=out_shape,
        compiler_params=_params("arbitrary", "arbitrary"),
        name="moe_sorted",
    )(*operands)
    if fuse:
        return outs[0], outs[1]
    sel = None if paged is None else _paged_gate(cache_kt, layer, page_table_flat, q_col, n_pages)
    return outs[0], sel


def _paged_gate_kernel(pt_ref, q_ref, *rest):
    pages = rest[:PAGES_PER_STEP]
    sel_ref, qb_ref, acc_ref = rest[PAGES_PER_STEP:]
    _paged_gate_step(pl.program_id(1), pl.num_programs(1) - 1, q_ref, pages, sel_ref, qb_ref, acc_ref)


def _paged_gate_step(j, last_j, q_ref, pages, sel_ref, qb_ref, acc_ref):
    rows_per_block = ATT_HEADS * SUBLANES

    @pl.when(j == 0)
    def _():
        qb_ref[...] = jnp.broadcast_to(q_ref[...] * ATT_HD ** -0.5, qb_ref.shape)

    qb = qb_ref[...]
    blocks_per_step = PAGES_PER_STEP // PAGES_PER_BLOCK
    for t in range(blocks_per_step):
        tot = pages[PAGES_PER_BLOCK * t][...]
        for u in range(1, PAGES_PER_BLOCK):
            tot = tot + pages[PAGES_PER_BLOCK * t + u][...]
        part = (tot * qb).reshape(ATT_HEADS, ATT_HD // SUBLANES, SUBLANES, PAGE_SIZE).sum(axis=1)
        base = pl.multiple_of((j * blocks_per_step + t) * rows_per_block, rows_per_block)
        acc_ref[pl.ds(base, rows_per_block), :] = part.reshape(rows_per_block, PAGE_SIZE)

    @pl.when(j == last_j)
    def _():
        n_rows = acc_ref.shape[0]
        nb = n_rows // rows_per_block
        g = jnp.sum(acc_ref[...], axis=1, keepdims=True)
        r = lax.broadcasted_iota(jnp.int32, (n_rows, LANES), 0)
        lane = lax.broadcasted_iota(jnp.int32, (n_rows, LANES), 1)
        spread = jnp.where(lane == (r // SUBLANES) % ATT_HEADS, g, 0.0)
        gate = spread.reshape(nb, rows_per_block, LANES).sum(axis=1) * (1.0 / MOBA_BLOCK)
        _, idxs = _top3_rows(gate, nb)
        out = jnp.zeros(sel_ref.shape, jnp.int32)
        rr = lax.broadcasted_iota(jnp.int32, sel_ref.shape, 0)
        for t, idx in enumerate(idxs):
            out = jnp.where(rr == t, idx, out)
        sel_ref[...] = out


def _paged_gate(cache_kt, layer, page_table_flat, q_col, n_pages):
    b = q_col.shape[0]
    assert n_pages % PAGES_PER_STEP == 0 and PAGES_PER_STEP % PAGES_PER_BLOCK == 0
    nb = n_pages // PAGES_PER_BLOCK
    assert nb >= MOBA_TOPK

    def page_spec(g):
        return pl.BlockSpec((None, None, HEAD_W, PAGE_SIZE),
                            lambda bi, j, pt: (layer, pt[bi * n_pages + j * PAGES_PER_STEP + g], 0, 0))

    grid_spec = pltpu.PrefetchScalarGridSpec(
        num_scalar_prefetch=1,
        grid=(b, n_pages // PAGES_PER_STEP),
        in_specs=[pl.BlockSpec((None, HEAD_W, 1), lambda bi, j, pt: (bi, 0, 0))]
                 + [page_spec(g) for g in range(PAGES_PER_STEP)],
        out_specs=pl.BlockSpec((None, SUBLANES, LANES), lambda bi, j, pt: (bi, 0, 0)),
        scratch_shapes=[pltpu.VMEM((HEAD_W, PAGE_SIZE), F32),
                        pltpu.VMEM((nb * ATT_HEADS * SUBLANES, PAGE_SIZE), F32)],
    )
    return pl.pallas_call(
        _paged_gate_kernel,
        grid_spec=grid_spec,
        out_shape=jax.ShapeDtypeStruct((b, SUBLANES, LANES), jnp.int32),
        compiler_params=_params("parallel", "arbitrary"),
        name="paged_gate",
    )(page_table_flat, q_col, *([cache_kt] * PAGES_PER_STEP))


def _moba_sample_kernel(pt_ref, sel_ref, slopes_ref, q_ref, kn_ref, vn_ref, *rest, past_len):
    npg = MOBA_TOPK * PAGES_PER_BLOCK
    kp = rest[:npg]
    vp = rest[npg:2 * npg]
    o_ref = rest[2 * npg]
    b = pl.program_id(0)
    h = pl.program_id(1)
    slope = slopes_ref[h]
    lane = lax.broadcasted_iota(jnp.int32, (1, PAGE_SIZE), 1)
    q = q_ref[...] * ATT_HD ** -0.5
    scores = []
    for s in range(MOBA_TOPK):
        blk = sel_ref[(b * ATT_HEADS + h) * MOBA_TOPK + s]
        for t in range(PAGES_PER_BLOCK):
            kpos = blk * MOBA_BLOCK + t * PAGE_SIZE + lane
            dist = (past_len - kpos).astype(F32)
            sc = jnp.sum(kp[s * PAGES_PER_BLOCK + t][...] * q, axis=0, keepdims=True)
            scores.append(sc - slope * dist)
    s_own = jnp.sum(q * kn_ref[...], axis=0, keepdims=True)
    m = functools.reduce(jnp.maximum, [jnp.max(sc, axis=1, keepdims=True) for sc in scores] + [s_own])
    p_own = jnp.exp(s_own - m)
    l = p_own
    acc = p_own * vn_ref[...]
    for idx, sc in enumerate(scores):
        pr = jnp.exp(sc - m)
        l = l + jnp.sum(pr, axis=1, keepdims=True)
        acc = acc + jnp.sum(vp[idx][...] * pr, axis=1, keepdims=True)
    o_ref[...] = acc / l


def _moba_sample(cache_kt, cache_vt, layer, page_table_flat, sel_flat, q_col, k_col, v_col, n_pages):
    b = q_col.shape[0]
    npg = MOBA_TOPK * PAGES_PER_BLOCK
    h = jnp.arange(1, ATT_HEADS + 1, dtype=F32)
    slopes = 2.0 ** (-8.0 * h / ATT_HEADS)

    def page_spec(g):
        s, t = divmod(g, PAGES_PER_BLOCK)

        def imap(bi, hi, pt, sel):
            blk = sel[(bi * ATT_HEADS + hi) * MOBA_TOPK + s]
            return (layer, pt[bi * n_pages + blk * PAGES_PER_BLOCK + t], hi, 0)

        return pl.BlockSpec((None, None, ATT_HD, PAGE_SIZE), imap)

    vec = pl.BlockSpec((None, ATT_HD, 1), lambda bi, hi, pt, sel: (bi, hi, 0))
    grid_spec = pltpu.PrefetchScalarGridSpec(
        num_scalar_prefetch=2,
        grid=(b, ATT_HEADS),
        in_specs=[pl.BlockSpec(memory_space=pltpu.SMEM), vec, vec, vec]
                 + [page_spec(g) for g in range(npg)] * 2,
        out_specs=vec,
    )
    return pl.pallas_call(
        functools.partial(_moba_sample_kernel, past_len=n_pages * PAGE_SIZE),
        grid_spec=grid_spec,
        out_shape=jax.ShapeDtypeStruct((b, HEAD_W, 1), F32),
        compiler_params=_params("parallel", "arbitrary"),
        name="moba_sample",
    )(page_table_flat, sel_flat, slopes, q_col, k_col, v_col, *([cache_kt] * npg), *([cache_vt] * npg))


def _ret_sample_kernel(st_ref, q_ref, k_ref, v_ref, rg_ref, gamma_ref, ns_ref, o_ref):
    new = gamma_ref[...] * st_ref[...] + (k_ref[...] * RET_DK ** -0.5) * v_ref[...]
    ns_ref[...] = new
    o = jnp.sum(q_ref[...] * new, axis=1, keepdims=True)
    mu = jnp.mean(o, axis=-1, keepdims=True)
    dlt = o - mu
    var = jnp.mean(dlt * dlt, axis=-1, keepdims=True)
    o_ref[...] = _silu(rg_ref[...]) * (dlt * lax.rsqrt(var + EPS))


def _ret_sample(state, layer, rq, rk, rv, rg):
    b = state.shape[1]
    hh = jnp.arange(RET_HEADS, dtype=F32)
    gamma = jnp.exp(jnp.log1p(-(2.0 ** (-5.0 - hh)))).reshape(RET_HEADS, 1, 1)
    st_in = pl.BlockSpec((None, None, RET_HEADS, RET_DK, RET_DV), lambda i: (layer, i, 0, 0, 0))
    st_out = pl.BlockSpec((None, RET_HEADS, RET_DK, RET_DV), lambda i: (i, 0, 0, 0))
    col = pl.BlockSpec((None, RET_HEADS, RET_DK, 1), lambda i: (i, 0, 0, 0))
    rw = pl.BlockSpec((None, RET_HEADS, 1, RET_DV), lambda i: (i, 0, 0, 0))
    return pl.pallas_call(
        _ret_sample_kernel,
        grid=(b,),
        in_specs=[st_in, col, col, rw, rw, _full((RET_HEADS, 1, 1))],
        out_specs=(st_out, rw),
        out_shape=(jax.ShapeDtypeStruct(state.shape[1:], F32),
                   jax.ShapeDtypeStruct((b, RET_HEADS, 1, RET_DV), F32)),
        compiler_params=_params("parallel"),
        name="retention_sample",
    )(state, rq, rk, rv, rg, gamma)


def kernel(x_prompt, x_sample, cache_k, cache_v, state_ret, page_table, c_prompt, c_sample, w_ada, b_ada, g_mix, w_in, w_ret_o, w_att_o, w_out, g_ffn, w_rg, b_rg, w_re, b_re, w_gate_e, w_up_e, w_down_e, g_final):
    bp, s, d = x_prompt.shape
    bs, ds, _ = x_sample.shape
    assert bp == 1 and ds == 1
    depth = w_ada.shape[0]
    n_pool = cache_k.shape[1]
    n_pages = page_table.shape[1]
    w = HEAD_W

    n_rows = bp + bs
    pad = (-n_rows) % SUBLANES
    c_all = jnp.concatenate([c_prompt, c_sample, jnp.zeros((pad, d), F32)], axis=0)
    mods = _ada(c_all, w_ada, b_ada)

    cache_kt = cache_k.transpose(0, 1, 3, 4, 2).reshape(depth, n_pool, w, PAGE_SIZE)
    cache_vt = cache_v.transpose(0, 1, 3, 4, 2).reshape(depth, n_pool, w, PAGE_SIZE)
    pt_flat = page_table.reshape(-1)

    n_log = N_GROUPS + N_EXPERTS
    rr = 32
    g_fin = g_final.reshape(1, d)

    xp = x_prompt.reshape(s, d)
    xs = x_sample.reshape(bs, d)
    kp_l, vp_l, rp_l, ks_l, vs_l, rs_l = [], [], [], [], [], []
    for l in range(depth):
        last = l == depth - 1
        mp = [mods[l, 0:bp, j * d:(j + 1) * d] for j in range(6)]
        ms = [mods[l, bp:bp + bs, j * d:(j + 1) * d] for j in range(6)]
        w_in_b = w_in[l].astype(BF16)
        w_kvt_b = w_in[l][:, 5 * w:7 * w].T.astype(BF16)
        wro_b = w_ret_o[l].astype(BF16)
        wao_b = w_att_o[l].astype(BF16)
        wo_b = w_out[l].astype(BF16)
        wr_t = jnp.concatenate([w_rg[l].T, w_re[l].T, jnp.zeros((rr - n_log, d), F32)], axis=0)
        br_col = jnp.concatenate([b_rg[l], b_re[l], jnp.zeros((rr - n_log,), F32)]).reshape(rr, 1)
        gm = g_mix[l].reshape(1, d)
        gf = g_ffn[l].reshape(1, d)

        proj = _proj_sample(xs, gm, ms[0], ms[1], w_in, l)
        cut = lambda j, n=1: proj[:, j * w:(j + n) * w]
        rq_s, rk_s, rv_s, rg_s, aq_s, ak_s, av_s = (cut(j) for j in range(7))
        gr_s, ga_s = cut(7, 2), cut(9, 2)
        q_col = aq_s.reshape(bs, w, 1)

        rq, rk, rv, rg, aq, akb, akt, avt, avtb, gr, ga, qc = _proj_prompt(xp, gm, mp[0], mp[1], w_in_b, w_kvt_b)
        ret_in, st = _retention_prompt(rq, rk, rv, rg, chunk=MOBA_BLOCK)
        att_in = _moba_prompt(aq, qc, akb, avtb)
        x1, h2, wts, wtst, gcnt = _merge_router(xp, ret_in, att_in, gr, ga, mp[2], wro_b, wao_b, wo_b, gf,
                                                mp[3], mp[4], wr_t, br_col, tm=512, exact=0)
        xp, sel_s = _moe_sorted(x1, h2, wts, wtst, gcnt, mp[5], w_gate_e[l].astype(BF16),
                                w_up_e[l].astype(BF16), w_down_e[l].astype(BF16), g_fin, tm=1024,
                                final_norm=last, paged=(cache_kt, l, pt_flat, q_col, n_pages))
        kp_l.append(akt)
        vp_l.append(avt)
        rp_l.append(st.reshape(bp, RET_HEADS, RET_DK, RET_DV))

        new_state, ret_s = _ret_sample(
            state_ret, l,
            rq_s.reshape(bs, RET_HEADS, RET_DK, 1), rk_s.reshape(bs, RET_HEADS, RET_DK, 1),
            rv_s.reshape(bs, RET_HEADS, 1, RET_DV), rg_s.reshape(bs, RET_HEADS, 1, RET_DV))
        sel_flat = sel_s[:, :MOBA_TOPK, :ATT_HEADS].transpose(0, 2, 1).reshape(-1)
        att_s = _moba_sample(cache_kt, cache_vt, l, pt_flat, sel_flat, q_col, ak_s.reshape(bs, w, 1),
                             av_s.reshape(bs, w, 1), n_pages)
        x1s, h2s, wts_s, _, _ = _merge_router(xs, ret_s.reshape(bs, w), att_s.reshape(bs, w), gr_s, ga_s, ms[2],
                                              w_ret_o, w_att_o, w_out, gf, ms[3], ms[4], wr_t, br_col,
                                              tm=bs, exact=l + 1)
        xs = _moe(x1s, h2s, wts_s, ms[5], w_gate_e, w_up_e, w_down_e, l, g_fin, tm=bs, final_norm=last,
                  exact=True)
        ks_l.append(ak_s.reshape(bs, 1, ATT_HEADS, ATT_HD))
        vs_l.append(av_s.reshape(bs, 1, ATT_HEADS, ATT_HD))
        rs_l.append(new_state)

    def kv_out(parts):
        return jnp.stack(parts).reshape(depth, bp, ATT_HEADS, ATT_HD, s).transpose(0, 1, 4, 2, 3)

    return (xp.reshape(bp, s, d), xs.reshape(bs, 1, d), kv_out(kp_l), kv_out(vp_l), jnp.stack(rp_l),
            jnp.stack(ks_l), jnp.stack(vs_l), jnp.stack(rs_l))
```

```python
import functools

import jax
import jax.numpy as jnp
from jax import lax
from jax.experimental import pallas as pl
from jax.experimental.pallas import tpu as pltpu

EPS = 1e-6
RET_HEADS = 8
RET_DK = 64
RET_DV = 64
ATT_HEADS = 8
ATT_HD = 64
HEAD_W = RET_HEADS * RET_DK
MOBA_BLOCK = 256
MOBA_TOPK = 3
PAGE_SIZE = 128
PAGES_PER_BLOCK = MOBA_BLOCK // PAGE_SIZE
N_GROUPS = 4
EXP_PER_GROUP = 4
N_EXPERTS = N_GROUPS * EXP_PER_GROUP
SUBLANES = 8
LANES = 128
PAIRS = HEAD_W // LANES
NEG = -1e30
ALIBI_SLOPES = tuple(2.0 ** (-8.0 * (j + 1) / ATT_HEADS) for j in range(ATT_HEADS))
PAGES_PER_STEP = 16
MOE_CHUNK = 128
MOE_SORT_ROWS = 256
VMEM_LIMIT = 56 * 1024 * 1024

F32 = jnp.float32
BF16 = jnp.bfloat16
HI = lax.Precision.HIGHEST
NT = (((1,), (1,)), ((), ()))
TN = (((0,), (0,)), ((), ()))


def _dot(a, b):
    return jnp.dot(a, b, preferred_element_type=F32)


def _dot_nt(a, b):
    return lax.dot_general(a, b, NT, preferred_element_type=F32)


def _dot_hi(a, b):
    return jnp.dot(a, b, preferred_element_type=F32, precision=HI)


def _dot_nt_hi(a, b):
    return lax.dot_general(a, b, NT, preferred_element_type=F32, precision=HI)


def _mm(a, w_ref, exact):
    if exact:
        return _dot_hi(a.astype(F32), w_ref[...])
    return _dot(a.astype(BF16), w_ref[...].astype(BF16))


def _dot_split(a, b_bf16, lhs_exact=False):
    hi = a.astype(BF16)
    lo = (a - hi.astype(F32)).astype(BF16)
    if lhs_exact:
        return _dot(b_bf16, hi) + _dot(b_bf16, lo)
    return _dot(hi, b_bf16) + _dot(lo, b_bf16)


def _params(*sem):
    return pltpu.CompilerParams(dimension_semantics=sem, vmem_limit_bytes=VMEM_LIMIT)


def _full(shape):
    n = len(shape)
    return pl.BlockSpec(shape, lambda *_: (0,) * n)


def _modulate(x, g, shift, scale):
    ms = jnp.mean(x * x, axis=-1, keepdims=True)
    y = x * lax.rsqrt(ms + EPS) * g
    return y * (1.0 + scale) + shift


def _silu(x):
    return x * jax.nn.sigmoid(x)


def _head_lane_mask(shape, a):
    lane = lax.broadcasted_iota(jnp.int32, shape, len(shape) - 1)
    return (lane // RET_DK) == a


def _top3_rows(g, n_rows):
    row = lax.broadcasted_iota(jnp.int32, g.shape, 0)
    sel = jnp.zeros(g.shape, F32)
    idxs = []
    for _ in range(MOBA_TOPK):
        cmax = jnp.max(g, axis=0, keepdims=True)
        idx = jnp.min(jnp.where(g == cmax, row, n_rows), axis=0, keepdims=True)
        pick = jnp.logical_and(row == idx, cmax > -jnp.inf)
        sel = jnp.where(pick, 1.0, sel)
        g = jnp.where(pick, -jnp.inf, g)
        idxs.append(idx)
    return sel, idxs


def _ada_kernel(c_ref, w_ref, b_ref, o_ref):
    c = c_ref[...]
    o_ref[...] = _dot_hi(_silu(c), w_ref[...]) + b_ref[...]


def _ada(c_all, w_ada, b_ada):
    depth, d, d6 = w_ada.shape
    r = c_all.shape[0]
    return pl.pallas_call(
        _ada_kernel,
        grid=(depth, d6 // d),
        in_specs=[
            pl.BlockSpec((r, d), lambda l, j: (0, 0)),
            pl.BlockSpec((None, d, d), lambda l, j: (l, 0, j)),
            pl.BlockSpec((None, 1, d), lambda l, j: (l, 0, j)),
        ],
        out_specs=pl.BlockSpec((None, r, d), lambda l, j: (l, 0, j)),
        out_shape=jax.ShapeDtypeStruct((depth, r, d6), F32),
        compiler_params=_params("parallel", "parallel"),
        name="ada_mod",
    )(c_all, w_ada, b_ada.reshape(depth, 1, d6))


def _proj_prompt_kernel(x_ref, g_ref, sh_ref, sc_ref, w_ref, wkvt_ref,
                        rq_ref, rk_ref, rv_ref, rg_ref, aq_ref, akb_ref, akt_ref, avt_ref, avtb_ref,
                        gr_ref, ga_ref, qc_ref, means_ref, *, slopes):
    i = pl.program_id(0)
    nb = means_ref.shape[0]
    tm = x_ref.shape[0]
    w = HEAD_W

    @pl.when(i == 0)
    def _():
        means_ref[...] = jnp.zeros_like(means_ref)

    hb = _modulate(x_ref[...], g_ref[...], sh_ref[...], sc_ref[...]).astype(BF16)

    def col(c, n=1):
        return _dot(hb, w_ref[:, c * w:(c + n) * w])

    rq_ref[...] = col(0).astype(BF16)
    rk_ref[...] = (col(1) * RET_DK ** -0.5).astype(BF16)
    rv_ref[...] = col(2).astype(BF16)
    rg_ref[...] = col(3)
    q = col(4) * ATT_HD ** -0.5
    aq_ref[...] = q.astype(BF16)
    k = col(5)
    akb_ref[...] = k.astype(BF16)
    gr_ref[...] = col(7, 2)
    ga_ref[...] = col(9, 2)
    kvt = _dot_nt(wkvt_ref[...], hb)
    akt_ref[...] = kvt[:w]
    avt_ref[...] = kvt[w:]
    avtb_ref[...] = kvt[w:].astype(BF16)

    blk = lax.broadcasted_iota(jnp.int32, (nb, tm), 0)
    valid = blk < i
    blk_dist = (MOBA_BLOCK * (i - blk)).astype(F32)
    xrow = lax.broadcasted_iota(jnp.int32, (LANES - nb, tm), 0)
    xlane = lax.broadcasted_iota(jnp.int32, (LANES - nb, tm), 1).astype(F32)
    for p in range(PAIRS):
        mp = means_ref[:, p * LANES:(p + 1) * LANES]
        qp = q[:, p * LANES:(p + 1) * LANES]
        for a in range(2):
            slope = slopes[2 * p + a]
            qm = jnp.where(_head_lane_mask(qp.shape, a), qp, 0.0)
            gate = jnp.where(valid, _dot_nt_hi(mp, qm), -jnp.inf)
            sel, _ = _top3_rows(gate, nb)
            ct = jnp.where(sel > 0.5, -slope * blk_dist, NEG)
            extra = jnp.where(xrow == 0, -slope * xlane, jnp.where(xrow == 1, slope, 0.0))
            qc_ref[2 * p + a] = jnp.concatenate([ct, extra], axis=0).T.astype(BF16)

    means_ref[pl.ds(i, 1), :] = jnp.mean(k, axis=0, keepdims=True)


def _proj_prompt(x, g, shift, scale, w_in_b, layer, w_kvt_b):
    s, d = x.shape
    tm = MOBA_BLOCK
    assert s % tm == 0
    nb = s // tm
    w = HEAD_W
    n_in = w_in_b.shape[2]
    row = lambda width: pl.BlockSpec((tm, width), lambda i: (i, 0))
    colblk = pl.BlockSpec((w, tm), lambda i: (0, i))
    vec = pl.BlockSpec((1, d), lambda i: (0, 0))
    out_shape = (
        jax.ShapeDtypeStruct((s, w), BF16),
        jax.ShapeDtypeStruct((s, w), BF16),
        jax.ShapeDtypeStruct((s, w), BF16),
        jax.ShapeDtypeStruct((s, w), F32),
        jax.ShapeDtypeStruct((s, w), BF16),
        jax.ShapeDtypeStruct((nb, tm, w), BF16),
        jax.ShapeDtypeStruct((w, s), F32),
        jax.ShapeDtypeStruct((w, s), F32),
        jax.ShapeDtypeStruct((nb, w, tm), BF16),
        jax.ShapeDtypeStruct((s, d), F32),
        jax.ShapeDtypeStruct((s, d), F32),
        jax.ShapeDtypeStruct((ATT_HEADS, s, LANES), BF16),
    )
    assert nb + 2 <= LANES
    out_specs = (
        row(w), row(w), row(w), row(w), row(w),
        pl.BlockSpec((None, tm, w), lambda i: (i, 0, 0)),
        colblk, colblk,
        pl.BlockSpec((None, w, tm), lambda i: (i, 0, 0)),
        row(d), row(d),
        pl.BlockSpec((ATT_HEADS, tm, LANES), lambda i: (0, i, 0)),
    )
    return pl.pallas_call(
        functools.partial(_proj_prompt_kernel, slopes=ALIBI_SLOPES),
        grid=(nb,),
        in_specs=[row(d), vec, vec, vec,
                  pl.BlockSpec((None, d, n_in), lambda i: (layer, 0, 0), pipeline_mode=pl.Buffered(1)),
                  pl.BlockSpec((2 * w, d), lambda i: (0, 0), pipeline_mode=pl.Buffered(1))],
        out_specs=out_specs,
        out_shape=out_shape,
        scratch_shapes=[pltpu.VMEM((nb, w), F32)],
        compiler_params=_params("arbitrary"),
        name="proj_prompt",
    )(x, g, shift, scale, w_in_b, w_kvt_b)


def _proj_sample_kernel(x_ref, g_ref, sh_ref, sc_ref, w_ref, o_ref):
    h = _modulate(x_ref[...], g_ref[...], sh_ref[...], sc_ref[...])
    o_ref[...] = _dot_hi(h, w_ref[...])


def _proj_sample(x, g, shift, scale, w_in, layer):
    b, d = x.shape
    n_in = w_in.shape[2]
    tn = HEAD_W
    return pl.pallas_call(
        _proj_sample_kernel,
        grid=(n_in // tn,),
        in_specs=[_full((b, d)), _full((1, d)), _full((b, d)), _full((b, d)),
                  pl.BlockSpec((None, d, tn), lambda j: (layer, 0, j))],
        out_specs=pl.BlockSpec((b, tn), lambda j: (0, j)),
        out_shape=jax.ShapeDtypeStruct((b, n_in), F32),
        compiler_params=_params("parallel"),
        name="proj_sample",
    )(x, g, shift, scale, w_in)


def _retention_kernel(q_ref, k_ref, v_ref, rg_ref, decay_ref, xi_ref, zeta_ref, gc_ref, avg_ref,
                      o_ref, st_ref):
    i = pl.program_id(0)

    @pl.when(i == 0)
    def _():
        st_ref[...] = jnp.zeros_like(st_ref)

    c = q_ref.shape[0]
    row = lax.broadcasted_iota(jnp.int32, (LANES, LANES), 0)
    lane = lax.broadcasted_iota(jnp.int32, (LANES, LANES), 1)
    same_head = (row // RET_DK) == (lane // RET_DV)
    avg = avg_ref[...]
    for p in range(PAIRS):
        sl = slice(p * LANES, (p + 1) * LANES)
        q = q_ref[:, sl]
        k = k_ref[:, sl]
        v = v_ref[:, sl]
        state = st_ref[p]
        cross = _dot(q, state.astype(BF16)) * xi_ref[:, sl]
        inner = []
        for a in range(2):
            qm = jnp.where(_head_lane_mask(q.shape, a), q, jnp.zeros_like(q))
            scores = _dot_nt(qm, k) * decay_ref[2 * p + a]
            inner.append(_dot(scores.astype(BF16), v))
        o = jnp.where(_head_lane_mask((c, LANES), 0), inner[0], inner[1]) + cross
        kz = (k.astype(F32) * zeta_ref[:, sl]).astype(BF16)
        upd = lax.dot_general(kz, v, TN, preferred_element_type=F32)
        st_ref[p] = gc_ref[:, sl] * state + jnp.where(same_head, upd, 0.0)
        mu = _dot_split(o, avg)
        dlt = o - mu
        var = _dot_split(dlt * dlt, avg)
        hn = dlt * lax.rsqrt(var + EPS)
        o_ref[:, sl] = (_silu(rg_ref[:, sl]) * hn).astype(BF16)


def _ret_tables(c):
    h = jnp.arange(RET_HEADS, dtype=F32)
    log_gamma = jnp.log1p(-(2.0 ** (-5.0 - h)))
    idx = jnp.arange(c, dtype=F32)
    diff = idx[:, None] - idx[None, :]
    decay = jnp.where(diff >= 0, jnp.exp(log_gamma[:, None, None] * jnp.maximum(diff, 0.0)), 0.0)
    xi = jnp.exp(log_gamma[None, :] * (idx[:, None] + 1.0))
    zeta = jnp.exp(log_gamma[None, :] * (c - 1.0 - idx[:, None]))
    gc = jnp.exp(log_gamma * c)[None, :]
    rep = lambda t: jnp.repeat(t, RET_DV, axis=1)
    return decay, rep(xi), rep(zeta), rep(gc)


def _head_avg_matrix():
    r = jnp.arange(LANES)
    return jnp.where((r[:, None] // RET_DV) == (r[None, :] // RET_DV), 1.0 / RET_DV, 0.0).astype(BF16)


def _retention_prompt(rq, rk, rv, rg, chunk):
    s, w = rq.shape
    assert s % chunk == 0
    decay, xi, zeta, gc = _ret_tables(chunk)
    row = pl.BlockSpec((chunk, w), lambda i: (i, 0))
    o, st = pl.pallas_call(
        _retention_kernel,
        grid=(s // chunk,),
        in_specs=[row, row, row, row,
                  _full((RET_HEADS, chunk, chunk)), _full((chunk, w)), _full((chunk, w)), _full((1, w)),
                  _full((LANES, LANES))],
        out_specs=(row, _full((PAIRS, LANES, LANES))),
        out_shape=(jax.ShapeDtypeStruct((s, w), BF16), jax.ShapeDtypeStruct((PAIRS, LANES, LANES), F32)),
        compiler_params=_params("arbitrary"),
        name="retention_prompt",
    )(rq, rk, rv, rg, decay, xi, zeta, gc, _head_avg_matrix())
    st = st.reshape(PAIRS, 2, RET_DK, 2, RET_DV)
    st = jnp.stack([st[:, 0, :, 0, :], st[:, 1, :, 1, :]], axis=1).reshape(RET_HEADS, RET_DK, RET_DV)
    return o, st


def _moba_prompt_kernel(q_ref, qc_ref, k_ref, vt_ref, ctab_ref, cown_ref, o_ref,
                        qcat_ref, sa_ref, sb_ref, ma_ref, mb_ref, m_ref, l_ref, acc_ref):
    i = pl.program_id(0)
    tq = q_ref.shape[0]
    tk = k_ref.shape[1]
    nb = k_ref.shape[0]
    heads = range(ATT_HEADS)
    hd = ATT_HD

    for p in range(PAIRS):
        qp = q_ref[:, p * LANES:(p + 1) * LANES]
        for a in range(2):
            h = 2 * p + a
            qm = jnp.where(_head_lane_mask(qp.shape, a), qp, jnp.zeros_like(qp))
            qcat_ref[h] = jnp.concatenate([qm, qc_ref[h]], axis=1)

    ones = jnp.ones((2 * SUBLANES, tk), BF16)

    def scores(n, h, s_ref, mx_ref):
        p = h // 2
        kcat = jnp.concatenate([k_ref[n, :, p * LANES:(p + 1) * LANES], ctab_ref[n]], axis=1)
        s = _dot_nt(kcat, qcat_ref[h])
        s_ref[h] = s
        mx_ref[h:h + 1, :] = jnp.max(s, axis=0, keepdims=True)

    def attend(n, h, s_ref, mx_ref):
        s = s_ref[h]
        m = m_ref[h:h + 1, :]
        m_new = jnp.maximum(m, mx_ref[h:h + 1, :])
        alpha = jnp.exp(m - m_new)
        pr = jnp.exp((s - m_new).astype(BF16))
        rows = slice(h * hd, (h + 1) * hd)
        pv = _dot(jnp.concatenate([vt_ref[n, rows, :], ones], axis=0), pr)
        m_ref[h:h + 1, :] = m_new
        l_ref[h:h + 1, :] = alpha * l_ref[h:h + 1, :] + pv[hd:hd + 1, :]
        acc_ref[rows, :] = alpha * acc_ref[rows, :] + pv[:hd, :]

    kk = lax.broadcasted_iota(jnp.int32, (tk, tq), 0)
    qq = lax.broadcasted_iota(jnp.int32, (tk, tq), 1)
    causal = kk <= qq
    m_ref[...] = jnp.full(m_ref.shape, NEG, F32)
    l_ref[...] = jnp.zeros_like(l_ref)
    acc_ref[...] = jnp.zeros_like(acc_ref)
    for h in heads:
        p = h // 2
        kcat = jnp.concatenate([k_ref[i, :, p * LANES:(p + 1) * LANES], cown_ref[...]], axis=1)
        s = jnp.where(causal, _dot_nt(kcat, qcat_ref[h]), NEG)
        sb_ref[h] = s
        mb_ref[h:h + 1, :] = jnp.max(s, axis=0, keepdims=True)
        scores(0, h, sa_ref, ma_ref)
    for h in heads:
        attend(i, h, sb_ref, mb_ref)

    @pl.loop(0, (i + 1) // 2)
    def _(t):
        nxt = jnp.minimum(2 * t + 2, nb - 1)
        for h in heads:
            scores(2 * t + 1, h, sb_ref, mb_ref)
            attend(2 * t, h, sa_ref, ma_ref)
        for h in heads:
            scores(nxt, h, sa_ref, ma_ref)
            attend(2 * t + 1, h, sb_ref, mb_ref)

    for h in heads:
        rows = slice(h * hd, (h + 1) * hd)
        acc_ref[rows, :] = acc_ref[rows, :] / l_ref[h:h + 1, :]
    o_ref[...] = acc_ref[...].T.astype(BF16)


def _moba_bias_table(nb, tk):
    lane = jnp.arange(LANES)[None, None, :]
    blk = jnp.arange(nb)[:, None, None]
    kk = jnp.arange(tk, dtype=F32)[None, :, None]
    t = jnp.where(lane == blk, 1.0, 0.0)
    t = jnp.where(lane == nb, 1.0, t)
    return jnp.where(lane == nb + 1, kk, t).astype(BF16)


def _moba_prompt(aq, qc, akb, avtb):
    s, w = aq.shape
    nb, tk, _ = akb.shape
    tq = MOBA_BLOCK
    assert nb % 2 == 0 and tk == tq
    ctab = _moba_bias_table(nb, tk)
    cown = jnp.where(jnp.arange(LANES)[None, :] < nb, 0, ctab[0])
    once = dict(pipeline_mode=pl.Buffered(1))
    return pl.pallas_call(
        _moba_prompt_kernel,
        grid=(s // tq,),
        in_specs=[
            pl.BlockSpec((tq, w), lambda i: (i, 0)),
            pl.BlockSpec((ATT_HEADS, tq, LANES), lambda i: (0, i, 0)),
            pl.BlockSpec((nb, tk, w), lambda i: (0, 0, 0), **once),
            pl.BlockSpec((nb, w, tk), lambda i: (0, 0, 0), **once),
            pl.BlockSpec((nb, tk, LANES), lambda i: (0, 0, 0), **once),
            pl.BlockSpec((tk, LANES), lambda i: (0, 0), **once),
        ],
        out_specs=pl.BlockSpec((tq, w), lambda i: (i, 0)),
        out_shape=jax.ShapeDtypeStruct((s, w), BF16),
        scratch_shapes=[
            pltpu.VMEM((ATT_HEADS, tq, 2 * LANES), BF16),
            pltpu.VMEM((ATT_HEADS, tk, tq), F32),
            pltpu.VMEM((ATT_HEADS, tk, tq), F32),
            pltpu.VMEM((ATT_HEADS, tq), F32),
            pltpu.VMEM((ATT_HEADS, tq), F32),
            pltpu.VMEM((ATT_HEADS, tq), F32),
            pltpu.VMEM((ATT_HEADS, tq), F32),
            pltpu.VMEM((w, tq), F32),
        ],
        compiler_params=_params("arbitrary"),
        name="moba_prompt",
    )(aq, qc, akb, avtb, ctab, cown)


def _merge_router_kernel(x_ref, ret_ref, att_ref, gr_ref, ga_ref, g1_ref, wro_ref, wao_ref, wo_ref,
                         gf_ref, sh_ref, sc_ref, wr_ref, br_ref,
                         x1_ref, h2_ref, wts_ref, wtst_ref, gcnt_ref, *, exact):
    ret = _mm(ret_ref[...], wro_ref, exact)
    att = _mm(att_ref[...], wao_ref, exact)
    merged = jax.nn.sigmoid(gr_ref[...]) * ret + jax.nn.sigmoid(ga_ref[...]) * att
    x1 = x_ref[...] + g1_ref[...] * _mm(merged, wo_ref, exact)
    x1_ref[...] = x1
    h2 = _modulate(x1, gf_ref[...], sh_ref[...], sc_ref[...])
    h2_ref[...] = h2.astype(h2_ref.dtype)

    logits = _dot_nt_hi(wr_ref[...], h2) + br_ref[...]
    lg = [logits[g:g + 1, :] for g in range(N_GROUPS)]
    le = [logits[N_GROUPS + e:N_GROUPS + e + 1, :] for e in range(N_EXPERTS)]
    mg = functools.reduce(jnp.maximum, lg)
    eg = [jnp.exp(v - mg) for v in lg]
    den = functools.reduce(jnp.add, eg)
    pg = [v / den for v in eg]
    p_top = functools.reduce(jnp.maximum, pg)
    taken = jnp.zeros(p_top.shape, jnp.bool_)
    onehot = []
    for g in range(N_GROUPS):
        hit = jnp.logical_and(pg[g] == p_top, jnp.logical_not(taken))
        onehot.append(hit)
        taken = jnp.logical_or(taken, hit)
    vin = []
    for j in range(EXP_PER_GROUP):
        v = jnp.zeros_like(p_top)
        for g in range(N_GROUPS):
            v = jnp.where(onehot[g], le[g * EXP_PER_GROUP + j], v)
        vin.append(v)
    rank = []
    for j in range(EXP_PER_GROUP):
        r = jnp.zeros(p_top.shape, jnp.int32)
        for kx in range(EXP_PER_GROUP):
            if kx == j:
                continue
            ahead = (vin[kx] > vin[j]) | ((vin[kx] == vin[j]) & (kx < j))
            r = r + ahead.astype(jnp.int32)
        rank.append(r)
    v1 = functools.reduce(jnp.maximum, vin)
    v2 = jnp.full_like(v1, -jnp.inf)
    for j in range(EXP_PER_GROUP):
        v2 = jnp.where(rank[j] == 1, vin[j], v2)
    e2 = jnp.exp(v2 - v1)
    w_first = 1.0 / (1.0 + e2)
    w_second = e2 / (1.0 + e2)
    wtst_ref[...] = jnp.zeros_like(wtst_ref)
    row8 = lax.broadcasted_iota(jnp.int32, gcnt_ref.shape, 0)
    lane8 = lax.broadcasted_iota(jnp.int32, gcnt_ref.shape, 1)
    gcnt = jnp.zeros(gcnt_ref.shape, F32)
    for g in range(N_GROUPS):
        hot = jnp.where(onehot[g], 1.0, 0.0)
        wtst_ref[N_EXPERTS + g:N_EXPERTS + g + 1, :] = hot
        n_g = jnp.sum(hot, axis=1, keepdims=True)
        gcnt = jnp.where(jnp.logical_and(row8 == 0, lane8 == g), n_g, gcnt)
        for j in range(EXP_PER_GROUP):
            wj = jnp.where(rank[j] == 0, w_first, jnp.where(rank[j] == 1, w_second, 0.0)) * p_top
            e = g * EXP_PER_GROUP + j
            wtst_ref[e:e + 1, :] = jnp.where(onehot[g], wj, 0.0)
    wts_ref[...] = wtst_ref[...].T
    gcnt_ref[...] = gcnt.astype(jnp.int32)


def _merge_router(x, ret_in, att_in, gr, ga, gate1, wro, wao, wo, g_ffn, shift2, scale2, wr_t, br_col, tm, exact):
    s, d = x.shape
    w = HEAD_W
    assert s % tm == 0
    row = lambda width: pl.BlockSpec((tm, width), lambda i: (i, 0))
    mod = lambda a: pl.BlockSpec((tm if a.shape[0] > 1 else 1, d), lambda i: (i if a.shape[0] > 1 else 0, 0))
    if exact:
        layer = exact - 1
        wspec = lambda a: pl.BlockSpec((None,) + a.shape[1:], lambda i: (layer, 0, 0))
    else:
        wspec = lambda a: _full(a.shape)
    rr = wr_t.shape[0]
    return pl.pallas_call(
        functools.partial(_merge_router_kernel, exact=bool(exact)),
        grid=(s // tm,),
        in_specs=[row(d), row(w), row(w), row(d), row(d), mod(gate1),
                  wspec(wro), wspec(wao), wspec(wo),
                  _full((1, d)), mod(shift2), mod(scale2), _full((rr, d)), _full((rr, 1))],
        out_specs=(row(d), row(d), row(LANES),
                   pl.BlockSpec((LANES, tm), lambda i: (0, i)),
                   pl.BlockSpec((None, SUBLANES, LANES), lambda i: (i, 0, 0))),
        out_shape=(jax.ShapeDtypeStruct((s, d), F32), jax.ShapeDtypeStruct((s, d), F32 if exact else BF16),
                   jax.ShapeDtypeStruct((s, LANES), F32),
                   jax.ShapeDtypeStruct((LANES, s), F32),
                   jax.ShapeDtypeStruct((s // tm, SUBLANES, LANES), jnp.int32)),
        compiler_params=_params("parallel"),
        name="merge_router",
    )(x, ret_in, att_in, gr, ga, gate1, wro, wao, wo, g_ffn, shift2, scale2, wr_t, br_col)


def _moe_kernel(x1_ref, h2_ref, wts_ref, g2_ref, wg_ref, wu_ref, wd_ref, gfin_ref, o_ref, acc_ref,
                *, final_norm, exact):
    e = pl.program_id(1)

    @pl.when(e == 0)
    def _():
        acc_ref[...] = jnp.zeros_like(acc_ref)

    h2 = h2_ref[...]
    wts = wts_ref[...]
    lane = lax.broadcasted_iota(jnp.int32, wts.shape, 1)
    wcol = jnp.sum(jnp.where(lane == e, wts, 0.0), axis=1, keepdims=True)
    act = _silu(_mm(h2, wg_ref, exact)) * _mm(h2, wu_ref, exact) * wcol
    acc_ref[...] += _mm(act, wd_ref, exact)

    @pl.when(e == pl.num_programs(1) - 1)
    def _():
        x2 = x1_ref[...] + g2_ref[...] * acc_ref[...]
        if final_norm:
            ms = jnp.mean(x2 * x2, axis=-1, keepdims=True)
            x2 = x2 * lax.rsqrt(ms + EPS) * gfin_ref[...]
        o_ref[...] = x2


def _moe(x1, h2, wts, gate2, wg, wu, wd, layer, g_final, tm, final_norm, exact):
    s, d = x1.shape
    ne, f = wg.shape[-3], wg.shape[-1]
    assert s % tm == 0
    row = lambda width: pl.BlockSpec((tm, width), lambda i, e: (i, 0))
    mod = lambda a: pl.BlockSpec((tm if a.shape[0] > 1 else 1, d), lambda i, e: (i if a.shape[0] > 1 else 0, 0))
    wspec = lambda r, c: pl.BlockSpec((None, None, r, c), lambda i, e: (layer, e, 0, 0))
    return pl.pallas_call(
        functools.partial(_moe_kernel, final_norm=final_norm, exact=exact),
        grid=(s // tm, ne),
        in_specs=[row(d), row(d), row(LANES), mod(gate2),
                  wspec(d, f), wspec(d, f), wspec(f, d),
                  pl.BlockSpec((1, d), lambda i, e: (0, 0))],
        out_specs=row(d),
        out_shape=jax.ShapeDtypeStruct((s, d), F32),
        scratch_shapes=[pltpu.VMEM((tm, d), F32)],
        compiler_params=_params("parallel", "arbitrary"),
        name="moe_experts",
    )(x1, h2, wts, gate2, wg, wu, wd, g_final)


def _moe_sort(starts, h2_ref, wts_ref, wtst_ref, p_scr, pt_scr, hs_scr, ws_scr, acc_scr):
    tm = h2_ref.shape[0]
    ch = MOE_SORT_ROWS
    lo, hi = N_EXPERTS, N_EXPERTS + N_GROUPS
    wts = wts_ref[...]
    lane_t = lax.broadcasted_iota(jnp.int32, wts.shape, 1)
    hot_t = jnp.where(jnp.logical_and(lane_t >= lo, lane_t < hi), wts, 0.0)
    wtst = wtst_ref[...]
    row_l = lax.broadcasted_iota(jnp.int32, wtst.shape, 0)
    hot_l = jnp.where(jnp.logical_and(row_l >= lo, row_l < hi), wtst, 0.0)
    start_t = jnp.zeros((1, LANES), F32)
    start_l = jnp.zeros((LANES, 1), F32)
    lane1 = lax.broadcasted_iota(jnp.int32, (1, LANES), 1)
    row1 = lax.broadcasted_iota(jnp.int32, (LANES, 1), 0)
    for g in range(N_GROUPS):
        sg = starts[g].astype(F32)
        start_t = jnp.where(lane1 == lo + g, sg, start_t)
        start_l = jnp.where(row1 == lo + g, sg, start_l)
    hot_t_b = hot_t.astype(BF16)
    hot_l_b = hot_l.astype(BF16)
    for c in range(tm // ch):
        sl = slice(c * ch, (c + 1) * ch)
        tok = lax.broadcasted_iota(jnp.int32, (ch, tm), 0) + c * ch
        other = lax.broadcasted_iota(jnp.int32, (ch, tm), 1)
        earlier = jnp.where(other < tok, 1.0, 0.0).astype(BF16)
        rank = _dot(earlier, hot_t_b)
        pos = jnp.sum(hot_t[sl] * (rank + start_t), axis=1, keepdims=True).astype(jnp.int32)
        pt_scr[sl, :] = jnp.where(other == pos, 1.0, 0.0).astype(BF16)
        tok = lax.broadcasted_iota(jnp.int32, (tm, ch), 1) + c * ch
        other = lax.broadcasted_iota(jnp.int32, (tm, ch), 0)
        earlier = jnp.where(other < tok, 1.0, 0.0).astype(BF16)
        rank = _dot(hot_l_b, earlier)
        pos = jnp.sum(hot_l[:, sl] * (rank + start_l), axis=0, keepdims=True).astype(jnp.int32)
        p_scr[:, sl] = jnp.where(other == pos, 1.0, 0.0).astype(BF16)
    p = p_scr[...]
    hs_scr[...] = _dot(p, h2_ref[...]).astype(BF16)
    ws_scr[...] = _dot_split(wts, p, lhs_exact=True)
    acc_scr[...] = jnp.zeros_like(acc_scr)


def _moe_sorted_kernel(cnt_ref, *refs, final_norm, router_tiles):
    _moe_sorted_body(cnt_ref, *refs, final_norm=final_norm, router_tiles=router_tiles)


def _moe_sorted_paged_kernel(cnt_ref, ptab_ref, *refs, final_norm, router_tiles, paged_steps, steps_per_seq):
    n_moe_in = 9
    moe_in, q_ref = refs[:n_moe_in], refs[n_moe_in]
    pages = refs[n_moe_in + 1:n_moe_in + 1 + PAGES_PER_STEP]
    o_ref, sel_ref, p_scr, pt_scr, hs_scr, ws_scr, acc_scr, qb_ref, gacc_ref = refs[n_moe_in + 1 + PAGES_PER_STEP:]
    _moe_sorted_body(cnt_ref, *moe_in, o_ref, p_scr, pt_scr, hs_scr, ws_scr, acc_scr,
                     final_norm=final_norm, router_tiles=router_tiles)
    step = pl.program_id(0) * pl.num_programs(1) + pl.program_id(1)

    @pl.when(step < paged_steps)
    def _():
        _paged_gate_step(step % steps_per_seq, steps_per_seq - 1, q_ref, pages, sel_ref, qb_ref, gacc_ref)


def _moe_sorted_body(cnt_ref, x1_ref, h2_ref, wts_ref, wtst_ref, g2_ref, wg_ref, wu_ref, wd_ref, gfin_ref,
                     o_ref, p_scr, pt_scr, hs_scr, ws_scr, acc_scr, *, final_norm, router_tiles):
    i = pl.program_id(0)
    e = pl.program_id(1)
    ch = MOE_CHUNK
    counts = []
    for g in range(N_GROUPS):
        n = cnt_ref[i * router_tiles * N_GROUPS + g]
        for j in range(1, router_tiles):
            n = n + cnt_ref[(i * router_tiles + j) * N_GROUPS + g]
        counts.append(n)
    starts = [jnp.int32(0)]
    for g in range(N_GROUPS - 1):
        starts.append(starts[-1] + counts[g])

    @pl.when(e == 0)
    def _():
        _moe_sort(starts, h2_ref, wts_ref, wtst_ref, p_scr, pt_scr, hs_scr, ws_scr, acc_scr)

    group = e // EXP_PER_GROUP
    seg0, seg_n = starts[0], counts[0]
    for g in range(1, N_GROUPS):
        seg0 = jnp.where(group == g, starts[g], seg0)
        seg_n = jnp.where(group == g, counts[g], seg_n)
    lane = lax.broadcasted_iota(jnp.int32, (ch, LANES), 1)

    def chunk(c, carry):
        rows = pl.ds(pl.multiple_of(c * ch, ch), ch)
        hs = hs_scr[rows, :]
        wcol = jnp.sum(jnp.where(lane == e, ws_scr[rows, :], 0.0), axis=1, keepdims=True)
        act = _silu(_dot(hs, wg_ref[...])) * _dot(hs, wu_ref[...]) * wcol
        acc_scr[rows, :] += _dot(act.astype(BF16), wd_ref[...])
        return carry

    lax.fori_loop(seg0 // ch, (seg0 + seg_n + ch - 1) // ch, chunk, 0)

    @pl.when(e == pl.num_programs(1) - 1)
    def _():
        y = _dot(pt_scr[...], acc_scr[...].astype(BF16))
        x2 = x1_ref[...] + g2_ref[...] * y
        if final_norm:
            ms = jnp.mean(x2 * x2, axis=-1, keepdims=True)
            x2 = x2 * lax.rsqrt(ms + EPS) * gfin_ref[...]
        o_ref[...] = x2


def _moe_sorted(x1, h2, wts, wtst, gcnt, gate2, wg_b, wu_b, wd_b, w_layer, g_final, tm, final_norm, paged=None):
    s, d = x1.shape
    _, ne, _, f = wg_b.shape
    router_tiles = gcnt.shape[0] * tm // s
    assert s % tm == 0 and tm % MOE_CHUNK == 0 and tm % MOE_SORT_ROWS == 0 and router_tiles >= 1
    cnt_flat = gcnt[:, 0, :N_GROUPS].reshape(-1)
    n_steps = (s // tm) * ne
    fuse = False
    if paged is not None:
        cache_kt, layer, page_table_flat, q_col, n_pages = paged
        assert n_pages % PAGES_PER_STEP == 0 and PAGES_PER_STEP % PAGES_PER_BLOCK == 0
        steps_per_seq = n_pages // PAGES_PER_STEP
        paged_steps = q_col.shape[0] * steps_per_seq
        n_blocks = n_pages // PAGES_PER_BLOCK
        assert n_blocks >= MOBA_TOPK
        fuse = paged_steps <= n_steps
    n_pre = 2 if fuse else 1
    row = lambda width: pl.BlockSpec((tm, width), lambda i, e, *_: (i, 0))
    wspec = lambda r, c: pl.BlockSpec((None, None, r, c), lambda i, e, *_: (w_layer, e, 0, 0))
    vec = pl.BlockSpec((1, d), lambda i, e, *_: (0, 0))
    in_specs = [row(d), row(d), row(LANES), pl.BlockSpec((LANES, tm), lambda i, e, *_: (0, i)), vec,
                wspec(d, f), wspec(d, f), wspec(f, d), vec]
    out_specs = [row(d)]
    out_shape = [jax.ShapeDtypeStruct((s, d), F32)]
    scratch = [pltpu.VMEM((tm, tm), BF16),
               pltpu.VMEM((tm, tm), BF16),
               pltpu.VMEM((tm, d), BF16),
               pltpu.VMEM((tm, LANES), F32),
               pltpu.VMEM((tm, d), F32)]
    operands = [cnt_flat, x1, h2, wts, wtst, gate2, wg_b, wu_b, wd_b, g_final]
    body = functools.partial(_moe_sorted_kernel, final_norm=final_norm, router_tiles=router_tiles)
    if fuse:
        def seq_step(i, e):
            st = jnp.minimum(i * ne + e, paged_steps - 1)
            return st // steps_per_seq, st % steps_per_seq

        def page_spec(g):
            def imap(i, e, cnt, pt):
                bi, j = seq_step(i, e)
                return (layer, pt[bi * n_pages + j * PAGES_PER_STEP + g], 0, 0)
            return pl.BlockSpec((None, None, HEAD_W, PAGE_SIZE), imap)

        in_specs += [pl.BlockSpec((None, HEAD_W, 1), lambda i, e, cnt, pt: (seq_step(i, e)[0], 0, 0))]
        in_specs += [page_spec(g) for g in range(PAGES_PER_STEP)]
        out_specs += [pl.BlockSpec((None, SUBLANES, LANES), lambda i, e, cnt, pt: (seq_step(i, e)[0], 0, 0))]
        out_shape += [jax.ShapeDtypeStruct((q_col.shape[0], SUBLANES, LANES), jnp.int32)]
        scratch += [pltpu.VMEM((HEAD_W, PAGE_SIZE), F32),
                    pltpu.VMEM((n_blocks * ATT_HEADS * SUBLANES, PAGE_SIZE), F32)]
        operands = [cnt_flat, page_table_flat] + operands[1:] + [q_col] + [cache_kt] * PAGES_PER_STEP
        body = functools.partial(_moe_sorted_paged_kernel, final_norm=final_norm, router_tiles=router_tiles,
                                 paged_steps=paged_steps, steps_per_seq=steps_per_seq)
    grid_spec = pltpu.PrefetchScalarGridSpec(
        num_scalar_prefetch=n_pre, grid=(s // tm, ne), in_specs=in_specs, out_specs=out_specs,
        scratch_shapes=scratch)
    outs = pl.pallas_call(
        body,
        grid_spec=grid_spec,
        out_shape=out_shape,
        compiler_params=_params("arbitrary", "arbitrary"),
        name="moe_sorted",
    )(*operands)
    if fuse:
        return outs[0], outs[1]
    sel = None if paged is None else _paged_gate(cache_kt, layer, page_table_flat, q_col, n_pages)
    return outs[0], sel


def _paged_gate_kernel(pt_ref, q_ref, *rest):
    pages = rest[:PAGES_PER_STEP]
    sel_ref, qb_ref, acc_ref = rest[PAGES_PER_STEP:]
    _paged_gate_step(pl.program_id(1), pl.num_programs(1) - 1, q_ref, pages, sel_ref, qb_ref, acc_ref)


def _paged_gate_step(j, last_j, q_ref, pages, sel_ref, qb_ref, acc_ref):
    rows_per_block = ATT_HEADS * SUBLANES

    @pl.when(j == 0)
    def _():
        qb_ref[...] = jnp.broadcast_to(q_ref[...] * ATT_HD ** -0.5, qb_ref.shape)

    qb = qb_ref[...]
    blocks_per_step = PAGES_PER_STEP // PAGES_PER_BLOCK
    for t in range(blocks_per_step):
        tot = pages[PAGES_PER_BLOCK * t][...]
        for u in range(1, PAGES_PER_BLOCK):
            tot = tot + pages[PAGES_PER_BLOCK * t + u][...]
        part = (tot * qb).reshape(ATT_HEADS, ATT_HD // SUBLANES, SUBLANES, PAGE_SIZE).sum(axis=1)
        base = pl.multiple_of((j * blocks_per_step + t) * rows_per_block, rows_per_block)
        acc_ref[pl.ds(base, rows_per_block), :] = part.reshape(rows_per_block, PAGE_SIZE)

    @pl.when(j == last_j)
    def _():
        n_rows = acc_ref.shape[0]
        nb = n_rows // rows_per_block
        g = jnp.sum(acc_ref[...], axis=1, keepdims=True)
        r = lax.broadcasted_iota(jnp.int32, (n_rows, LANES), 0)
        lane = lax.broadcasted_iota(jnp.int32, (n_rows, LANES), 1)
        spread = jnp.where(lane == (r // SUBLANES) % ATT_HEADS, g, 0.0)
        gate = spread.reshape(nb, rows_per_block, LANES).sum(axis=1) * (1.0 / MOBA_BLOCK)
        _, idxs = _top3_rows(gate, nb)
        out = jnp.zeros(sel_ref.shape, jnp.int32)
        rr = lax.broadcasted_iota(jnp.int32, sel_ref.shape, 0)
        for t, idx in enumerate(idxs):
            out = jnp.where(rr == t, idx, out)
        sel_ref[...] = out


def _paged_gate(cache_kt, layer, page_table_flat, q_col, n_pages):
    b = q_col.shape[0]
    assert n_pages % PAGES_PER_STEP == 0 and PAGES_PER_STEP % PAGES_PER_BLOCK == 0
    nb = n_pages // PAGES_PER_BLOCK
    assert nb >= MOBA_TOPK

    def page_spec(g):
        return pl.BlockSpec((None, None, HEAD_W, PAGE_SIZE),
                            lambda bi, j, pt: (layer, pt[bi * n_pages + j * PAGES_PER_STEP + g], 0, 0))

    grid_spec = pltpu.PrefetchScalarGridSpec(
        num_scalar_prefetch=1,
        grid=(b, n_pages // PAGES_PER_STEP),
        in_specs=[pl.BlockSpec((None, HEAD_W, 1), lambda bi, j, pt: (bi, 0, 0))]
                 + [page_spec(g) for g in range(PAGES_PER_STEP)],
        out_specs=pl.BlockSpec((None, SUBLANES, LANES), lambda bi, j, pt: (bi, 0, 0)),
        scratch_shapes=[pltpu.VMEM((HEAD_W, PAGE_SIZE), F32),
                        pltpu.VMEM((nb * ATT_HEADS * SUBLANES, PAGE_SIZE), F32)],
    )
    return pl.pallas_call(
        _paged_gate_kernel,
        grid_spec=grid_spec,
        out_shape=jax.ShapeDtypeStruct((b, SUBLANES, LANES), jnp.int32),
        compiler_params=_params("parallel", "arbitrary"),
        name="paged_gate",
    )(page_table_flat, q_col, *([cache_kt] * PAGES_PER_STEP))


def _moba_sample_kernel(pt_ref, sel_ref, slopes_ref, q_ref, kn_ref, vn_ref, *rest, past_len):
    npg = MOBA_TOPK * PAGES_PER_BLOCK
    kp = rest[:npg]
    vp = rest[npg:2 * npg]
    o_ref = rest[2 * npg]
    b = pl.program_id(0)
    h = pl.program_id(1)
    slope = slopes_ref[h]
    lane = lax.broadcasted_iota(jnp.int32, (1, PAGE_SIZE), 1)
    q = q_ref[...] * ATT_HD ** -0.5
    scores = []
    for s in range(MOBA_TOPK):
        blk = sel_ref[(b * ATT_HEADS + h) * MOBA_TOPK + s]
        for t in range(PAGES_PER_BLOCK):
            kpos = blk * MOBA_BLOCK + t * PAGE_SIZE + lane
            dist = (past_len - kpos).astype(F32)
            sc = jnp.sum(kp[s * PAGES_PER_BLOCK + t][...] * q, axis=0, keepdims=True)
            scores.append(sc - slope * dist)
    s_own = jnp.sum(q * kn_ref[...], axis=0, keepdims=True)
    m = functools.reduce(jnp.maximum, [jnp.max(sc, axis=1, keepdims=True) for sc in scores] + [s_own])
    p_own = jnp.exp(s_own - m)
    l = p_own
    acc = p_own * vn_ref[...]
    for idx, sc in enumerate(scores):
        pr = jnp.exp(sc - m)
        l = l + jnp.sum(pr, axis=1, keepdims=True)
        acc = acc + jnp.sum(vp[idx][...] * pr, axis=1, keepdims=True)
    o_ref[...] = acc / l


def _moba_sample(cache_kt, cache_vt, layer, page_table_flat, sel_flat, q_col, k_col, v_col, n_pages):
    b = q_col.shape[0]
    npg = MOBA_TOPK * PAGES_PER_BLOCK
    h = jnp.arange(1, ATT_HEADS + 1, dtype=F32)
    slopes = 2.0 ** (-8.0 * h / ATT_HEADS)

    def page_spec(g):
        s, t = divmod(g, PAGES_PER_BLOCK)

        def imap(bi, hi, pt, sel):
            blk = sel[(bi * ATT_HEADS + hi) * MOBA_TOPK + s]
            return (layer, pt[bi * n_pages + blk * PAGES_PER_BLOCK + t], hi, 0)

        return pl.BlockSpec((None, None, ATT_HD, PAGE_SIZE), imap)

    vec = pl.BlockSpec((None, ATT_HD, 1), lambda bi, hi, pt, sel: (bi, hi, 0))
    grid_spec = pltpu.PrefetchScalarGridSpec(
        num_scalar_prefetch=2,
        grid=(b, ATT_HEADS),
        in_specs=[pl.BlockSpec(memory_space=pltpu.SMEM), vec, vec, vec]
                 + [page_spec(g) for g in range(npg)] * 2,
        out_specs=vec,
    )
    return pl.pallas_call(
        functools.partial(_moba_sample_kernel, past_len=n_pages * PAGE_SIZE),
        grid_spec=grid_spec,
        out_shape=jax.ShapeDtypeStruct((b, HEAD_W, 1), F32),
        compiler_params=_params("parallel", "arbitrary"),
        name="moba_sample",
    )(page_table_flat, sel_flat, slopes, q_col, k_col, v_col, *([cache_kt] * npg), *([cache_vt] * npg))


def _ret_sample_kernel(st_ref, q_ref, k_ref, v_ref, rg_ref, gamma_ref, ns_ref, o_ref):
    new = gamma_ref[...] * st_ref[...] + (k_ref[...] * RET_DK ** -0.5) * v_ref[...]
    ns_ref[...] = new
    o = jnp.sum(q_ref[...] * new, axis=1, keepdims=True)
    mu = jnp.mean(o, axis=-1, keepdims=True)
    dlt = o - mu
    var = jnp.mean(dlt * dlt, axis=-1, keepdims=True)
    o_ref[...] = _silu(rg_ref[...]) * (dlt * lax.rsqrt(var + EPS))


def _ret_sample(state, layer, rq, rk, rv, rg):
    b = state.shape[1]
    hh = jnp.arange(RET_HEADS, dtype=F32)
    gamma = jnp.exp(jnp.log1p(-(2.0 ** (-5.0 - hh)))).reshape(RET_HEADS, 1, 1)
    st_in = pl.BlockSpec((None, None, RET_HEADS, RET_DK, RET_DV), lambda i: (layer, i, 0, 0, 0))
    st_out = pl.BlockSpec((None, RET_HEADS, RET_DK, RET_DV), lambda i: (i, 0, 0, 0))
    col = pl.BlockSpec((None, RET_HEADS, RET_DK, 1), lambda i: (i, 0, 0, 0))
    rw = pl.BlockSpec((None, RET_HEADS, 1, RET_DV), lambda i: (i, 0, 0, 0))
    return pl.pallas_call(
        _ret_sample_kernel,
        grid=(b,),
        in_specs=[st_in, col, col, rw, rw, _full((RET_HEADS, 1, 1))],
        out_specs=(st_out, rw),
        out_shape=(jax.ShapeDtypeStruct(state.shape[1:], F32),
                   jax.ShapeDtypeStruct((b, RET_HEADS, 1, RET_DV), F32)),
        compiler_params=_params("parallel"),
        name="retention_sample",
    )(state, rq, rk, rv, rg, gamma)


def kernel(x_prompt, x_sample, cache_k, cache_v, state_ret, page_table, c_prompt, c_sample, w_ada, b_ada, g_mix, w_in, w_ret_o, w_att_o, w_out, g_ffn, w_rg, b_rg, w_re, b_re, w_gate_e, w_up_e, w_down_e, g_final):
    bp, s, d = x_prompt.shape
    bs, ds, _ = x_sample.shape
    assert bp == 1 and ds == 1
    depth = w_ada.shape[0]
    n_pool = cache_k.shape[1]
    n_pages = page_table.shape[1]
    w = HEAD_W

    n_rows = bp + bs
    pad = (-n_rows) % SUBLANES
    c_all = jnp.concatenate([c_prompt, c_sample, jnp.zeros((pad, d), F32)], axis=0)
    mods = _ada(c_all, w_ada, b_ada)

    cache_kt = cache_k.transpose(0, 1, 3, 4, 2).reshape(depth, n_pool, w, PAGE_SIZE)
    cache_vt = cache_v.transpose(0, 1, 3, 4, 2).reshape(depth, n_pool, w, PAGE_SIZE)
    pt_flat = page_table.reshape(-1)

    n_log = N_GROUPS + N_EXPERTS
    rr = 32
    g_fin = g_final.reshape(1, d)

    w_in_all = w_in.astype(BF16)
    wg_all = w_gate_e.astype(BF16)
    wu_all = w_up_e.astype(BF16)
    wd_all = w_down_e.astype(BF16)

    xp = x_prompt.reshape(s, d)
    xs = x_sample.reshape(bs, d)
    kp_l, vp_l, rp_l, ks_l, vs_l, rs_l = [], [], [], [], [], []
    for l in range(depth):
        last = l == depth - 1
        mp = [mods[l, 0:bp, j * d:(j + 1) * d] for j in range(6)]
        ms = [mods[l, bp:bp + bs, j * d:(j + 1) * d] for j in range(6)]
        w_kvt_b = w_in[l][:, 5 * w:7 * w].T.astype(BF16)
        wro_b = w_ret_o[l].astype(BF16)
        wao_b = w_att_o[l].astype(BF16)
        wo_b = w_out[l].astype(BF16)
        wr_t = jnp.concatenate([w_rg[l].T, w_re[l].T, jnp.zeros((rr - n_log, d), F32)], axis=0)
        br_col = jnp.concatenate([b_rg[l], b_re[l], jnp.zeros((rr - n_log,), F32)]).reshape(rr, 1)
        gm = g_mix[l].reshape(1, d)
        gf = g_ffn[l].reshape(1, d)

        proj = _proj_sample(xs, gm, ms[0], ms[1], w_in, l)
        cut = lambda j, n=1: proj[:, j * w:(j + n) * w]
        rq_s, rk_s, rv_s, rg_s, aq_s, ak_s, av_s = (cut(j) for j in range(7))
        gr_s, ga_s = cut(7, 2), cut(9, 2)
        q_col = aq_s.reshape(bs, w, 1)

        rq, rk, rv, rg, aq, akb, akt, avt, avtb, gr, ga, qc = _proj_prompt(xp, gm, mp[0], mp[1], w_in_all, l,
                                                                           w_kvt_b)
        ret_in, st = _retention_prompt(rq, rk, rv, rg, chunk=MOBA_BLOCK)
        att_in = _moba_prompt(aq, qc, akb, avtb)
        x1, h2, wts, wtst, gcnt = _merge_router(xp, ret_in, att_in, gr, ga, mp[2], wro_b, wao_b, wo_b, gf,
                                                mp[3], mp[4], wr_t, br_col, tm=512, exact=0)
        xp, sel_s = _moe_sorted(x1, h2, wts, wtst, gcnt, mp[5], wg_all, wu_all, wd_all, l, g_fin, tm=1024,
                                final_norm=last, paged=(cache_kt, l, pt_flat, q_col, n_pages))
        kp_l.append(akt)
        vp_l.append(avt)
        rp_l.append(st.reshape(bp, RET_HEADS, RET_DK, RET_DV))

        new_state, ret_s = _ret_sample(
            state_ret, l,
            rq_s.reshape(bs, RET_HEADS, RET_DK, 1), rk_s.reshape(bs, RET_HEADS, RET_DK, 1),
            rv_s.reshape(bs, RET_HEADS, 1, RET_DV), rg_s.reshape(bs, RET_HEADS, 1, RET_DV))
        sel_flat = sel_s[:, :MOBA_TOPK, :ATT_HEADS].transpose(0, 2, 1).reshape(-1)
        att_s = _moba_sample(cache_kt, cache_vt, l, pt_flat, sel_flat, q_col, ak_s.reshape(bs, w, 1),
                             av_s.reshape(bs, w, 1), n_pages)
        x1s, h2s, wts_s, _, _ = _merge_router(xs, ret_s.reshape(bs, w), att_s.reshape(bs, w), gr_s, ga_s, ms[2],
                                              w_ret_o, w_att_o, w_out, gf, ms[3], ms[4], wr_t, br_col,
                                              tm=bs, exact=l + 1)
        xs = _moe(x1s, h2s, wts_s, ms[5], w_gate_e, w_up_e, w_down_e, l, g_fin, tm=bs, final_norm=last,
                  exact=True)
        ks_l.append(ak_s.reshape(bs, 1, ATT_HEADS, ATT_HD))
        vs_l.append(av_s.reshape(bs, 1, ATT_HEADS, ATT_HD))
        rs_l.append(new_state)

    def kv_out(parts):
        return jnp.stack(parts).reshape(depth, bp, ATT_HEADS, ATT_HD, s).transpose(0, 1, 4, 2, 3)

    return (xp.reshape(bp, s, d), xs.reshape(bs, 1, d), kv_out(kp_l), kv_out(vp_l), jnp.stack(rp_l),
            jnp.stack(ks_l), jnp.stack(vs_l), jnp.stack(rs_l))
```

```python
import functools

import jax
import jax.numpy as jnp
from jax import lax
from jax.experimental import pallas as pl
from jax.experimental.pallas import tpu as pltpu

EPS = 1e-6
RET_HEADS = 8
RET_DK = 64
RET_DV = 64
ATT_HEADS = 8
ATT_HD = 64
HEAD_W = RET_HEADS * RET_DK
MOBA_BLOCK = 256
MOBA_TOPK = 3
PAGE_SIZE = 128
PAGES_PER_BLOCK = MOBA_BLOCK // PAGE_SIZE
N_GROUPS = 4
EXP_PER_GROUP = 4
N_EXPERTS = N_GROUPS * EXP_PER_GROUP
SUBLANES = 8
LANES = 128
PAIRS = HEAD_W // LANES
NEG = -1e30
ALIBI_SLOPES = tuple(2.0 ** (-8.0 * (j + 1) / ATT_HEADS) for j in range(ATT_HEADS))
PAGES_PER_STEP = 16
MOE_CHUNK = 128
MOE_SORT_ROWS = 256
VMEM_LIMIT = 56 * 1024 * 1024

F32 = jnp.float32
BF16 = jnp.bfloat16
HI = lax.Precision.HIGHEST
NT = (((1,), (1,)), ((), ()))
TN = (((0,), (0,)), ((), ()))


def _dot(a, b):
    return jnp.dot(a, b, preferred_element_type=F32)


def _dot_nt(a, b):
    return lax.dot_general(a, b, NT, preferred_element_type=F32)


def _dot_hi(a, b):
    return jnp.dot(a, b, preferred_element_type=F32, precision=HI)


def _dot_nt_hi(a, b):
    return lax.dot_general(a, b, NT, preferred_element_type=F32, precision=HI)


def _mm(a, w_ref, exact):
    if exact:
        return _dot_hi(a.astype(F32), w_ref[...])
    return _dot(a.astype(BF16), w_ref[...].astype(BF16))


def _dot_split(a, b_bf16, lhs_exact=False):
    hi = a.astype(BF16)
    lo = (a - hi.astype(F32)).astype(BF16)
    if lhs_exact:
        return _dot(b_bf16, hi) + _dot(b_bf16, lo)
    return _dot(hi, b_bf16) + _dot(lo, b_bf16)


def _params(*sem):
    return pltpu.CompilerParams(dimension_semantics=sem, vmem_limit_bytes=VMEM_LIMIT)


def _full(shape):
    n = len(shape)
    return pl.BlockSpec(shape, lambda *_: (0,) * n)


def _modulate(x, g, shift, scale):
    ms = jnp.mean(x * x, axis=-1, keepdims=True)
    y = x * lax.rsqrt(ms + EPS) * g
    return y * (1.0 + scale) + shift


def _silu(x):
    return x * jax.nn.sigmoid(x)


def _head_lane_mask(shape, a):
    lane = lax.broadcasted_iota(jnp.int32, shape, len(shape) - 1)
    return (lane // RET_DK) == a


def _top3_rows(g, n_rows):
    row = lax.broadcasted_iota(jnp.int32, g.shape, 0)
    sel = jnp.zeros(g.shape, F32)
    idxs = []
    for _ in range(MOBA_TOPK):
        cmax = jnp.max(g, axis=0, keepdims=True)
        idx = jnp.min(jnp.where(g == cmax, row, n_rows), axis=0, keepdims=True)
        pick = jnp.logical_and(row == idx, cmax > -jnp.inf)
        sel = jnp.where(pick, 1.0, sel)
        g = jnp.where(pick, -jnp.inf, g)
        idxs.append(idx)
    return sel, idxs


def _ada_kernel(c_ref, w_ref, b_ref, o_ref):
    c = c_ref[...]
    o_ref[...] = _dot_hi(_silu(c), w_ref[...]) + b_ref[...]


def _ada(c_all, w_ada, b_ada):
    depth, d, d6 = w_ada.shape
    r = c_all.shape[0]
    return pl.pallas_call(
        _ada_kernel,
        grid=(depth, d6 // d),
        in_specs=[
            pl.BlockSpec((r, d), lambda l, j: (0, 0)),
            pl.BlockSpec((None, d, d), lambda l, j: (l, 0, j)),
            pl.BlockSpec((None, 1, d), lambda l, j: (l, 0, j)),
        ],
        out_specs=pl.BlockSpec((None, r, d), lambda l, j: (l, 0, j)),
        out_shape=jax.ShapeDtypeStruct((depth, r, d6), F32),
        compiler_params=_params("parallel", "parallel"),
        name="ada_mod",
    )(c_all, w_ada, b_ada.reshape(depth, 1, d6))


def _proj_prompt_kernel(x_ref, g_ref, sh_ref, sc_ref, w_ref, wkvt_ref,
                        rq_ref, rk_ref, rv_ref, rg_ref, aq_ref, akb_ref, akt_ref, avt_ref, avtb_ref,
                        gr_ref, ga_ref, qc_ref, means_ref, *, slopes):
    i = pl.program_id(0)
    nb = means_ref.shape[0]
    tm = x_ref.shape[0]
    w = HEAD_W

    @pl.when(i == 0)
    def _():
        means_ref[...] = jnp.zeros_like(means_ref)

    hb = _modulate(x_ref[...], g_ref[...], sh_ref[...], sc_ref[...]).astype(BF16)

    def col(c, n=1):
        return _dot(hb, w_ref[:, c * w:(c + n) * w])

    rq_ref[...] = col(0).astype(BF16)
    rk_ref[...] = (col(1) * RET_DK ** -0.5).astype(BF16)
    rv_ref[...] = col(2).astype(BF16)
    rg_ref[...] = col(3)
    q = col(4) * ATT_HD ** -0.5
    aq_ref[...] = q.astype(BF16)
    k = col(5)
    akb_ref[...] = k.astype(BF16)
    gr_ref[...] = col(7, 2)
    ga_ref[...] = col(9, 2)
    kvt = _dot_nt(wkvt_ref[...], hb)
    akt_ref[...] = kvt[:w]
    avt_ref[...] = kvt[w:]
    avtb_ref[...] = kvt[w:].astype(BF16)

    blk = lax.broadcasted_iota(jnp.int32, (nb, tm), 0)
    valid = blk < i
    blk_dist = (MOBA_BLOCK * (i - blk)).astype(F32)
    xrow = lax.broadcasted_iota(jnp.int32, (LANES - nb, tm), 0)
    xlane = lax.broadcasted_iota(jnp.int32, (LANES - nb, tm), 1).astype(F32)
    for p in range(PAIRS):
        mp = means_ref[:, p * LANES:(p + 1) * LANES]
        qp = q[:, p * LANES:(p + 1) * LANES]
        for a in range(2):
            slope = slopes[2 * p + a]
            qm = jnp.where(_head_lane_mask(qp.shape, a), qp, 0.0)
            gate = jnp.where(valid, _dot_nt_hi(mp, qm), -jnp.inf)
            sel, _ = _top3_rows(gate, nb)
            ct = jnp.where(sel > 0.5, -slope * blk_dist, NEG)
            extra = jnp.where(xrow == 0, -slope * xlane, jnp.where(xrow == 1, slope, 0.0))
            qc_ref[2 * p + a] = jnp.concatenate([ct, extra], axis=0).T.astype(BF16)

    means_ref[pl.ds(i, 1), :] = jnp.mean(k, axis=0, keepdims=True)


def _proj_prompt(x, g, shift, scale, w_in_b, layer, w_kvt_b):
    s, d = x.shape
    tm = MOBA_BLOCK
    assert s % tm == 0
    nb = s // tm
    w = HEAD_W
    n_in = w_in_b.shape[2]
    row = lambda width: pl.BlockSpec((tm, width), lambda i: (i, 0))
    colblk = pl.BlockSpec((w, tm), lambda i: (0, i))
    vec = pl.BlockSpec((1, d), lambda i: (0, 0))
    out_shape = (
        jax.ShapeDtypeStruct((s, w), BF16),
        jax.ShapeDtypeStruct((s, w), BF16),
        jax.ShapeDtypeStruct((s, w), BF16),
        jax.ShapeDtypeStruct((s, w), F32),
        jax.ShapeDtypeStruct((s, w), BF16),
        jax.ShapeDtypeStruct((nb, tm, w), BF16),
        jax.ShapeDtypeStruct((w, s), F32),
        jax.ShapeDtypeStruct((w, s), F32),
        jax.ShapeDtypeStruct((nb, w, tm), BF16),
        jax.ShapeDtypeStruct((s, d), F32),
        jax.ShapeDtypeStruct((s, d), F32),
        jax.ShapeDtypeStruct((ATT_HEADS, s, LANES), BF16),
    )
    assert nb + 2 <= LANES
    out_specs = (
        row(w), row(w), row(w), row(w), row(w),
        pl.BlockSpec((None, tm, w), lambda i: (i, 0, 0)),
        colblk, colblk,
        pl.BlockSpec((None, w, tm), lambda i: (i, 0, 0)),
        row(d), row(d),
        pl.BlockSpec((ATT_HEADS, tm, LANES), lambda i: (0, i, 0)),
    )
    return pl.pallas_call(
        functools.partial(_proj_prompt_kernel, slopes=ALIBI_SLOPES),
        grid=(nb,),
        in_specs=[row(d), vec, vec, vec,
                  pl.BlockSpec((None, d, n_in), lambda i: (layer, 0, 0), pipeline_mode=pl.Buffered(1)),
                  pl.BlockSpec((2 * w, d), lambda i: (0, 0), pipeline_mode=pl.Buffered(1))],
        out_specs=out_specs,
        out_shape=out_shape,
        scratch_shapes=[pltpu.VMEM((nb, w), F32)],
        compiler_params=_params("arbitrary"),
        name="proj_prompt",
    )(x, g, shift, scale, w_in_b, w_kvt_b)


def _proj_sample_kernel(x_ref, g_ref, sh_ref, sc_ref, w_ref, o_ref):
    h = _modulate(x_ref[...], g_ref[...], sh_ref[...], sc_ref[...])
    o_ref[...] = _dot_hi(h, w_ref[...])


def _proj_sample(x, g, shift, scale, w_in, layer):
    b, d = x.shape
    n_in = w_in.shape[2]
    tn = HEAD_W
    return pl.pallas_call(
        _proj_sample_kernel,
        grid=(n_in // tn,),
        in_specs=[_full((b, d)), _full((1, d)), _full((b, d)), _full((b, d)),
                  pl.BlockSpec((None, d, tn), lambda j: (layer, 0, j))],
        out_specs=pl.BlockSpec((b, tn), lambda j: (0, j)),
        out_shape=jax.ShapeDtypeStruct((b, n_in), F32),
        compiler_params=_params("parallel"),
        name="proj_sample",
    )(x, g, shift, scale, w_in)


def _retention_kernel(q_ref, k_ref, v_ref, rg_ref, decay_ref, xi_ref, zeta_ref, gc_ref, avg_ref,
                      o_ref, st_ref):
    i = pl.program_id(0)

    @pl.when(i == 0)
    def _():
        st_ref[...] = jnp.zeros_like(st_ref)

    c = q_ref.shape[0]
    row = lax.broadcasted_iota(jnp.int32, (LANES, LANES), 0)
    lane = lax.broadcasted_iota(jnp.int32, (LANES, LANES), 1)
    same_head = (row // RET_DK) == (lane // RET_DV)
    avg = avg_ref[...]
    for p in range(PAIRS):
        sl = slice(p * LANES, (p + 1) * LANES)
        q = q_ref[:, sl]
        k = k_ref[:, sl]
        v = v_ref[:, sl]
        state = st_ref[p]
        cross = _dot(q, state.astype(BF16)) * xi_ref[:, sl]
        inner = []
        for a in range(2):
            qm = jnp.where(_head_lane_mask(q.shape, a), q, jnp.zeros_like(q))
            scores = _dot_nt(qm, k) * decay_ref[2 * p + a]
            inner.append(_dot(scores.astype(BF16), v))
        o = jnp.where(_head_lane_mask((c, LANES), 0), inner[0], inner[1]) + cross
        kz = (k.astype(F32) * zeta_ref[:, sl]).astype(BF16)
        upd = lax.dot_general(kz, v, TN, preferred_element_type=F32)
        st_ref[p] = gc_ref[:, sl] * state + jnp.where(same_head, upd, 0.0)
        mu = _dot_split(o, avg)
        dlt = o - mu
        var = _dot_split(dlt * dlt, avg)
        hn = dlt * lax.rsqrt(var + EPS)
        o_ref[:, sl] = (_silu(rg_ref[:, sl]) * hn).astype(BF16)


def _ret_tables(c):
    h = jnp.arange(RET_HEADS, dtype=F32)
    log_gamma = jnp.log1p(-(2.0 ** (-5.0 - h)))
    idx = jnp.arange(c, dtype=F32)
    diff = idx[:, None] - idx[None, :]
    decay = jnp.where(diff >= 0, jnp.exp(log_gamma[:, None, None] * jnp.maximum(diff, 0.0)), 0.0)
    xi = jnp.exp(log_gamma[None, :] * (idx[:, None] + 1.0))
    zeta = jnp.exp(log_gamma[None, :] * (c - 1.0 - idx[:, None]))
    gc = jnp.exp(log_gamma * c)[None, :]
    rep = lambda t: jnp.repeat(t, RET_DV, axis=1)
    return decay, rep(xi), rep(zeta), rep(gc)


def _head_avg_matrix():
    r = jnp.arange(LANES)
    return jnp.where((r[:, None] // RET_DV) == (r[None, :] // RET_DV), 1.0 / RET_DV, 0.0).astype(BF16)


def _retention_prompt(rq, rk, rv, rg, chunk):
    s, w = rq.shape
    assert s % chunk == 0
    decay, xi, zeta, gc = _ret_tables(chunk)
    row = pl.BlockSpec((chunk, w), lambda i: (i, 0))
    o, st = pl.pallas_call(
        _retention_kernel,
        grid=(s // chunk,),
        in_specs=[row, row, row, row,
                  _full((RET_HEADS, chunk, chunk)), _full((chunk, w)), _full((chunk, w)), _full((1, w)),
                  _full((LANES, LANES))],
        out_specs=(row, _full((PAIRS, LANES, LANES))),
        out_shape=(jax.ShapeDtypeStruct((s, w), BF16), jax.ShapeDtypeStruct((PAIRS, LANES, LANES), F32)),
        compiler_params=_params("arbitrary"),
        name="retention_prompt",
    )(rq, rk, rv, rg, decay, xi, zeta, gc, _head_avg_matrix())
    st = st.reshape(PAIRS, 2, RET_DK, 2, RET_DV)
    st = jnp.stack([st[:, 0, :, 0, :], st[:, 1, :, 1, :]], axis=1).reshape(RET_HEADS, RET_DK, RET_DV)
    return o, st


def _moba_prompt_kernel(q_ref, qc_ref, k_ref, vt_ref, ctab_ref, cown_ref, o_ref,
                        qcat_ref, sa_ref, sb_ref, sc_ref, sd_ref, ma_ref, mb_ref, mc_ref, md_ref,
                        m_ref, l_ref, acc_ref):
    i = pl.program_id(0)
    tq = q_ref.shape[0]
    tk = k_ref.shape[1]
    nb = k_ref.shape[0]
    heads = range(ATT_HEADS)
    hd = ATT_HD

    for p in range(PAIRS):
        qp = q_ref[:, p * LANES:(p + 1) * LANES]
        for a in range(2):
            h = 2 * p + a
            qm = jnp.where(_head_lane_mask(qp.shape, a), qp, jnp.zeros_like(qp))
            qcat_ref[h] = jnp.concatenate([qm, qc_ref[h]], axis=1)

    ones = jnp.ones((2 * SUBLANES, tk), BF16)

    def scores2(n_a, n_b, h, ra_ref, rb_ref, mxa_ref, mxb_ref):
        cols = slice((h // 2) * LANES, (h // 2 + 1) * LANES)
        kcat = jnp.concatenate([jnp.concatenate([k_ref[n_a, :, cols], ctab_ref[n_a]], axis=1),
                                jnp.concatenate([k_ref[n_b, :, cols], ctab_ref[n_b]], axis=1)], axis=0)
        s2 = _dot_nt(kcat, qcat_ref[h])
        for r_ref, mx_ref, rows in ((ra_ref, mxa_ref, slice(0, tk)), (rb_ref, mxb_ref, slice(tk, 2 * tk))):
            r_ref[h] = s2[rows]
            mx_ref[h:h + 1, :] = jnp.max(s2[rows], axis=0, keepdims=True)

    def attend(n, h, s_ref, mx_ref):
        s = s_ref[h]
        m = m_ref[h:h + 1, :]
        m_new = jnp.maximum(m, mx_ref[h:h + 1, :])
        alpha = jnp.exp(m - m_new)
        pr = jnp.exp((s - m_new).astype(BF16))
        rows = slice(h * hd, (h + 1) * hd)
        pv = _dot(jnp.concatenate([vt_ref[n, rows, :], ones], axis=0), pr)
        m_ref[h:h + 1, :] = m_new
        l_ref[h:h + 1, :] = alpha * l_ref[h:h + 1, :] + pv[hd:hd + 1, :]
        acc_ref[rows, :] = alpha * acc_ref[rows, :] + pv[:hd, :]

    kk = lax.broadcasted_iota(jnp.int32, (tk, tq), 0)
    qq = lax.broadcasted_iota(jnp.int32, (tk, tq), 1)
    causal = kk <= qq
    m_ref[...] = jnp.full(m_ref.shape, NEG, F32)
    l_ref[...] = jnp.zeros_like(l_ref)
    acc_ref[...] = jnp.zeros_like(acc_ref)
    for h in heads:
        p = h // 2
        kcat = jnp.concatenate([k_ref[i, :, p * LANES:(p + 1) * LANES], cown_ref[...]], axis=1)
        s = jnp.where(causal, _dot_nt(kcat, qcat_ref[h]), NEG)
        sc_ref[h] = s
        mc_ref[h:h + 1, :] = jnp.max(s, axis=0, keepdims=True)
        scores2(0, 1, h, sa_ref, sb_ref, ma_ref, mb_ref)
    for h in heads:
        attend(i, h, sc_ref, mc_ref)

    @pl.loop(0, (i + 3) // 4)
    def _(t):
        n0 = 4 * t
        for h in heads:
            scores2(n0 + 2, n0 + 3, h, sc_ref, sd_ref, mc_ref, md_ref)
            attend(n0, h, sa_ref, ma_ref)
            attend(n0 + 1, h, sb_ref, mb_ref)
        for h in heads:
            scores2(jnp.minimum(n0 + 4, nb - 1), jnp.minimum(n0 + 5, nb - 1), h, sa_ref, sb_ref, ma_ref, mb_ref)
            attend(n0 + 2, h, sc_ref, mc_ref)
            attend(n0 + 3, h, sd_ref, md_ref)

    for h in heads:
        rows = slice(h * hd, (h + 1) * hd)
        acc_ref[rows, :] = acc_ref[rows, :] / l_ref[h:h + 1, :]
    o_ref[...] = acc_ref[...].T.astype(BF16)


def _moba_bias_table(nb, tk):
    lane = jnp.arange(LANES)[None, None, :]
    blk = jnp.arange(nb)[:, None, None]
    kk = jnp.arange(tk, dtype=F32)[None, :, None]
    t = jnp.where(lane == blk, 1.0, 0.0)
    t = jnp.where(lane == nb, 1.0, t)
    return jnp.where(lane == nb + 1, kk, t).astype(BF16)


def _moba_prompt(aq, qc, akb, avtb):
    s, w = aq.shape
    nb, tk, _ = akb.shape
    tq = MOBA_BLOCK
    assert nb % 4 == 0 and tk == tq
    ctab = _moba_bias_table(nb, tk)
    cown = jnp.where(jnp.arange(LANES)[None, :] < nb, 0, ctab[0])
    once = dict(pipeline_mode=pl.Buffered(1))
    return pl.pallas_call(
        _moba_prompt_kernel,
        grid=(s // tq,),
        in_specs=[
            pl.BlockSpec((tq, w), lambda i: (i, 0)),
            pl.BlockSpec((ATT_HEADS, tq, LANES), lambda i: (0, i, 0)),
            pl.BlockSpec((nb, tk, w), lambda i: (0, 0, 0), **once),
            pl.BlockSpec((nb, w, tk), lambda i: (0, 0, 0), **once),
            pl.BlockSpec((nb, tk, LANES), lambda i: (0, 0, 0), **once),
            pl.BlockSpec((tk, LANES), lambda i: (0, 0), **once),
        ],
        out_specs=pl.BlockSpec((tq, w), lambda i: (i, 0)),
        out_shape=jax.ShapeDtypeStruct((s, w), BF16),
        scratch_shapes=[
            pltpu.VMEM((ATT_HEADS, tq, 2 * LANES), BF16),
            *[pltpu.VMEM((ATT_HEADS, tk, tq), F32)] * 4,
            *[pltpu.VMEM((ATT_HEADS, tq), F32)] * 4,
            pltpu.VMEM((ATT_HEADS, tq), F32),
            pltpu.VMEM((ATT_HEADS, tq), F32),
            pltpu.VMEM((w, tq), F32),
        ],
        compiler_params=_params("arbitrary"),
        name="moba_prompt",
    )(aq, qc, akb, avtb, ctab, cown)


def _merge_router_kernel(x_ref, ret_ref, att_ref, gr_ref, ga_ref, g1_ref, wro_ref, wao_ref, wo_ref,
                         gf_ref, sh_ref, sc_ref, wr_ref, br_ref,
                         x1_ref, h2_ref, wts_ref, wtst_ref, gcnt_ref, *, exact):
    ret = _mm(ret_ref[...], wro_ref, exact)
    att = _mm(att_ref[...], wao_ref, exact)
    merged = jax.nn.sigmoid(gr_ref[...]) * ret + jax.nn.sigmoid(ga_ref[...]) * att
    x1 = x_ref[...] + g1_ref[...] * _mm(merged, wo_ref, exact)
    x1_ref[...] = x1
    h2 = _modulate(x1, gf_ref[...], sh_ref[...], sc_ref[...])
    h2_ref[...] = h2.astype(h2_ref.dtype)

    logits = _dot_nt_hi(wr_ref[...], h2) + br_ref[...]
    lg = [logits[g:g + 1, :] for g in range(N_GROUPS)]
    le = [logits[N_GROUPS + e:N_GROUPS + e + 1, :] for e in range(N_EXPERTS)]
    mg = functools.reduce(jnp.maximum, lg)
    eg = [jnp.exp(v - mg) for v in lg]
    den = functools.reduce(jnp.add, eg)
    pg = [v / den for v in eg]
    p_top = functools.reduce(jnp.maximum, pg)
    taken = jnp.zeros(p_top.shape, jnp.bool_)
    onehot = []
    for g in range(N_GROUPS):
        hit = jnp.logical_and(pg[g] == p_top, jnp.logical_not(taken))
        onehot.append(hit)
        taken = jnp.logical_or(taken, hit)
    vin = []
    for j in range(EXP_PER_GROUP):
        v = jnp.zeros_like(p_top)
        for g in range(N_GROUPS):
            v = jnp.where(onehot[g], le[g * EXP_PER_GROUP + j], v)
        vin.append(v)
    rank = []
    for j in range(EXP_PER_GROUP):
        r = jnp.zeros(p_top.shape, jnp.int32)
        for kx in range(EXP_PER_GROUP):
            if kx == j:
                continue
            ahead = (vin[kx] > vin[j]) | ((vin[kx] == vin[j]) & (kx < j))
            r = r + ahead.astype(jnp.int32)
        rank.append(r)
    v1 = functools.reduce(jnp.maximum, vin)
    v2 = jnp.full_like(v1, -jnp.inf)
    for j in range(EXP_PER_GROUP):
        v2 = jnp.where(rank[j] == 1, vin[j], v2)
    e2 = jnp.exp(v2 - v1)
    w_first = 1.0 / (1.0 + e2)
    w_second = e2 / (1.0 + e2)
    wtst_ref[...] = jnp.zeros_like(wtst_ref)
    row8 = lax.broadcasted_iota(jnp.int32, gcnt_ref.shape, 0)
    lane8 = lax.broadcasted_iota(jnp.int32, gcnt_ref.shape, 1)
    gcnt = jnp.zeros(gcnt_ref.shape, F32)
    for g in range(N_GROUPS):
        hot = jnp.where(onehot[g], 1.0, 0.0)
        wtst_ref[N_EXPERTS + g:N_EXPERTS + g + 1, :] = hot
        n_g = jnp.sum(hot, axis=1, keepdims=True)
        gcnt = jnp.where(jnp.logical_and(row8 == 0, lane8 == g), n_g, gcnt)
        for j in range(EXP_PER_GROUP):
            wj = jnp.where(rank[j] == 0, w_first, jnp.where(rank[j] == 1, w_second, 0.0)) * p_top
            e = g * EXP_PER_GROUP + j
            wtst_ref[e:e + 1, :] = jnp.where(onehot[g], wj, 0.0)
    wts_ref[...] = wtst_ref[...].T
    gcnt_ref[...] = gcnt.astype(jnp.int32)


def _merge_router(x, ret_in, att_in, gr, ga, gate1, wro, wao, wo, g_ffn, shift2, scale2, wr_t, br_col, tm, exact):
    s, d = x.shape
    w = HEAD_W
    assert s % tm == 0
    row = lambda width: pl.BlockSpec((tm, width), lambda i: (i, 0))
    mod = lambda a: pl.BlockSpec((tm if a.shape[0] > 1 else 1, d), lambda i: (i if a.shape[0] > 1 else 0, 0))
    if exact:
        layer = exact - 1
        wspec = lambda a: pl.BlockSpec((None,) + a.shape[1:], lambda i: (layer, 0, 0))
    else:
        wspec = lambda a: _full(a.shape)
    rr = wr_t.shape[0]
    return pl.pallas_call(
        functools.partial(_merge_router_kernel, exact=bool(exact)),
        grid=(s // tm,),
        in_specs=[row(d), row(w), row(w), row(d), row(d), mod(gate1),
                  wspec(wro), wspec(wao), wspec(wo),
                  _full((1, d)), mod(shift2), mod(scale2), _full((rr, d)), _full((rr, 1))],
        out_specs=(row(d), row(d), row(LANES),
                   pl.BlockSpec((LANES, tm), lambda i: (0, i)),
                   pl.BlockSpec((None, SUBLANES, LANES), lambda i: (i, 0, 0))),
        out_shape=(jax.ShapeDtypeStruct((s, d), F32), jax.ShapeDtypeStruct((s, d), F32 if exact else BF16),
                   jax.ShapeDtypeStruct((s, LANES), F32),
                   jax.ShapeDtypeStruct((LANES, s), F32),
                   jax.ShapeDtypeStruct((s // tm, SUBLANES, LANES), jnp.int32)),
        compiler_params=_params("parallel"),
        name="merge_router",
    )(x, ret_in, att_in, gr, ga, gate1, wro, wao, wo, g_ffn, shift2, scale2, wr_t, br_col)


def _moe_kernel(x1_ref, h2_ref, wts_ref, g2_ref, wg_ref, wu_ref, wd_ref, gfin_ref, o_ref, acc_ref,
                *, final_norm, exact):
    e = pl.program_id(1)

    @pl.when(e == 0)
    def _():
        acc_ref[...] = jnp.zeros_like(acc_ref)

    h2 = h2_ref[...]
    wts = wts_ref[...]
    lane = lax.broadcasted_iota(jnp.int32, wts.shape, 1)
    wcol = jnp.sum(jnp.where(lane == e, wts, 0.0), axis=1, keepdims=True)
    act = _silu(_mm(h2, wg_ref, exact)) * _mm(h2, wu_ref, exact) * wcol
    acc_ref[...] += _mm(act, wd_ref, exact)

    @pl.when(e == pl.num_programs(1) - 1)
    def _():
        x2 = x1_ref[...] + g2_ref[...] * acc_ref[...]
        if final_norm:
            ms = jnp.mean(x2 * x2, axis=-1, keepdims=True)
            x2 = x2 * lax.rsqrt(ms + EPS) * gfin_ref[...]
        o_ref[...] = x2


def _moe(x1, h2, wts, gate2, wg, wu, wd, layer, g_final, tm, final_norm, exact):
    s, d = x1.shape
    ne, f = wg.shape[-3], wg.shape[-1]
    assert s % tm == 0
    row = lambda width: pl.BlockSpec((tm, width), lambda i, e: (i, 0))
    mod = lambda a: pl.BlockSpec((tm if a.shape[0] > 1 else 1, d), lambda i, e: (i if a.shape[0] > 1 else 0, 0))
    wspec = lambda r, c: pl.BlockSpec((None, None, r, c), lambda i, e: (layer, e, 0, 0))
    return pl.pallas_call(
        functools.partial(_moe_kernel, final_norm=final_norm, exact=exact),
        grid=(s // tm, ne),
        in_specs=[row(d), row(d), row(LANES), mod(gate2),
                  wspec(d, f), wspec(d, f), wspec(f, d),
                  pl.BlockSpec((1, d), lambda i, e: (0, 0))],
        out_specs=row(d),
        out_shape=jax.ShapeDtypeStruct((s, d), F32),
        scratch_shapes=[pltpu.VMEM((tm, d), F32)],
        compiler_params=_params("parallel", "arbitrary"),
        name="moe_experts",
    )(x1, h2, wts, gate2, wg, wu, wd, g_final)


def _moe_sort(starts, h2_ref, wts_ref, wtst_ref, p_scr, pt_scr, hs_scr, ws_scr, acc_scr):
    tm = h2_ref.shape[0]
    ch = MOE_SORT_ROWS
    lo, hi = N_EXPERTS, N_EXPERTS + N_GROUPS
    wts = wts_ref[...]
    lane_t = lax.broadcasted_iota(jnp.int32, wts.shape, 1)
    hot_t = jnp.where(jnp.logical_and(lane_t >= lo, lane_t < hi), wts, 0.0)
    wtst = wtst_ref[...]
    row_l = lax.broadcasted_iota(jnp.int32, wtst.shape, 0)
    hot_l = jnp.where(jnp.logical_and(row_l >= lo, row_l < hi), wtst, 0.0)
    start_t = jnp.zeros((1, LANES), F32)
    start_l = jnp.zeros((LANES, 1), F32)
    lane1 = lax.broadcasted_iota(jnp.int32, (1, LANES), 1)
    row1 = lax.broadcasted_iota(jnp.int32, (LANES, 1), 0)
    for g in range(N_GROUPS):
        sg = starts[g].astype(F32)
        start_t = jnp.where(lane1 == lo + g, sg, start_t)
        start_l = jnp.where(row1 == lo + g, sg, start_l)
    hot_t_b = hot_t.astype(BF16)
    hot_l_b = hot_l.astype(BF16)
    for c in range(tm // ch):
        sl = slice(c * ch, (c + 1) * ch)
        tok = lax.broadcasted_iota(jnp.int32, (ch, tm), 0) + c * ch
        other = lax.broadcasted_iota(jnp.int32, (ch, tm), 1)
        earlier = jnp.where(other < tok, 1.0, 0.0).astype(BF16)
        rank = _dot(earlier, hot_t_b)
        pos = jnp.sum(hot_t[sl] * (rank + start_t), axis=1, keepdims=True).astype(jnp.int32)
        pt_scr[sl, :] = jnp.where(other == pos, 1.0, 0.0).astype(BF16)
        tok = lax.broadcasted_iota(jnp.int32, (tm, ch), 1) + c * ch
        other = lax.broadcasted_iota(jnp.int32, (tm, ch), 0)
        earlier = jnp.where(other < tok, 1.0, 0.0).astype(BF16)
        rank = _dot(hot_l_b, earlier)
        pos = jnp.sum(hot_l[:, sl] * (rank + start_l), axis=0, keepdims=True).astype(jnp.int32)
        p_scr[:, sl] = jnp.where(other == pos, 1.0, 0.0).astype(BF16)
    p = p_scr[...]
    hs_scr[...] = _dot(p, h2_ref[...]).astype(BF16)
    ws_scr[...] = _dot_split(wts, p, lhs_exact=True)
    acc_scr[...] = jnp.zeros_like(acc_scr)


def _moe_sorted_kernel(cnt_ref, *refs, final_norm, router_tiles):
    _moe_sorted_body(cnt_ref, *refs, final_norm=final_norm, router_tiles=router_tiles)


def _moe_sorted_paged_kernel(cnt_ref, ptab_ref, *refs, final_norm, router_tiles, paged_steps, steps_per_seq):
    n_moe_in = 9
    moe_in, q_ref = refs[:n_moe_in], refs[n_moe_in]
    pages = refs[n_moe_in + 1:n_moe_in + 1 + PAGES_PER_STEP]
    o_ref, sel_ref, p_scr, pt_scr, hs_scr, ws_scr, acc_scr, qb_ref, gacc_ref = refs[n_moe_in + 1 + PAGES_PER_STEP:]
    _moe_sorted_body(cnt_ref, *moe_in, o_ref, p_scr, pt_scr, hs_scr, ws_scr, acc_scr,
                     final_norm=final_norm, router_tiles=router_tiles)
    step = pl.program_id(0) * pl.num_programs(1) + pl.program_id(1)

    @pl.when(step < paged_steps)
    def _():
        _paged_gate_step(step % steps_per_seq, steps_per_seq - 1, q_ref, pages, sel_ref, qb_ref, gacc_ref)


def _moe_sorted_body(cnt_ref, x1_ref, h2_ref, wts_ref, wtst_ref, g2_ref, wg_ref, wu_ref, wd_ref, gfin_ref,
                     o_ref, p_scr, pt_scr, hs_scr, ws_scr, acc_scr, *, final_norm, router_tiles):
    i = pl.program_id(0)
    e = pl.program_id(1)
    ch = MOE_CHUNK
    counts = []
    for g in range(N_GROUPS):
        n = cnt_ref[i * router_tiles * N_GROUPS + g]
        for j in range(1, router_tiles):
            n = n + cnt_ref[(i * router_tiles + j) * N_GROUPS + g]
        counts.append(n)
    starts = [jnp.int32(0)]
    for g in range(N_GROUPS - 1):
        starts.append(starts[-1] + counts[g])

    @pl.when(e == 0)
    def _():
        _moe_sort(starts, h2_ref, wts_ref, wtst_ref, p_scr, pt_scr, hs_scr, ws_scr, acc_scr)

    group = e // EXP_PER_GROUP
    seg0, seg_n = starts[0], counts[0]
    for g in range(1, N_GROUPS):
        seg0 = jnp.where(group == g, starts[g], seg0)
        seg_n = jnp.where(group == g, counts[g], seg_n)
    lane = lax.broadcasted_iota(jnp.int32, (ch, LANES), 1)

    def chunk(c, carry):
        rows = pl.ds(pl.multiple_of(c * ch, ch), ch)
        hs = hs_scr[rows, :]
        wcol = jnp.sum(jnp.where(lane == e, ws_scr[rows, :], 0.0), axis=1, keepdims=True)
        act = _silu(_dot(hs, wg_ref[...])) * _dot(hs, wu_ref[...]) * wcol
        acc_scr[rows, :] += _dot(act.astype(BF16), wd_ref[...])
        return carry

    lax.fori_loop(seg0 // ch, (seg0 + seg_n + ch - 1) // ch, chunk, 0)

    @pl.when(e == pl.num_programs(1) - 1)
    def _():
        y = _dot(pt_scr[...], acc_scr[...].astype(BF16))
        x2 = x1_ref[...] + g2_ref[...] * y
        if final_norm:
            ms = jnp.mean(x2 * x2, axis=-1, keepdims=True)
            x2 = x2 * lax.rsqrt(ms + EPS) * gfin_ref[...]
        o_ref[...] = x2


def _moe_sorted(x1, h2, wts, wtst, gcnt, gate2, wg_b, wu_b, wd_b, w_layer, g_final, tm, final_norm, paged=None):
    s, d = x1.shape
    _, ne, _, f = wg_b.shape
    router_tiles = gcnt.shape[0] * tm // s
    assert s % tm == 0 and tm % MOE_CHUNK == 0 and tm % MOE_SORT_ROWS == 0 and router_tiles >= 1
    cnt_flat = gcnt[:, 0, :N_GROUPS].reshape(-1)
    n_steps = (s // tm) * ne
    fuse = False
    if paged is not None:
        cache_kt, layer, page_table_flat, q_col, n_pages = paged
        assert n_pages % PAGES_PER_STEP == 0 and PAGES_PER_STEP % PAGES_PER_BLOCK == 0
        steps_per_seq = n_pages // PAGES_PER_STEP
        paged_steps = q_col.shape[0] * steps_per_seq
        n_blocks = n_pages // PAGES_PER_BLOCK
        assert n_blocks >= MOBA_TOPK
        fuse = paged_steps <= n_steps
    n_pre = 2 if fuse else 1
    row = lambda width: pl.BlockSpec((tm, width), lambda i, e, *_: (i, 0))
    wspec = lambda r, c: pl.BlockSpec((None, None, r, c), lambda i, e, *_: (w_layer, e, 0, 0))
    vec = pl.BlockSpec((1, d), lambda i, e, *_: (0, 0))
    in_specs = [row(d), row(d), row(LANES), pl.BlockSpec((LANES, tm), lambda i, e, *_: (0, i)), vec,
                wspec(d, f), wspec(d, f), wspec(f, d), vec]
    out_specs = [row(d)]
    out_shape = [jax.ShapeDtypeStruct((s, d), F32)]
    scratch = [pltpu.VMEM((tm, tm), BF16),
               pltpu.VMEM((tm, tm), BF16),
               pltpu.VMEM((tm, d), BF16),
               pltpu.VMEM((tm, LANES), F32),
               pltpu.VMEM((tm, d), F32)]
    operands = [cnt_flat, x1, h2, wts, wtst, gate2, wg_b, wu_b, wd_b, g_final]
    body = functools.partial(_moe_sorted_kernel, final_norm=final_norm, router_tiles=router_tiles)
    if fuse:
        def seq_step(i, e):
            st = jnp.minimum(i * ne + e, paged_steps - 1)
            return st // steps_per_seq, st % steps_per_seq

        def page_spec(g):
            def imap(i, e, cnt, pt):
                bi, j = seq_step(i, e)
                return (layer, pt[bi * n_pages + j * PAGES_PER_STEP + g], 0, 0)
            return pl.BlockSpec((None, None, HEAD_W, PAGE_SIZE), imap)

        in_specs += [pl.BlockSpec((None, HEAD_W, 1), lambda i, e, cnt, pt: (seq_step(i, e)[0], 0, 0))]
        in_specs += [page_spec(g) for g in range(PAGES_PER_STEP)]
        out_specs += [pl.BlockSpec((None, SUBLANES, LANES), lambda i, e, cnt, pt: (seq_step(i, e)[0], 0, 0))]
        out_shape += [jax.ShapeDtypeStruct((q_col.shape[0], SUBLANES, LANES), jnp.int32)]
        scratch += [pltpu.VMEM((HEAD_W, PAGE_SIZE), F32),
                    pltpu.VMEM((n_blocks * ATT_HEADS * SUBLANES, PAGE_SIZE), F32)]
        operands = [cnt_flat, page_table_flat] + operands[1:] + [q_col] + [cache_kt] * PAGES_PER_STEP
        body = functools.partial(_moe_sorted_paged_kernel, final_norm=final_norm, router_tiles=router_tiles,
                                 paged_steps=paged_steps, steps_per_seq=steps_per_seq)
    grid_spec = pltpu.PrefetchScalarGridSpec(
        num_scalar_prefetch=n_pre, grid=(s // tm, ne), in_specs=in_specs, out_specs=out_specs,
        scratch_shapes=scratch)
    outs = pl.pallas_call(
        body,
        grid_spec=grid_spec,
        out_shape=out_shape,
        compiler_params=_params("arbitrary", "arbitrary"),
        name="moe_sorted",
    )(*operands)
    if fuse:
        return outs[0], outs[1]
    sel = None if paged is None else _paged_gate(cache_kt, layer, page_table_flat, q_col, n_pages)
    return outs[0], sel


def _paged_gate_kernel(pt_ref, q_ref, *rest):
    pages = rest[:PAGES_PER_STEP]
    sel_ref, qb_ref, acc_ref = rest[PAGES_PER_STEP:]
    _paged_gate_step(pl.program_id(1), pl.num_programs(1) - 1, q_ref, pages, sel_ref, qb_ref, acc_ref)


def _paged_gate_step(j, last_j, q_ref, pages, sel_ref, qb_ref, acc_ref):
    rows_per_block = ATT_HEADS * SUBLANES

    @pl.when(j == 0)
    def _():
        qb_ref[...] = jnp.broadcast_to(q_ref[...] * ATT_HD ** -0.5, qb_ref.shape)

    qb = qb_ref[...]
    blocks_per_step = PAGES_PER_STEP // PAGES_PER_BLOCK
    for t in range(blocks_per_step):
        tot = pages[PAGES_PER_BLOCK * t][...]
        for u in range(1, PAGES_PER_BLOCK):
            tot = tot + pages[PAGES_PER_BLOCK * t + u][...]
        part = (tot * qb).reshape(ATT_HEADS, ATT_HD // SUBLANES, SUBLANES, PAGE_SIZE).sum(axis=1)
        base = pl.multiple_of((j * blocks_per_step + t) * rows_per_block, rows_per_block)
        acc_ref[pl.ds(base, rows_per_block), :] = part.reshape(rows_per_block, PAGE_SIZE)

    @pl.when(j == last_j)
    def _():
        n_rows = acc_ref.shape[0]
        nb = n_rows // rows_per_block
        g = jnp.sum(acc_ref[...], axis=1, keepdims=True)
        r = lax.broadcasted_iota(jnp.int32, (n_rows, LANES), 0)
        lane = lax.broadcasted_iota(jnp.int32, (n_rows, LANES), 1)
        spread = jnp.where(lane == (r // SUBLANES) % ATT_HEADS, g, 0.0)
        gate = spread.reshape(nb, rows_per_block, LANES).sum(axis=1) * (1.0 / MOBA_BLOCK)
        _, idxs = _top3_rows(gate, nb)
        out = jnp.zeros(sel_ref.shape, jnp.int32)
        rr = lax.broadcasted_iota(jnp.int32, sel_ref.shape, 0)
        for t, idx in enumerate(idxs):
            out = jnp.where(rr == t, idx, out)
        sel_ref[...] = out


def _paged_gate(cache_kt, layer, page_table_flat, q_col, n_pages):
    b = q_col.shape[0]
    assert n_pages % PAGES_PER_STEP == 0 and PAGES_PER_STEP % PAGES_PER_BLOCK == 0
    nb = n_pages // PAGES_PER_BLOCK
    assert nb >= MOBA_TOPK

    def page_spec(g):
        return pl.BlockSpec((None, None, HEAD_W, PAGE_SIZE),
                            lambda bi, j, pt: (layer, pt[bi * n_pages + j * PAGES_PER_STEP + g], 0, 0))

    grid_spec = pltpu.PrefetchScalarGridSpec(
        num_scalar_prefetch=1,
        grid=(b, n_pages // PAGES_PER_STEP),
        in_specs=[pl.BlockSpec((None, HEAD_W, 1), lambda bi, j, pt: (bi, 0, 0))]
                 + [page_spec(g) for g in range(PAGES_PER_STEP)],
        out_specs=pl.BlockSpec((None, SUBLANES, LANES), lambda bi, j, pt: (bi, 0, 0)),
        scratch_shapes=[pltpu.VMEM((HEAD_W, PAGE_SIZE), F32),
                        pltpu.VMEM((nb * ATT_HEADS * SUBLANES, PAGE_SIZE), F32)],
    )
    return pl.pallas_call(
        _paged_gate_kernel,
        grid_spec=grid_spec,
        out_shape=jax.ShapeDtypeStruct((b, SUBLANES, LANES), jnp.int32),
        compiler_params=_params("parallel", "arbitrary"),
        name="paged_gate",
    )(page_table_flat, q_col, *([cache_kt] * PAGES_PER_STEP))


def _moba_sample_kernel(pt_ref, sel_ref, slopes_ref, q_ref, kn_ref, vn_ref, *rest, past_len):
    npg = MOBA_TOPK * PAGES_PER_BLOCK
    kp = rest[:npg]
    vp = rest[npg:2 * npg]
    o_ref = rest[2 * npg]
    b = pl.program_id(0)
    h = pl.program_id(1)
    slope = slopes_ref[h]
    lane = lax.broadcasted_iota(jnp.int32, (1, PAGE_SIZE), 1)
    q = q_ref[...] * ATT_HD ** -0.5
    scores = []
    for s in range(MOBA_TOPK):
        blk = sel_ref[(b * ATT_HEADS + h) * MOBA_TOPK + s]
        for t in range(PAGES_PER_BLOCK):
            kpos = blk * MOBA_BLOCK + t * PAGE_SIZE + lane
            dist = (past_len - kpos).astype(F32)
            sc = jnp.sum(kp[s * PAGES_PER_BLOCK + t][...] * q, axis=0, keepdims=True)
            scores.append(sc - slope * dist)
    s_own = jnp.sum(q * kn_ref[...], axis=0, keepdims=True)
    m = functools.reduce(jnp.maximum, [jnp.max(sc, axis=1, keepdims=True) for sc in scores] + [s_own])
    p_own = jnp.exp(s_own - m)
    l = p_own
    acc = p_own * vn_ref[...]
    for idx, sc in enumerate(scores):
        pr = jnp.exp(sc - m)
        l = l + jnp.sum(pr, axis=1, keepdims=True)
        acc = acc + jnp.sum(vp[idx][...] * pr, axis=1, keepdims=True)
    o_ref[...] = acc / l


def _moba_sample(cache_kt, cache_vt, layer, page_table_flat, sel_flat, q_col, k_col, v_col, n_pages):
    b = q_col.shape[0]
    npg = MOBA_TOPK * PAGES_PER_BLOCK
    h = jnp.arange(1, ATT_HEADS + 1, dtype=F32)
    slopes = 2.0 ** (-8.0 * h / ATT_HEADS)

    def page_spec(g):
        s, t = divmod(g, PAGES_PER_BLOCK)

        def imap(bi, hi, pt, sel):
            blk = sel[(bi * ATT_HEADS + hi) * MOBA_TOPK + s]
            return (layer, pt[bi * n_pages + blk * PAGES_PER_BLOCK + t], hi, 0)

        return pl.BlockSpec((None, None, ATT_HD, PAGE_SIZE), imap)

    vec = pl.BlockSpec((None, ATT_HD, 1), lambda bi, hi, pt, sel: (bi, hi, 0))
    grid_spec = pltpu.PrefetchScalarGridSpec(
        num_scalar_prefetch=2,
        grid=(b, ATT_HEADS),
        in_specs=[pl.BlockSpec(memory_space=pltpu.SMEM), vec, vec, vec]
                 + [page_spec(g) for g in range(npg)] * 2,
        out_specs=vec,
    )
    return pl.pallas_call(
        functools.partial(_moba_sample_kernel, past_len=n_pages * PAGE_SIZE),
        grid_spec=grid_spec,
        out_shape=jax.ShapeDtypeStruct((b, HEAD_W, 1), F32),
        compiler_params=_params("parallel", "arbitrary"),
        name="moba_sample",
    )(page_table_flat, sel_flat, slopes, q_col, k_col, v_col, *([cache_kt] * npg), *([cache_vt] * npg))


def _ret_sample_kernel(st_ref, q_ref, k_ref, v_ref, rg_ref, gamma_ref, ns_ref, o_ref):
    new = gamma_ref[...] * st_ref[...] + (k_ref[...] * RET_DK ** -0.5) * v_ref[...]
    ns_ref[...] = new
    o = jnp.sum(q_ref[...] * new, axis=1, keepdims=True)
    mu = jnp.mean(o, axis=-1, keepdims=True)
    dlt = o - mu
    var = jnp.mean(dlt * dlt, axis=-1, keepdims=True)
    o_ref[...] = _silu(rg_ref[...]) * (dlt * lax.rsqrt(var + EPS))


def _ret_sample(state, layer, rq, rk, rv, rg):
    b = state.shape[1]
    hh = jnp.arange(RET_HEADS, dtype=F32)
    gamma = jnp.exp(jnp.log1p(-(2.0 ** (-5.0 - hh)))).reshape(RET_HEADS, 1, 1)
    st_in = pl.BlockSpec((None, None, RET_HEADS, RET_DK, RET_DV), lambda i: (layer, i, 0, 0, 0))
    st_out = pl.BlockSpec((None, RET_HEADS, RET_DK, RET_DV), lambda i: (i, 0, 0, 0))
    col = pl.BlockSpec((None, RET_HEADS, RET_DK, 1), lambda i: (i, 0, 0, 0))
    rw = pl.BlockSpec((None, RET_HEADS, 1, RET_DV), lambda i: (i, 0, 0, 0))
    return pl.pallas_call(
        _ret_sample_kernel,
        grid=(b,),
        in_specs=[st_in, col, col, rw, rw, _full((RET_HEADS, 1, 1))],
        out_specs=(st_out, rw),
        out_shape=(jax.ShapeDtypeStruct(state.shape[1:], F32),
                   jax.ShapeDtypeStruct((b, RET_HEADS, 1, RET_DV), F32)),
        compiler_params=_params("parallel"),
        name="retention_sample",
    )(state, rq, rk, rv, rg, gamma)


def kernel(x_prompt, x_sample, cache_k, cache_v, state_ret, page_table, c_prompt, c_sample, w_ada, b_ada, g_mix, w_in, w_ret_o, w_att_o, w_out, g_ffn, w_rg, b_rg, w_re, b_re, w_gate_e, w_up_e, w_down_e, g_final):
    bp, s, d = x_prompt.shape
    bs, ds, _ = x_sample.shape
    assert bp == 1 and ds == 1
    depth = w_ada.shape[0]
    n_pool = cache_k.shape[1]
    n_pages = page_table.shape[1]
    w = HEAD_W

    n_rows = bp + bs
    pad = (-n_rows) % SUBLANES
    c_all = jnp.concatenate([c_prompt, c_sample, jnp.zeros((pad, d), F32)], axis=0)
    mods = _ada(c_all, w_ada, b_ada)

    cache_kt = cache_k.transpose(0, 1, 3, 4, 2).reshape(depth, n_pool, w, PAGE_SIZE)
    cache_vt = cache_v.transpose(0, 1, 3, 4, 2).reshape(depth, n_pool, w, PAGE_SIZE)
    pt_flat = page_table.reshape(-1)

    n_log = N_GROUPS + N_EXPERTS
    rr = 32
    g_fin = g_final.reshape(1, d)

    w_in_all = w_in.astype(BF16)
    wg_all = w_gate_e.astype(BF16)
    wu_all = w_up_e.astype(BF16)
    wd_all = w_down_e.astype(BF16)

    xp = x_prompt.reshape(s, d)
    xs = x_sample.reshape(bs, d)
    kp_l, vp_l, rp_l, ks_l, vs_l, rs_l = [], [], [], [], [], []
    for l in range(depth):
        last = l == depth - 1
        mp = [mods[l, 0:bp, j * d:(j + 1) * d] for j in range(6)]
        ms = [mods[l, bp:bp + bs, j * d:(j + 1) * d] for j in range(6)]
        w_kvt_b = w_in[l][:, 5 * w:7 * w].T.astype(BF16)
        wro_b = w_ret_o[l].astype(BF16)
        wao_b = w_att_o[l].astype(BF16)
        wo_b = w_out[l].astype(BF16)
        wr_t = jnp.concatenate([w_rg[l].T, w_re[l].T, jnp.zeros((rr - n_log, d), F32)], axis=0)
        br_col = jnp.concatenate([b_rg[l], b_re[l], jnp.zeros((rr - n_log,), F32)]).reshape(rr, 1)
        gm = g_mix[l].reshape(1, d)
        gf = g_ffn[l].reshape(1, d)

        proj = _proj_sample(xs, gm, ms[0], ms[1], w_in, l)
        cut = lambda j, n=1: proj[:, j * w:(j + n) * w]
        rq_s, rk_s, rv_s, rg_s, aq_s, ak_s, av_s = (cut(j) for j in range(7))
        gr_s, ga_s = cut(7, 2), cut(9, 2)
        q_col = aq_s.reshape(bs, w, 1)

        rq, rk, rv, rg, aq, akb, akt, avt, avtb, gr, ga, qc = _proj_prompt(xp, gm, mp[0], mp[1], w_in_all, l,
                                                                           w_kvt_b)
        ret_in, st = _retention_prompt(rq, rk, rv, rg, chunk=MOBA_BLOCK)
        att_in = _moba_prompt(aq, qc, akb, avtb)
        x1, h2, wts, wtst, gcnt = _merge_router(xp, ret_in, att_in, gr, ga, mp[2], wro_b, wao_b, wo_b, gf,
                                                mp[3], mp[4], wr_t, br_col, tm=512, exact=0)
        xp, sel_s = _moe_sorted(x1, h2, wts, wtst, gcnt, mp[5], wg_all, wu_all, wd_all, l, g_fin, tm=1024,
                                final_norm=last, paged=(cache_kt, l, pt_flat, q_col, n_pages))
        kp_l.append(akt)
        vp_l.append(avt)
        rp_l.append(st.reshape(bp, RET_HEADS, RET_DK, RET_DV))

        new_state, ret_s = _ret_sample(
            state_ret, l,
            rq_s.reshape(bs, RET_HEADS, RET_DK, 1), rk_s.reshape(bs, RET_HEADS, RET_DK, 1),
            rv_s.reshape(bs, RET_HEADS, 1, RET_DV), rg_s.reshape(bs, RET_HEADS, 1, RET_DV))
        sel_flat = sel_s[:, :MOBA_TOPK, :ATT_HEADS].transpose(0, 2, 1).reshape(-1)
        att_s = _moba_sample(cache_kt, cache_vt, l, pt_flat, sel_flat, q_col, ak_s.reshape(bs, w, 1),
                             av_s.reshape(bs, w, 1), n_pages)
        x1s, h2s, wts_s, _, _ = _merge_router(xs, ret_s.reshape(bs, w), att_s.reshape(bs, w), gr_s, ga_s, ms[2],
                                              w_ret_o, w_att_o, w_out, gf, ms[3], ms[4], wr_t, br_col,
                                              tm=bs, exact=l + 1)
        xs = _moe(x1s, h2s, wts_s, ms[5], w_gate_e, w_up_e, w_down_e, l, g_fin, tm=bs, final_norm=last,
                  exact=True)
        ks_l.append(ak_s.reshape(bs, 1, ATT_HEADS, ATT_HD))
        vs_l.append(av_s.reshape(bs, 1, ATT_HEADS, ATT_HD))
        rs_l.append(new_state)

    def kv_out(parts):
        return jnp.stack(parts).reshape(depth, bp, ATT_HEADS, ATT_HD, s).transpose(0, 1, 4, 2, 3)

    return (xp.reshape(bp, s, d), xs.reshape(bs, 1, d), kv_out(kp_l), kv_out(vp_l), jnp.stack(rp_l),
            jnp.stack(ks_l), jnp.stack(vs_l), jnp.stack(rs_l))
```

```python
import functools

import jax
import jax.numpy as jnp
from jax import lax
from jax.experimental import pallas as pl
from jax.experimental.pallas import tpu as pltpu

EPS = 1e-6
RET_HEADS = 8
RET_DK = 64
RET_DV = 64
ATT_HEADS = 8
ATT_HD = 64
HEAD_W = RET_HEADS * RET_DK
MOBA_BLOCK = 256
MOBA_TOPK = 3
PAGE_SIZE = 128
PAGES_PER_BLOCK = MOBA_BLOCK // PAGE_SIZE
N_GROUPS = 4
EXP_PER_GROUP = 4
N_EXPERTS = N_GROUPS * EXP_PER_GROUP
SUBLANES = 8
LANES = 128
PAIRS = HEAD_W // LANES
NEG = -1e30
ALIBI_SLOPES = tuple(2.0 ** (-8.0 * (j + 1) / ATT_HEADS) for j in range(ATT_HEADS))
PAGES_PER_STEP = 16
MOE_CHUNK = 128
MOE_SORT_ROWS = 256
VMEM_LIMIT = 56 * 1024 * 1024

F32 = jnp.float32
BF16 = jnp.bfloat16
HI = lax.Precision.HIGHEST
NT = (((1,), (1,)), ((), ()))
TN = (((0,), (0,)), ((), ()))


def _dot(a, b):
    return jnp.dot(a, b, preferred_element_type=F32)


def _dot_nt(a, b):
    return lax.dot_general(a, b, NT, preferred_element_type=F32)


def _dot_hi(a, b):
    return jnp.dot(a, b, preferred_element_type=F32, precision=HI)


def _dot_nt_hi(a, b):
    return lax.dot_general(a, b, NT, preferred_element_type=F32, precision=HI)


def _mm(a, w_ref, exact):
    if exact:
        return _dot_hi(a.astype(F32), w_ref[...])
    return _dot(a.astype(BF16), w_ref[...].astype(BF16))


def _dot_split(a, b_bf16, lhs_exact=False):
    hi = a.astype(BF16)
    lo = (a - hi.astype(F32)).astype(BF16)
    if lhs_exact:
        return _dot(b_bf16, hi) + _dot(b_bf16, lo)
    return _dot(hi, b_bf16) + _dot(lo, b_bf16)


def _params(*sem):
    return pltpu.CompilerParams(dimension_semantics=sem, vmem_limit_bytes=VMEM_LIMIT)


def _full(shape):
    n = len(shape)
    return pl.BlockSpec(shape, lambda *_: (0,) * n)


def _modulate(x, g, shift, scale):
    ms = jnp.mean(x * x, axis=-1, keepdims=True)
    y = x * lax.rsqrt(ms + EPS) * g
    return y * (1.0 + scale) + shift


def _silu(x):
    return x * jax.nn.sigmoid(x)


def _head_lane_mask(shape, a):
    lane = lax.broadcasted_iota(jnp.int32, shape, len(shape) - 1)
    return (lane // RET_DK) == a


def _top3_rows(g, n_rows):
    row = lax.broadcasted_iota(jnp.int32, g.shape, 0)
    sel = jnp.zeros(g.shape, F32)
    idxs = []
    for _ in range(MOBA_TOPK):
        cmax = jnp.max(g, axis=0, keepdims=True)
        idx = jnp.min(jnp.where(g == cmax, row, n_rows), axis=0, keepdims=True)
        pick = jnp.logical_and(row == idx, cmax > -jnp.inf)
        sel = jnp.where(pick, 1.0, sel)
        g = jnp.where(pick, -jnp.inf, g)
        idxs.append(idx)
    return sel, idxs


def _ada_kernel(c_ref, w_ref, b_ref, o_ref):
    c = c_ref[...]
    o_ref[...] = _dot_hi(_silu(c), w_ref[...]) + b_ref[...]


def _ada(c_all, w_ada, b_ada):
    depth, d, d6 = w_ada.shape
    r = c_all.shape[0]
    return pl.pallas_call(
        _ada_kernel,
        grid=(depth, d6 // d),
        in_specs=[
            pl.BlockSpec((r, d), lambda l, j: (0, 0)),
            pl.BlockSpec((None, d, d), lambda l, j: (l, 0, j)),
            pl.BlockSpec((None, 1, d), lambda l, j: (l, 0, j)),
        ],
        out_specs=pl.BlockSpec((None, r, d), lambda l, j: (l, 0, j)),
        out_shape=jax.ShapeDtypeStruct((depth, r, d6), F32),
        compiler_params=_params("parallel", "parallel"),
        name="ada_mod",
    )(c_all, w_ada, b_ada.reshape(depth, 1, d6))


def _proj_prompt_kernel(x_ref, g_ref, sh_ref, sc_ref, w_ref, wkvt_ref,
                        rq_ref, rk_ref, rv_ref, rg_ref, aq_ref, akb_ref, akt_ref, avt_ref, avtb_ref,
                        gr_ref, ga_ref, qc_ref, means_ref, *, slopes):
    i = pl.program_id(0)
    nb = means_ref.shape[0]
    tm = x_ref.shape[0]
    w = HEAD_W

    @pl.when(i == 0)
    def _():
        means_ref[...] = jnp.zeros_like(means_ref)

    hb = _modulate(x_ref[...], g_ref[...], sh_ref[...], sc_ref[...]).astype(BF16)

    def col(c, n=1):
        return _dot(hb, w_ref[:, c * w:(c + n) * w])

    rq_ref[...] = col(0).astype(BF16)
    rk_ref[...] = (col(1) * RET_DK ** -0.5).astype(BF16)
    rv_ref[...] = col(2).astype(BF16)
    rg_ref[...] = col(3)
    q = col(4) * ATT_HD ** -0.5
    aq_ref[...] = q.astype(BF16)
    k = col(5)
    akb_ref[...] = k.astype(BF16)
    gr_ref[...] = col(7, 2)
    ga_ref[...] = col(9, 2)
    kvt = _dot_nt(wkvt_ref[...], hb)
    akt_ref[...] = kvt[:w]
    avt_ref[...] = kvt[w:]
    avtb_ref[...] = kvt[w:].astype(BF16)

    blk = lax.broadcasted_iota(jnp.int32, (nb, tm), 0)
    valid = blk < i
    blk_dist = (MOBA_BLOCK * (i - blk)).astype(F32)
    xrow = lax.broadcasted_iota(jnp.int32, (LANES - nb, tm), 0)
    xlane = lax.broadcasted_iota(jnp.int32, (LANES - nb, tm), 1).astype(F32)
    for p in range(PAIRS):
        mp = means_ref[:, p * LANES:(p + 1) * LANES]
        qp = q[:, p * LANES:(p + 1) * LANES]
        for a in range(2):
            slope = slopes[2 * p + a]
            qm = jnp.where(_head_lane_mask(qp.shape, a), qp, 0.0)
            gate = jnp.where(valid, _dot_nt_hi(mp, qm), -jnp.inf)
            sel, _ = _top3_rows(gate, nb)
            ct = jnp.where(sel > 0.5, -slope * blk_dist, NEG)
            extra = jnp.where(xrow == 0, -slope * xlane, jnp.where(xrow == 1, slope, 0.0))
            qc_ref[2 * p + a] = jnp.concatenate([ct, extra], axis=0).T.astype(BF16)

    means_ref[pl.ds(i, 1), :] = jnp.mean(k, axis=0, keepdims=True)


def _proj_prompt(x, g, shift, scale, w_in_b, layer, w_kvt_b):
    s, d = x.shape
    tm = MOBA_BLOCK
    assert s % tm == 0
    nb = s // tm
    w = HEAD_W
    n_in = w_in_b.shape[2]
    row = lambda width: pl.BlockSpec((tm, width), lambda i: (i, 0))
    colblk = pl.BlockSpec((w, tm), lambda i: (0, i))
    vec = pl.BlockSpec((1, d), lambda i: (0, 0))
    out_shape = (
        jax.ShapeDtypeStruct((s, w), BF16),
        jax.ShapeDtypeStruct((s, w), BF16),
        jax.ShapeDtypeStruct((s, w), BF16),
        jax.ShapeDtypeStruct((s, w), F32),
        jax.ShapeDtypeStruct((s, w), BF16),
        jax.ShapeDtypeStruct((nb, tm, w), BF16),
        jax.ShapeDtypeStruct((w, s), F32),
        jax.ShapeDtypeStruct((w, s), F32),
        jax.ShapeDtypeStruct((nb, w, tm), BF16),
        jax.ShapeDtypeStruct((s, d), F32),
        jax.ShapeDtypeStruct((s, d), F32),
        jax.ShapeDtypeStruct((ATT_HEADS, s, LANES), BF16),
    )
    assert nb + 2 <= LANES
    out_specs = (
        row(w), row(w), row(w), row(w), row(w),
        pl.BlockSpec((None, tm, w), lambda i: (i, 0, 0)),
        colblk, colblk,
        pl.BlockSpec((None, w, tm), lambda i: (i, 0, 0)),
        row(d), row(d),
        pl.BlockSpec((ATT_HEADS, tm, LANES), lambda i: (0, i, 0)),
    )
    return pl.pallas_call(
        functools.partial(_proj_prompt_kernel, slopes=ALIBI_SLOPES),
        grid=(nb,),
        in_specs=[row(d), vec, vec, vec,
                  pl.BlockSpec((None, d, n_in), lambda i: (layer, 0, 0), pipeline_mode=pl.Buffered(1)),
                  pl.BlockSpec((2 * w, d), lambda i: (0, 0), pipeline_mode=pl.Buffered(1))],
        out_specs=out_specs,
        out_shape=out_shape,
        scratch_shapes=[pltpu.VMEM((nb, w), F32)],
        compiler_params=_params("arbitrary"),
        name="proj_prompt",
    )(x, g, shift, scale, w_in_b, w_kvt_b)


def _proj_sample_kernel(x_ref, g_ref, sh_ref, sc_ref, w_ref, o_ref):
    h = _modulate(x_ref[...], g_ref[...], sh_ref[...], sc_ref[...])
    o_ref[...] = _dot_hi(h, w_ref[...])


def _proj_sample(x, g, shift, scale, w_in, layer):
    b, d = x.shape
    n_in = w_in.shape[2]
    tn = HEAD_W
    return pl.pallas_call(
        _proj_sample_kernel,
        grid=(n_in // tn,),
        in_specs=[_full((b, d)), _full((1, d)), _full((b, d)), _full((b, d)),
                  pl.BlockSpec((None, d, tn), lambda j: (layer, 0, j))],
        out_specs=pl.BlockSpec((b, tn), lambda j: (0, j)),
        out_shape=jax.ShapeDtypeStruct((b, n_in), F32),
        compiler_params=_params("parallel"),
        name="proj_sample",
    )(x, g, shift, scale, w_in)


def _retention_kernel(q_ref, k_ref, v_ref, rg_ref, decay_ref, xi_ref, zeta_ref, gc_ref, avg_ref,
                      o_ref, st_ref):
    i = pl.program_id(0)

    @pl.when(i == 0)
    def _():
        st_ref[...] = jnp.zeros_like(st_ref)

    c = q_ref.shape[0]
    row = lax.broadcasted_iota(jnp.int32, (LANES, LANES), 0)
    lane = lax.broadcasted_iota(jnp.int32, (LANES, LANES), 1)
    same_head = (row // RET_DK) == (lane // RET_DV)
    avg = avg_ref[...]
    for p in range(PAIRS):
        sl = slice(p * LANES, (p + 1) * LANES)
        q = q_ref[:, sl]
        k = k_ref[:, sl]
        v = v_ref[:, sl]
        state = st_ref[p]
        cross = _dot(q, state.astype(BF16)) * xi_ref[:, sl]
        inner = []
        for a in range(2):
            qm = jnp.where(_head_lane_mask(q.shape, a), q, jnp.zeros_like(q))
            scores = _dot_nt(qm, k) * decay_ref[2 * p + a]
            inner.append(_dot(scores.astype(BF16), v))
        o = jnp.where(_head_lane_mask((c, LANES), 0), inner[0], inner[1]) + cross
        kz = (k.astype(F32) * zeta_ref[:, sl]).astype(BF16)
        upd = lax.dot_general(kz, v, TN, preferred_element_type=F32)
        st_ref[p] = gc_ref[:, sl] * state + jnp.where(same_head, upd, 0.0)
        mu = _dot_split(o, avg)
        dlt = o - mu
        var = _dot_split(dlt * dlt, avg)
        hn = dlt * lax.rsqrt(var + EPS)
        o_ref[:, sl] = (_silu(rg_ref[:, sl]) * hn).astype(BF16)


def _ret_tables(c):
    h = jnp.arange(RET_HEADS, dtype=F32)
    log_gamma = jnp.log1p(-(2.0 ** (-5.0 - h)))
    idx = jnp.arange(c, dtype=F32)
    diff = idx[:, None] - idx[None, :]
    decay = jnp.where(diff >= 0, jnp.exp(log_gamma[:, None, None] * jnp.maximum(diff, 0.0)), 0.0)
    xi = jnp.exp(log_gamma[None, :] * (idx[:, None] + 1.0))
    zeta = jnp.exp(log_gamma[None, :] * (c - 1.0 - idx[:, None]))
    gc = jnp.exp(log_gamma * c)[None, :]
    rep = lambda t: jnp.repeat(t, RET_DV, axis=1)
    return decay, rep(xi), rep(zeta), rep(gc)


def _head_avg_matrix():
    r = jnp.arange(LANES)
    return jnp.where((r[:, None] // RET_DV) == (r[None, :] // RET_DV), 1.0 / RET_DV, 0.0).astype(BF16)


def _retention_prompt(rq, rk, rv, rg, chunk):
    s, w = rq.shape
    assert s % chunk == 0
    decay, xi, zeta, gc = _ret_tables(chunk)
    row = pl.BlockSpec((chunk, w), lambda i: (i, 0))
    o, st = pl.pallas_call(
        _retention_kernel,
        grid=(s // chunk,),
        in_specs=[row, row, row, row,
                  _full((RET_HEADS, chunk, chunk)), _full((chunk, w)), _full((chunk, w)), _full((1, w)),
                  _full((LANES, LANES))],
        out_specs=(row, _full((PAIRS, LANES, LANES))),
        out_shape=(jax.ShapeDtypeStruct((s, w), BF16), jax.ShapeDtypeStruct((PAIRS, LANES, LANES), F32)),
        compiler_params=_params("arbitrary"),
        name="retention_prompt",
    )(rq, rk, rv, rg, decay, xi, zeta, gc, _head_avg_matrix())
    st = st.reshape(PAIRS, 2, RET_DK, 2, RET_DV)
    st = jnp.stack([st[:, 0, :, 0, :], st[:, 1, :, 1, :]], axis=1).reshape(RET_HEADS, RET_DK, RET_DV)
    return o, st


def _moba_prompt_kernel(q_ref, qc_ref, k_ref, vt_ref, ctab_ref, cown_ref, o_ref,
                        qcat_ref, sa_ref, sb_ref, sc_ref, sd_ref, ma_ref, mb_ref, mc_ref, md_ref,
                        m_ref, l_ref, acc_ref):
    i = pl.program_id(0)
    tq = q_ref.shape[0]
    tk = k_ref.shape[1]
    nb = k_ref.shape[0]
    heads = range(ATT_HEADS)
    hd = ATT_HD

    for p in range(PAIRS):
        qp = q_ref[:, p * LANES:(p + 1) * LANES]
        for a in range(2):
            h = 2 * p + a
            qm = jnp.where(_head_lane_mask(qp.shape, a), qp, jnp.zeros_like(qp))
            qcat_ref[h] = jnp.concatenate([qm, qc_ref[h]], axis=1)

    ones = jnp.ones((2 * SUBLANES, tk), BF16)

    def scores2(n_a, n_b, h, ra_ref, rb_ref, mxa_ref, mxb_ref):
        cols = slice((h // 2) * LANES, (h // 2 + 1) * LANES)
        kcat = jnp.concatenate([jnp.concatenate([k_ref[n_a, :, cols], ctab_ref[n_a]], axis=1),
                                jnp.concatenate([k_ref[n_b, :, cols], ctab_ref[n_b]], axis=1)], axis=0)
        s2 = _dot_nt(kcat, qcat_ref[h])
        for r_ref, mx_ref, rows in ((ra_ref, mxa_ref, slice(0, tk)), (rb_ref, mxb_ref, slice(tk, 2 * tk))):
            r_ref[h] = s2[rows]
            mx_ref[h:h + 1, :] = jnp.max(s2[rows], axis=0, keepdims=True)

    def attend(n, h, s_ref, mx_ref):
        s = s_ref[h]
        m = m_ref[h:h + 1, :]
        m_new = jnp.maximum(m, mx_ref[h:h + 1, :])
        alpha = jnp.exp(m - m_new)
        pr = jnp.exp((s - m_new).astype(BF16))
        rows = slice(h * hd, (h + 1) * hd)
        pv = _dot(jnp.concatenate([vt_ref[n, rows, :], ones], axis=0), pr)
        m_ref[h:h + 1, :] = m_new
        l_ref[h:h + 1, :] = alpha * l_ref[h:h + 1, :] + pv[hd:hd + 1, :]
        acc_ref[rows, :] = alpha * acc_ref[rows, :] + pv[:hd, :]

    kk = lax.broadcasted_iota(jnp.int32, (tk, tq), 0)
    qq = lax.broadcasted_iota(jnp.int32, (tk, tq), 1)
    causal = kk <= qq
    m_ref[...] = jnp.full(m_ref.shape, NEG, F32)
    l_ref[...] = jnp.zeros_like(l_ref)
    acc_ref[...] = jnp.zeros_like(acc_ref)
    for h in heads:
        p = h // 2
        kcat = jnp.concatenate([k_ref[i, :, p * LANES:(p + 1) * LANES], cown_ref[...]], axis=1)
        s = jnp.where(causal, _dot_nt(kcat, qcat_ref[h]), NEG)
        sc_ref[h] = s
        mc_ref[h:h + 1, :] = jnp.max(s, axis=0, keepdims=True)
        scores2(0, 1, h, sa_ref, sb_ref, ma_ref, mb_ref)
    for h in heads:
        attend(i, h, sc_ref, mc_ref)

    n_full = i // 4

    @pl.loop(0, n_full)
    def _(t):
        n0 = 4 * t
        for h in heads:
            scores2(n0 + 2, n0 + 3, h, sc_ref, sd_ref, mc_ref, md_ref)
            attend(n0, h, sa_ref, ma_ref)
            attend(n0 + 1, h, sb_ref, mb_ref)
        for h in heads:
            scores2(jnp.minimum(n0 + 4, nb - 1), jnp.minimum(n0 + 5, nb - 1), h, sa_ref, sb_ref, ma_ref, mb_ref)
            attend(n0 + 2, h, sc_ref, mc_ref)
            attend(n0 + 3, h, sd_ref, md_ref)

    rem = i - 4 * n_full
    n0 = 4 * n_full

    @pl.when(rem > 0)
    def _():
        @pl.when(rem > 2)
        def _():
            for h in heads:
                scores2(n0 + 2, n0 + 3, h, sc_ref, sd_ref, mc_ref, md_ref)

        for h in heads:
            attend(n0, h, sa_ref, ma_ref)
            attend(n0 + 1, h, sb_ref, mb_ref)

        @pl.when(rem > 2)
        def _():
            for h in heads:
                attend(n0 + 2, h, sc_ref, mc_ref)
                attend(n0 + 3, h, sd_ref, md_ref)

    for h in heads:
        rows = slice(h * hd, (h + 1) * hd)
        acc_ref[rows, :] = acc_ref[rows, :] / l_ref[h:h + 1, :]
    o_ref[...] = acc_ref[...].T.astype(BF16)


def _moba_bias_table(nb, tk):
    lane = jnp.arange(LANES)[None, None, :]
    blk = jnp.arange(nb)[:, None, None]
    kk = jnp.arange(tk, dtype=F32)[None, :, None]
    t = jnp.where(lane == blk, 1.0, 0.0)
    t = jnp.where(lane == nb, 1.0, t)
    return jnp.where(lane == nb + 1, kk, t).astype(BF16)


def _moba_prompt(aq, qc, akb, avtb):
    s, w = aq.shape
    nb, tk, _ = akb.shape
    tq = MOBA_BLOCK
    assert nb % 4 == 0 and tk == tq
    ctab = _moba_bias_table(nb, tk)
    cown = jnp.where(jnp.arange(LANES)[None, :] < nb, 0, ctab[0])
    once = dict(pipeline_mode=pl.Buffered(1))
    return pl.pallas_call(
        _moba_prompt_kernel,
        grid=(s // tq,),
        in_specs=[
            pl.BlockSpec((tq, w), lambda i: (i, 0)),
            pl.BlockSpec((ATT_HEADS, tq, LANES), lambda i: (0, i, 0)),
            pl.BlockSpec((nb, tk, w), lambda i: (0, 0, 0), **once),
            pl.BlockSpec((nb, w, tk), lambda i: (0, 0, 0), **once),
            pl.BlockSpec((nb, tk, LANES), lambda i: (0, 0, 0), **once),
            pl.BlockSpec((tk, LANES), lambda i: (0, 0), **once),
        ],
        out_specs=pl.BlockSpec((tq, w), lambda i: (i, 0)),
        out_shape=jax.ShapeDtypeStruct((s, w), BF16),
        scratch_shapes=[
            pltpu.VMEM((ATT_HEADS, tq, 2 * LANES), BF16),
            *[pltpu.VMEM((ATT_HEADS, tk, tq), F32)] * 4,
            *[pltpu.VMEM((ATT_HEADS, tq), F32)] * 4,
            pltpu.VMEM((ATT_HEADS, tq), F32),
            pltpu.VMEM((ATT_HEADS, tq), F32),
            pltpu.VMEM((w, tq), F32),
        ],
        compiler_params=_params("arbitrary"),
        name="moba_prompt",
    )(aq, qc, akb, avtb, ctab, cown)


def _merge_router_kernel(x_ref, ret_ref, att_ref, gr_ref, ga_ref, g1_ref, wro_ref, wao_ref, wo_ref,
                         gf_ref, sh_ref, sc_ref, wr_ref, br_ref,
                         x1_ref, h2_ref, wts_ref, wtst_ref, gcnt_ref, *, exact):
    ret = _mm(ret_ref[...], wro_ref, exact)
    att = _mm(att_ref[...], wao_ref, exact)
    merged = jax.nn.sigmoid(gr_ref[...]) * ret + jax.nn.sigmoid(ga_ref[...]) * att
    x1 = x_ref[...] + g1_ref[...] * _mm(merged, wo_ref, exact)
    x1_ref[...] = x1
    h2 = _modulate(x1, gf_ref[...], sh_ref[...], sc_ref[...])
    h2_ref[...] = h2.astype(h2_ref.dtype)

    logits = _dot_nt_hi(wr_ref[...], h2) + br_ref[...]
    lg = [logits[g:g + 1, :] for g in range(N_GROUPS)]
    le = [logits[N_GROUPS + e:N_GROUPS + e + 1, :] for e in range(N_EXPERTS)]
    mg = functools.reduce(jnp.maximum, lg)
    eg = [jnp.exp(v - mg) for v in lg]
    den = functools.reduce(jnp.add, eg)
    pg = [v / den for v in eg]
    p_top = functools.reduce(jnp.maximum, pg)
    taken = jnp.zeros(p_top.shape, jnp.bool_)
    onehot = []
    for g in range(N_GROUPS):
        hit = jnp.logical_and(pg[g] == p_top, jnp.logical_not(taken))
        onehot.append(hit)
        taken = jnp.logical_or(taken, hit)
    vin = []
    for j in range(EXP_PER_GROUP):
        v = jnp.zeros_like(p_top)
        for g in range(N_GROUPS):
            v = jnp.where(onehot[g], le[g * EXP_PER_GROUP + j], v)
        vin.append(v)
    rank = []
    for j in range(EXP_PER_GROUP):
        r = jnp.zeros(p_top.shape, jnp.int32)
        for kx in range(EXP_PER_GROUP):
            if kx == j:
                continue
            ahead = (vin[kx] > vin[j]) | ((vin[kx] == vin[j]) & (kx < j))
            r = r + ahead.astype(jnp.int32)
        rank.append(r)
    v1 = functools.reduce(jnp.maximum, vin)
    v2 = jnp.full_like(v1, -jnp.inf)
    for j in range(EXP_PER_GROUP):
        v2 = jnp.where(rank[j] == 1, vin[j], v2)
    e2 = jnp.exp(v2 - v1)
    w_first = 1.0 / (1.0 + e2)
    w_second = e2 / (1.0 + e2)
    wtst_ref[...] = jnp.zeros_like(wtst_ref)
    row8 = lax.broadcasted_iota(jnp.int32, gcnt_ref.shape, 0)
    lane8 = lax.broadcasted_iota(jnp.int32, gcnt_ref.shape, 1)
    gcnt = jnp.zeros(gcnt_ref.shape, F32)
    for g in range(N_GROUPS):
        hot = jnp.where(onehot[g], 1.0, 0.0)
        wtst_ref[N_EXPERTS + g:N_EXPERTS + g + 1, :] = hot
        n_g = jnp.sum(hot, axis=1, keepdims=True)
        gcnt = jnp.where(jnp.logical_and(row8 == 0, lane8 == g), n_g, gcnt)
        for j in range(EXP_PER_GROUP):
            wj = jnp.where(rank[j] == 0, w_first, jnp.where(rank[j] == 1, w_second, 0.0)) * p_top
            e = g * EXP_PER_GROUP + j
            wtst_ref[e:e + 1, :] = jnp.where(onehot[g], wj, 0.0)
    wts_ref[...] = wtst_ref[...].T
    gcnt_ref[...] = gcnt.astype(jnp.int32)


def _merge_router(x, ret_in, att_in, gr, ga, gate1, wro, wao, wo, g_ffn, shift2, scale2, wr_t, br_col, tm, exact):
    s, d = x.shape
    w = HEAD_W
    assert s % tm == 0
    row = lambda width: pl.BlockSpec((tm, width), lambda i: (i, 0))
    mod = lambda a: pl.BlockSpec((tm if a.shape[0] > 1 else 1, d), lambda i: (i if a.shape[0] > 1 else 0, 0))
    if exact:
        layer = exact - 1
        wspec = lambda a: pl.BlockSpec((None,) + a.shape[1:], lambda i: (layer, 0, 0))
    else:
        wspec = lambda a: _full(a.shape)
    rr = wr_t.shape[0]
    return pl.pallas_call(
        functools.partial(_merge_router_kernel, exact=bool(exact)),
        grid=(s // tm,),
        in_specs=[row(d), row(w), row(w), row(d), row(d), mod(gate1),
                  wspec(wro), wspec(wao), wspec(wo),
                  _full((1, d)), mod(shift2), mod(scale2), _full((rr, d)), _full((rr, 1))],
        out_specs=(row(d), row(d), row(LANES),
                   pl.BlockSpec((LANES, tm), lambda i: (0, i)),
                   pl.BlockSpec((None, SUBLANES, LANES), lambda i: (i, 0, 0))),
        out_shape=(jax.ShapeDtypeStruct((s, d), F32), jax.ShapeDtypeStruct((s, d), F32 if exact else BF16),
                   jax.ShapeDtypeStruct((s, LANES), F32),
                   jax.ShapeDtypeStruct((LANES, s), F32),
                   jax.ShapeDtypeStruct((s // tm, SUBLANES, LANES), jnp.int32)),
        compiler_params=_params("parallel"),
        name="merge_router",
    )(x, ret_in, att_in, gr, ga, gate1, wro, wao, wo, g_ffn, shift2, scale2, wr_t, br_col)


def _moe_kernel(x1_ref, h2_ref, wts_ref, g2_ref, wg_ref, wu_ref, wd_ref, gfin_ref, o_ref, acc_ref,
                *, final_norm, exact):
    e = pl.program_id(1)

    @pl.when(e == 0)
    def _():
        acc_ref[...] = jnp.zeros_like(acc_ref)

    h2 = h2_ref[...]
    wts = wts_ref[...]
    lane = lax.broadcasted_iota(jnp.int32, wts.shape, 1)
    wcol = jnp.sum(jnp.where(lane == e, wts, 0.0), axis=1, keepdims=True)
    act = _silu(_mm(h2, wg_ref, exact)) * _mm(h2, wu_ref, exact) * wcol
    acc_ref[...] += _mm(act, wd_ref, exact)

    @pl.when(e == pl.num_programs(1) - 1)
    def _():
        x2 = x1_ref[...] + g2_ref[...] * acc_ref[...]
        if final_norm:
            ms = jnp.mean(x2 * x2, axis=-1, keepdims=True)
            x2 = x2 * lax.rsqrt(ms + EPS) * gfin_ref[...]
        o_ref[...] = x2


def _moe(x1, h2, wts, gate2, wg, wu, wd, layer, g_final, tm, final_norm, exact):
    s, d = x1.shape
    ne, f = wg.shape[-3], wg.shape[-1]
    assert s % tm == 0
    row = lambda width: pl.BlockSpec((tm, width), lambda i, e: (i, 0))
    mod = lambda a: pl.BlockSpec((tm if a.shape[0] > 1 else 1, d), lambda i, e: (i if a.shape[0] > 1 else 0, 0))
    wspec = lambda r, c: pl.BlockSpec((None, None, r, c), lambda i, e: (layer, e, 0, 0))
    return pl.pallas_call(
        functools.partial(_moe_kernel, final_norm=final_norm, exact=exact),
        grid=(s // tm, ne),
        in_specs=[row(d), row(d), row(LANES), mod(gate2),
                  wspec(d, f), wspec(d, f), wspec(f, d),
                  pl.BlockSpec((1, d), lambda i, e: (0, 0))],
        out_specs=row(d),
        out_shape=jax.ShapeDtypeStruct((s, d), F32),
        scratch_shapes=[pltpu.VMEM((tm, d), F32)],
        compiler_params=_params("parallel", "arbitrary"),
        name="moe_experts",
    )(x1, h2, wts, gate2, wg, wu, wd, g_final)


def _moe_sort(starts, h2_ref, wts_ref, wtst_ref, p_scr, pt_scr, hs_scr, ws_scr, acc_scr):
    tm = h2_ref.shape[0]
    ch = MOE_SORT_ROWS
    lo, hi = N_EXPERTS, N_EXPERTS + N_GROUPS
    wts = wts_ref[...]
    lane_t = lax.broadcasted_iota(jnp.int32, wts.shape, 1)
    hot_t = jnp.where(jnp.logical_and(lane_t >= lo, lane_t < hi), wts, 0.0)
    wtst = wtst_ref[...]
    row_l = lax.broadcasted_iota(jnp.int32, wtst.shape, 0)
    hot_l = jnp.where(jnp.logical_and(row_l >= lo, row_l < hi), wtst, 0.0)
    start_t = jnp.zeros((1, LANES), F32)
    start_l = jnp.zeros((LANES, 1), F32)
    lane1 = lax.broadcasted_iota(jnp.int32, (1, LANES), 1)
    row1 = lax.broadcasted_iota(jnp.int32, (LANES, 1), 0)
    for g in range(N_GROUPS):
        sg = starts[g].astype(F32)
        start_t = jnp.where(lane1 == lo + g, sg, start_t)
        start_l = jnp.where(row1 == lo + g, sg, start_l)
    hot_t_b = hot_t.astype(BF16)
    hot_l_b = hot_l.astype(BF16)
    for c in range(tm // ch):
        sl = slice(c * ch, (c + 1) * ch)
        tok = lax.broadcasted_iota(jnp.int32, (ch, tm), 0) + c * ch
        other = lax.broadcasted_iota(jnp.int32, (ch, tm), 1)
        earlier = jnp.where(other < tok, 1.0, 0.0).astype(BF16)
        rank = _dot(earlier, hot_t_b)
        pos = jnp.sum(hot_t[sl] * (rank + start_t), axis=1, keepdims=True).astype(jnp.int32)
        pt_scr[sl, :] = jnp.where(other == pos, 1.0, 0.0).astype(BF16)
        tok = lax.broadcasted_iota(jnp.int32, (tm, ch), 1) + c * ch
        other = lax.broadcasted_iota(jnp.int32, (tm, ch), 0)
        earlier = jnp.where(other < tok, 1.0, 0.0).astype(BF16)
        rank = _dot(hot_l_b, earlier)
        pos = jnp.sum(hot_l[:, sl] * (rank + start_l), axis=0, keepdims=True).astype(jnp.int32)
        p_scr[:, sl] = jnp.where(other == pos, 1.0, 0.0).astype(BF16)
    p = p_scr[...]
    hs_scr[...] = _dot(p, h2_ref[...]).astype(BF16)
    ws_scr[...] = _dot_split(wts, p, lhs_exact=True)
    acc_scr[...] = jnp.zeros_like(acc_scr)


def _moe_sorted_kernel(cnt_ref, *refs, final_norm, router_tiles):
    _moe_sorted_body(cnt_ref, *refs, final_norm=final_norm, router_tiles=router_tiles)


def _moe_sorted_paged_kernel(cnt_ref, ptab_ref, *refs, final_norm, router_tiles, paged_steps, steps_per_seq):
    n_moe_in = 9
    moe_in, q_ref = refs[:n_moe_in], refs[n_moe_in]
    pages = refs[n_moe_in + 1:n_moe_in + 1 + PAGES_PER_STEP]
    o_ref, sel_ref, p_scr, pt_scr, hs_scr, ws_scr, acc_scr, qb_ref, gacc_ref = refs[n_moe_in + 1 + PAGES_PER_STEP:]
    _moe_sorted_body(cnt_ref, *moe_in, o_ref, p_scr, pt_scr, hs_scr, ws_scr, acc_scr,
                     final_norm=final_norm, router_tiles=router_tiles)
    step = pl.program_id(0) * pl.num_programs(1) + pl.program_id(1)

    @pl.when(step < paged_steps)
    def _():
        _paged_gate_step(step % steps_per_seq, steps_per_seq - 1, q_ref, pages, sel_ref, qb_ref, gacc_ref)


def _moe_sorted_body(cnt_ref, x1_ref, h2_ref, wts_ref, wtst_ref, g2_ref, wg_ref, wu_ref, wd_ref, gfin_ref,
                     o_ref, p_scr, pt_scr, hs_scr, ws_scr, acc_scr, *, final_norm, router_tiles):
    i = pl.program_id(0)
    e = pl.program_id(1)
    ch = MOE_CHUNK
    counts = []
    for g in range(N_GROUPS):
        n = cnt_ref[i * router_tiles * N_GROUPS + g]
        for j in range(1, router_tiles):
            n = n + cnt_ref[(i * router_tiles + j) * N_GROUPS + g]
        counts.append(n)
    starts = [jnp.int32(0)]
    for g in range(N_GROUPS - 1):
        starts.append(starts[-1] + counts[g])

    @pl.when(e == 0)
    def _():
        _moe_sort(starts, h2_ref, wts_ref, wtst_ref, p_scr, pt_scr, hs_scr, ws_scr, acc_scr)

    group = e // EXP_PER_GROUP
    seg0, seg_n = starts[0], counts[0]
    for g in range(1, N_GROUPS):
        seg0 = jnp.where(group == g, starts[g], seg0)
        seg_n = jnp.where(group == g, counts[g], seg_n)
    lane = lax.broadcasted_iota(jnp.int32, (ch, LANES), 1)

    def chunk(c, carry):
        rows = pl.ds(pl.multiple_of(c * ch, ch), ch)
        hs = hs_scr[rows, :]
        wcol = jnp.sum(jnp.where(lane == e, ws_scr[rows, :], 0.0), axis=1, keepdims=True)
        act = _silu(_dot(hs, wg_ref[...])) * _dot(hs, wu_ref[...]) * wcol
        acc_scr[rows, :] += _dot(act.astype(BF16), wd_ref[...])
        return carry

    lax.fori_loop(seg0 // ch, (seg0 + seg_n + ch - 1) // ch, chunk, 0)

    @pl.when(e == pl.num_programs(1) - 1)
    def _():
        y = _dot(pt_scr[...], acc_scr[...].astype(BF16))
        x2 = x1_ref[...] + g2_ref[...] * y
        if final_norm:
            ms = jnp.mean(x2 * x2, axis=-1, keepdims=True)
            x2 = x2 * lax.rsqrt(ms + EPS) * gfin_ref[...]
        o_ref[...] = x2


def _moe_sorted(x1, h2, wts, wtst, gcnt, gate2, wg_b, wu_b, wd_b, w_layer, g_final, tm, final_norm, paged=None):
    s, d = x1.shape
    _, ne, _, f = wg_b.shape
    router_tiles = gcnt.shape[0] * tm // s
    assert s % tm == 0 and tm % MOE_CHUNK == 0 and tm % MOE_SORT_ROWS == 0 and router_tiles >= 1
    cnt_flat = gcnt[:, 0, :N_GROUPS].reshape(-1)
    n_steps = (s // tm) * ne
    fuse = False
    if paged is not None:
        cache_kt, layer, page_table_flat, q_col, n_pages = paged
        assert n_pages % PAGES_PER_STEP == 0 and PAGES_PER_STEP % PAGES_PER_BLOCK == 0
        steps_per_seq = n_pages // PAGES_PER_STEP
        paged_steps = q_col.shape[0] * steps_per_seq
        n_blocks = n_pages // PAGES_PER_BLOCK
        assert n_blocks >= MOBA_TOPK
        fuse = paged_steps <= n_steps
    n_pre = 2 if fuse else 1
    row = lambda width: pl.BlockSpec((tm, width), lambda i, e, *_: (i, 0))
    wspec = lambda r, c: pl.BlockSpec((None, None, r, c), lambda i, e, *_: (w_layer, e, 0, 0))
    vec = pl.BlockSpec((1, d), lambda i, e, *_: (0, 0))
    in_specs = [row(d), row(d), row(LANES), pl.BlockSpec((LANES, tm), lambda i, e, *_: (0, i)), vec,
                wspec(d, f), wspec(d, f), wspec(f, d), vec]
    out_specs = [row(d)]
    out_shape = [jax.ShapeDtypeStruct((s, d), F32)]
    scratch = [pltpu.VMEM((tm, tm), BF16),
               pltpu.VMEM((tm, tm), BF16),
               pltpu.VMEM((tm, d), BF16),
               pltpu.VMEM((tm, LANES), F32),
               pltpu.VMEM((tm, d), F32)]
    operands = [cnt_flat, x1, h2, wts, wtst, gate2, wg_b, wu_b, wd_b, g_final]
    body = functools.partial(_moe_sorted_kernel, final_norm=final_norm, router_tiles=router_tiles)
    if fuse:
        def seq_step(i, e):
            st = jnp.minimum(i * ne + e, paged_steps - 1)
            return st // steps_per_seq, st % steps_per_seq

        def page_spec(g):
            def imap(i, e, cnt, pt):
                bi, j = seq_step(i, e)
                return (layer, pt[bi * n_pages + j * PAGES_PER_STEP + g], 0, 0)
            return pl.BlockSpec((None, None, HEAD_W, PAGE_SIZE), imap)

        in_specs += [pl.BlockSpec((None, HEAD_W, 1), lambda i, e, cnt, pt: (seq_step(i, e)[0], 0, 0))]
        in_specs += [page_spec(g) for g in range(PAGES_PER_STEP)]
        out_specs += [pl.BlockSpec((None, SUBLANES, LANES), lambda i, e, cnt, pt: (seq_step(i, e)[0], 0, 0))]
        out_shape += [jax.ShapeDtypeStruct((q_col.shape[0], SUBLANES, LANES), jnp.int32)]
        scratch += [pltpu.VMEM((HEAD_W, PAGE_SIZE), F32),
                    pltpu.VMEM((n_blocks * ATT_HEADS * SUBLANES, PAGE_SIZE), F32)]
        operands = [cnt_flat, page_table_flat] + operands[1:] + [q_col] + [cache_kt] * PAGES_PER_STEP
        body = functools.partial(_moe_sorted_paged_kernel, final_norm=final_norm, router_tiles=router_tiles,
                                 paged_steps=paged_steps, steps_per_seq=steps_per_seq)
    grid_spec = pltpu.PrefetchScalarGridSpec(
        num_scalar_prefetch=n_pre, grid=(s // tm, ne), in_specs=in_specs, out_specs=out_specs,
        scratch_shapes=scratch)
    outs = pl.pallas_call(
        body,
        grid_spec=grid_spec,
        out_shape=out_shape,
        compiler_params=_params("arbitrary", "arbitrary"),
        name="moe_sorted",
    )(*operands)
    if fuse:
        return outs[0], outs[1]
    sel = None if paged is None else _paged_gate(cache_kt, layer, page_table_flat, q_col, n_pages)
    return outs[0], sel


def _paged_gate_kernel(pt_ref, q_ref, *rest):
    pages = rest[:PAGES_PER_STEP]
    sel_ref, qb_ref, acc_ref = rest[PAGES_PER_STEP:]
    _paged_gate_step(pl.program_id(1), pl.num_programs(1) - 1, q_ref, pages, sel_ref, qb_ref, acc_ref)


def _paged_gate_step(j, last_j, q_ref, pages, sel_ref, qb_ref, acc_ref):
    rows_per_block = ATT_HEADS * SUBLANES

    @pl.when(j == 0)
    def _():
        qb_ref[...] = jnp.broadcast_to(q_ref[...] * ATT_HD ** -0.5, qb_ref.shape)

    qb = qb_ref[...]
    blocks_per_step = PAGES_PER_STEP // PAGES_PER_BLOCK
    for t in range(blocks_per_step):
        tot = pages[PAGES_PER_BLOCK * t][...]
        for u in range(1, PAGES_PER_BLOCK):
            tot = tot + pages[PAGES_PER_BLOCK * t + u][...]
        part = (tot * qb).reshape(ATT_HEADS, ATT_HD // SUBLANES, SUBLANES, PAGE_SIZE).sum(axis=1)
        base = pl.multiple_of((j * blocks_per_step + t) * rows_per_block, rows_per_block)
        acc_ref[pl.ds(base, rows_per_block), :] = part.reshape(rows_per_block, PAGE_SIZE)

    @pl.when(j == last_j)
    def _():
        n_rows = acc_ref.shape[0]
        nb = n_rows // rows_per_block
        g = jnp.sum(acc_ref[...], axis=1, keepdims=True)
        r = lax.broadcasted_iota(jnp.int32, (n_rows, LANES), 0)
        lane = lax.broadcasted_iota(jnp.int32, (n_rows, LANES), 1)
        spread = jnp.where(lane == (r // SUBLANES) % ATT_HEADS, g, 0.0)
        gate = spread.reshape(nb, rows_per_block, LANES).sum(axis=1) * (1.0 / MOBA_BLOCK)
        _, idxs = _top3_rows(gate, nb)
        out = jnp.zeros(sel_ref.shape, jnp.int32)
        rr = lax.broadcasted_iota(jnp.int32, sel_ref.shape, 0)
        for t, idx in enumerate(idxs):
            out = jnp.where(rr == t, idx, out)
        sel_ref[...] = out


def _paged_gate(cache_kt, layer, page_table_flat, q_col, n_pages):
    b = q_col.shape[0]
    assert n_pages % PAGES_PER_STEP == 0 and PAGES_PER_STEP % PAGES_PER_BLOCK == 0
    nb = n_pages // PAGES_PER_BLOCK
    assert nb >= MOBA_TOPK

    def page_spec(g):
        return pl.BlockSpec((None, None, HEAD_W, PAGE_SIZE),
                            lambda bi, j, pt: (layer, pt[bi * n_pages + j * PAGES_PER_STEP + g], 0, 0))

    grid_spec = pltpu.PrefetchScalarGridSpec(
        num_scalar_prefetch=1,
        grid=(b, n_pages // PAGES_PER_STEP),
        in_specs=[pl.BlockSpec((None, HEAD_W, 1), lambda bi, j, pt: (bi, 0, 0))]
                 + [page_spec(g) for g in range(PAGES_PER_STEP)],
        out_specs=pl.BlockSpec((None, SUBLANES, LANES), lambda bi, j, pt: (bi, 0, 0)),
        scratch_shapes=[pltpu.VMEM((HEAD_W, PAGE_SIZE), F32),
                        pltpu.VMEM((nb * ATT_HEADS * SUBLANES, PAGE_SIZE), F32)],
    )
    return pl.pallas_call(
        _paged_gate_kernel,
        grid_spec=grid_spec,
        out_shape=jax.ShapeDtypeStruct((b, SUBLANES, LANES), jnp.int32),
        compiler_params=_params("parallel", "arbitrary"),
        name="paged_gate",
    )(page_table_flat, q_col, *([cache_kt] * PAGES_PER_STEP))


def _moba_sample_kernel(pt_ref, sel_ref, slopes_ref, q_ref, kn_ref, vn_ref, *rest, past_len):
    npg = MOBA_TOPK * PAGES_PER_BLOCK
    kp = rest[:npg]
    vp = rest[npg:2 * npg]
    o_ref = rest[2 * npg]
    b = pl.program_id(0)
    h = pl.program_id(1)
    slope = slopes_ref[h]
    lane = lax.broadcasted_iota(jnp.int32, (1, PAGE_SIZE), 1)
    q = q_ref[...] * ATT_HD ** -0.5
    scores = []
    for s in range(MOBA_TOPK):
        blk = sel_ref[(b * ATT_HEADS + h) * MOBA_TOPK + s]
        for t in range(PAGES_PER_BLOCK):
            kpos = blk * MOBA_BLOCK + t * PAGE_SIZE + lane
            dist = (past_len - kpos).astype(F32)
            sc = jnp.sum(kp[s * PAGES_PER_BLOCK + t][...] * q, axis=0, keepdims=True)
            scores.append(sc - slope * dist)
    s_own = jnp.sum(q * kn_ref[...], axis=0, keepdims=True)
    m = functools.reduce(jnp.maximum, [jnp.max(sc, axis=1, keepdims=True) for sc in scores] + [s_own])
    p_own = jnp.exp(s_own - m)
    l = p_own
    acc = p_own * vn_ref[...]
    for idx, sc in enumerate(scores):
        pr = jnp.exp(sc - m)
        l = l + jnp.sum(pr, axis=1, keepdims=True)
        acc = acc + jnp.sum(vp[idx][...] * pr, axis=1, keepdims=True)
    o_ref[...] = acc / l


def _moba_sample(cache_kt, cache_vt, layer, page_table_flat, sel_flat, q_col, k_col, v_col, n_pages):
    b = q_col.shape[0]
    npg = MOBA_TOPK * PAGES_PER_BLOCK
    h = jnp.arange(1, ATT_HEADS + 1, dtype=F32)
    slopes = 2.0 ** (-8.0 * h / ATT_HEADS)

    def page_spec(g):
        s, t = divmod(g, PAGES_PER_BLOCK)

        def imap(bi, hi, pt, sel):
            blk = sel[(bi * ATT_HEADS + hi) * MOBA_TOPK + s]
            return (layer, pt[bi * n_pages + blk * PAGES_PER_BLOCK + t], hi, 0)

        return pl.BlockSpec((None, None, ATT_HD, PAGE_SIZE), imap)

    vec = pl.BlockSpec((None, ATT_HD, 1), lambda bi, hi, pt, sel: (bi, hi, 0))
    grid_spec = pltpu.PrefetchScalarGridSpec(
        num_scalar_prefetch=2,
        grid=(b, ATT_HEADS),
        in_specs=[pl.BlockSpec(memory_space=pltpu.SMEM), vec, vec, vec]
                 + [page_spec(g) for g in range(npg)] * 2,
        out_specs=vec,
    )
    return pl.pallas_call(
        functools.partial(_moba_sample_kernel, past_len=n_pages * PAGE_SIZE),
        grid_spec=grid_spec,
        out_shape=jax.ShapeDtypeStruct((b, HEAD_W, 1), F32),
        compiler_params=_params("parallel", "arbitrary"),
        name="moba_sample",
    )(page_table_flat, sel_flat, slopes, q_col, k_col, v_col, *([cache_kt] * npg), *([cache_vt] * npg))


def _ret_sample_kernel(st_ref, q_ref, k_ref, v_ref, rg_ref, gamma_ref, ns_ref, o_ref):
    new = gamma_ref[...] * st_ref[...] + (k_ref[...] * RET_DK ** -0.5) * v_ref[...]
    ns_ref[...] = new
    o = jnp.sum(q_ref[...] * new, axis=1, keepdims=True)
    mu = jnp.mean(o, axis=-1, keepdims=True)
    dlt = o - mu
    var = jnp.mean(dlt * dlt, axis=-1, keepdims=True)
    o_ref[...] = _silu(rg_ref[...]) * (dlt * lax.rsqrt(var + EPS))


def _ret_sample(state, layer, rq, rk, rv, rg):
    b = state.shape[1]
    hh = jnp.arange(RET_HEADS, dtype=F32)
    gamma = jnp.exp(jnp.log1p(-(2.0 ** (-5.0 - hh)))).reshape(RET_HEADS, 1, 1)
    st_in = pl.BlockSpec((None, None, RET_HEADS, RET_DK, RET_DV), lambda i: (layer, i, 0, 0, 0))
    st_out = pl.BlockSpec((None, RET_HEADS, RET_DK, RET_DV), lambda i: (i, 0, 0, 0))
    col = pl.BlockSpec((None, RET_HEADS, RET_DK, 1), lambda i: (i, 0, 0, 0))
    rw = pl.BlockSpec((None, RET_HEADS, 1, RET_DV), lambda i: (i, 0, 0, 0))
    return pl.pallas_call(
        _ret_sample_kernel,
        grid=(b,),
        in_specs=[st_in, col, col, rw, rw, _full((RET_HEADS, 1, 1))],
        out_specs=(st_out, rw),
        out_shape=(jax.ShapeDtypeStruct(state.shape[1:], F32),
                   jax.ShapeDtypeStruct((b, RET_HEADS, 1, RET_DV), F32)),
        compiler_params=_params("parallel"),
        name="retention_sample",
    )(state, rq, rk, rv, rg, gamma)


def kernel(x_prompt, x_sample, cache_k, cache_v, state_ret, page_table, c_prompt, c_sample, w_ada, b_ada, g_mix, w_in, w_ret_o, w_att_o, w_out, g_ffn, w_rg, b_rg, w_re, b_re, w_gate_e, w_up_e, w_down_e, g_final):
    bp, s, d = x_prompt.shape
    bs, ds, _ = x_sample.shape
    assert bp == 1 and ds == 1
    depth = w_ada.shape[0]
    n_pool = cache_k.shape[1]
    n_pages = page_table.shape[1]
    w = HEAD_W

    n_rows = bp + bs
    pad = (-n_rows) % SUBLANES
    c_all = jnp.concatenate([c_prompt, c_sample, jnp.zeros((pad, d), F32)], axis=0)
    mods = _ada(c_all, w_ada, b_ada)

    cache_kt = cache_k.transpose(0, 1, 3, 4, 2).reshape(depth, n_pool, w, PAGE_SIZE)
    cache_vt = cache_v.transpose(0, 1, 3, 4, 2).reshape(depth, n_pool, w, PAGE_SIZE)
    pt_flat = page_table.reshape(-1)

    n_log = N_GROUPS + N_EXPERTS
    rr = 32
    g_fin = g_final.reshape(1, d)

    w_in_all = w_in.astype(BF16)
    wg_all = w_gate_e.astype(BF16)
    wu_all = w_up_e.astype(BF16)
    wd_all = w_down_e.astype(BF16)

    xp = x_prompt.reshape(s, d)
    xs = x_sample.reshape(bs, d)
    kp_l, vp_l, rp_l, ks_l, vs_l, rs_l = [], [], [], [], [], []
    for l in range(depth):
        last = l == depth - 1
        mp = [mods[l, 0:bp, j * d:(j + 1) * d] for j in range(6)]
        ms = [mods[l, bp:bp + bs, j * d:(j + 1) * d] for j in range(6)]
        w_kvt_b = w_in[l][:, 5 * w:7 * w].T.astype(BF16)
        wro_b = w_ret_o[l].astype(BF16)
        wao_b = w_att_o[l].astype(BF16)
        wo_b = w_out[l].astype(BF16)
        wr_t = jnp.concatenate([w_rg[l].T, w_re[l].T, jnp.zeros((rr - n_log, d), F32)], axis=0)
        br_col = jnp.concatenate([b_rg[l], b_re[l], jnp.zeros((rr - n_log,), F32)]).reshape(rr, 1)
        gm = g_mix[l].reshape(1, d)
        gf = g_ffn[l].reshape(1, d)

        proj = _proj_sample(xs, gm, ms[0], ms[1], w_in, l)
        cut = lambda j, n=1: proj[:, j * w:(j + n) * w]
        rq_s, rk_s, rv_s, rg_s, aq_s, ak_s, av_s = (cut(j) for j in range(7))
        gr_s, ga_s = cut(7, 2), cut(9, 2)
        q_col = aq_s.reshape(bs, w, 1)

        rq, rk, rv, rg, aq, akb, akt, avt, avtb, gr, ga, qc = _proj_prompt(xp, gm, mp[0], mp[1], w_in_all, l,
                                                                           w_kvt_b)
        ret_in, st = _retention_prompt(rq, rk, rv, rg, chunk=MOBA_BLOCK)
        att_in = _moba_prompt(aq, qc, akb, avtb)
        x1, h2, wts, wtst, gcnt = _merge_router(xp, ret_in, att_in, gr, ga, mp[2], wro_b, wao_b, wo_b, gf,
                                                mp[3], mp[4], wr_t, br_col, tm=512, exact=0)
        xp, sel_s = _moe_sorted(x1, h2, wts, wtst, gcnt, mp[5], wg_all, wu_all, wd_all, l, g_fin, tm=1024,
                                final_norm=last, paged=(cache_kt, l, pt_flat, q_col, n_pages))
        kp_l.append(akt)
        vp_l.append(avt)
        rp_l.append(st.reshape(bp, RET_HEADS, RET_DK, RET_DV))

        new_state, ret_s = _ret_sample(
            state_ret, l,
            rq_s.reshape(bs, RET_HEADS, RET_DK, 1), rk_s.reshape(bs, RET_HEADS, RET_DK, 1),
            rv_s.reshape(bs, RET_HEADS, 1, RET_DV), rg_s.reshape(bs, RET_HEADS, 1, RET_DV))
        sel_flat = sel_s[:, :MOBA_TOPK, :ATT_HEADS].transpose(0, 2, 1).reshape(-1)
        att_s = _moba_sample(cache_kt, cache_vt, l, pt_flat, sel_flat, q_col, ak_s.reshape(bs, w, 1),
                             av_s.reshape(bs, w, 1), n_pages)
        x1s, h2s, wts_s, _, _ = _merge_router(xs, ret_s.reshape(bs, w), att_s.reshape(bs, w), gr_s, ga_s, ms[2],
                                              w_ret_o, w_att_o, w_out, gf, ms[3], ms[4], wr_t, br_col,
                                              tm=bs, exact=l + 1)
        xs = _moe(x1s, h2s, wts_s, ms[5], w_gate_e, w_up_e, w_down_e, l, g_fin, tm=bs, final_norm=last,
                  exact=True)
        ks_l.append(ak_s.reshape(bs, 1, ATT_HEADS, ATT_HD))
        vs_l.append(av_s.reshape(bs, 1, ATT_HEADS, ATT_HD))
        rs_l.append(new_state)

    def kv_out(parts):
        return jnp.stack(parts).reshape(depth, bp, ATT_HEADS, ATT_HD, s).transpose(0, 1, 4, 2, 3)

    return (xp.reshape(bp, s, d), xs.reshape(bs, 1, d), kv_out(kp_l), kv_out(vp_l), jnp.stack(rp_l),
            jnp.stack(ks_l), jnp.stack(vs_l), jnp.stack(rs_l))
```
